```python
import math
import jax, jax.numpy as jnp
from jax import lax
import numpy as np

D_MODEL = 2048
BATCH = 4
SEQ = 2048
DEPTH = 4
DEC_BATCH = 128
DEC_SEQ = 1
PAST_LEN = 16384
PAGE_SIZE = 128

D_MIX = D_MODEL
M_HEADS = 4
M_DV = D_MIX // (2 * M_HEADS)
M_DK = M_DV // 2
G_HEADS = 8
G_DK = D_MIX // (2 * G_HEADS)
G_DV = G_DK
CONV_W = 4
G_CONV_CH = G_HEADS * (2 * G_DK + G_DV)
X_HEADS = 4
X_DH = 128
N_MEM = 256
D_FF = -(-8 * D_MODEL // (3 * 256)) * 256
CHUNK = 64
GATE_CAP = 15.0
EPS = 1e-6
SPLIT_SIZES = (M_HEADS * M_DK, M_HEADS * M_DK, M_HEADS * M_DV, M_HEADS * M_DV, M_HEADS, M_HEADS,
               G_HEADS * G_DK, G_HEADS * G_DK, G_HEADS * G_DV, G_HEADS * G_DV, G_HEADS, G_HEADS)
N_IN = sum(SPLIT_SIZES)
SPLIT_IDX = tuple(int(s) for s in np.cumsum(SPLIT_SIZES)[:-1])

kernel_name = 'hybrid_mlstm_gdn_memory_decoder_step'


def rmsnorm(x, g):
    xf = x.astype(jnp.float32)
    y = xf * lax.rsqrt(jnp.mean(xf * xf, axis=-1, keepdims=True) + EPS)
    return (y * g.astype(jnp.float32)).astype(x.dtype)


def l2norm(x):
    return x * lax.rsqrt(jnp.sum(x * x, axis=-1, keepdims=True) + EPS)


def softcap(x, cap):
    return cap * jnp.tanh(x / cap)


def to_chunks(a, L):
    B, T = a.shape[0], a.shape[1]
    return jnp.moveaxis(a.reshape((B, T // L, L) + a.shape[2:]), 1, 0)


def from_chunks(a):
    a = jnp.moveaxis(a, 0, 1)
    return a.reshape((a.shape[0], a.shape[1] * a.shape[2]) + a.shape[3:])


def causal_conv(x, buf, w):
    T = x.shape[1]
    xp = jnp.concatenate([buf.astype(x.dtype), x], axis=1)
    y = xp[:, 0:T] * w[0]
    for j in range(1, CONV_W):
        y = y + xp[:, j:j + T] * w[j]
    return jax.nn.silu(y), xp[:, -(CONV_W - 1):]


def mlstm_chunkwise(q, k, v, i_pre, logf, C0, n0, m0):
    L = math.gcd(q.shape[1], CHUNK)
    causal = jnp.tril(jnp.ones((L, L), dtype=bool))

    def step(carry, xs):
        C, n, m = carry
        qc, kc, vc, ic, fc = xs
        b = jnp.cumsum(fc, axis=1)
        D = b[:, :, None, :] - b[:, None, :, :] + ic[:, None, :, :]
        D = jnp.where(causal[None, :, :, None], D, -jnp.inf)
        inter = b + m[:, None, :]
        m_t = jnp.maximum(inter, jnp.max(D, axis=2))
        Wts = jnp.exp(D - m_t[:, :, None, :]) * jnp.einsum('bthd,bshd->btsh', qc, kc)
        a = jnp.exp(inter - m_t)
        num = a[..., None] * jnp.einsum('bthd,bhde->bthe', qc, C) + jnp.einsum('btsh,bshe->bthe', Wts, vc)
        nq = a * jnp.einsum('bthd,bhd->bth', qc, n) + jnp.sum(Wts, axis=2)
        h = num / jnp.maximum(jnp.abs(nq), jnp.exp(-m_t))[..., None]
        m_new = m_t[:, -1]
        decay = jnp.exp(b[:, -1] + m - m_new)
        w_s = jnp.exp(b[:, -1:, :] - b + ic - m_new[:, None, :])
        C_new = decay[..., None, None] * C + jnp.einsum('bsh,bshd,bshe->bhde', w_s, kc, vc)
        n_new = decay[..., None] * n + jnp.einsum('bsh,bshd->bhd', w_s, kc)
        return (C_new, n_new, m_new), h

    xs = (to_chunks(q, L), to_chunks(k, L), to_chunks(v, L), to_chunks(i_pre, L), to_chunks(logf, L))
    (C, n, m), h = lax.scan(step, (C0, n0, m0), xs)
    return from_chunks(h), C, n, m


def gdn_chunked(q, k, v, log_a, beta, S0):
    L = math.gcd(q.shape[1], CHUNK)
    incl = jnp.tril(jnp.ones((L, L), dtype=bool))
    strict = jnp.tril(jnp.ones((L, L), dtype=bool), -1)

    def step(S, xs):
        qc, kc, vc, gc, bc = xs
        g = jnp.cumsum(gc, axis=1)
        G = g[:, :, None, :] - g[:, None, :, :]
        decay_ts = jnp.exp(jnp.where(incl[None, :, :, None], G, -jnp.inf))
        kk = jnp.einsum('bthd,bshd->btsh', kc, kc)
        A = jnp.where(strict[None, :, :, None], decay_ts * kk * bc[:, None, :, :], 0.0)
        eg = jnp.exp(g)
        rhs = vc - eg[..., None] * jnp.einsum('bthd,bhde->bthe', kc, S)
        U = lax.linalg.triangular_solve(jnp.transpose(A, (0, 3, 1, 2)), jnp.transpose(rhs, (0, 2, 1, 3)),
                                        left_side=True, lower=True, unit_diagonal=True)
        wu = bc[..., None] * jnp.transpose(U, (0, 2, 1, 3))
        qk = jnp.einsum('bthd,bshd->btsh', qc, kc) * decay_ts
        o = eg[..., None] * jnp.einsum('bthd,bhde->bthe', qc, S) + jnp.einsum('btsh,bshe->bthe', qk, wu)
        wend = jnp.exp(g[:, -1:, :] - g)
        S_new = jnp.exp(g[:, -1])[..., None, None] * S + jnp.einsum('bsh,bshd,bshe->bhde', wend, kc, wu)
        return S_new, o

    xs = (to_chunks(q, L), to_chunks(k, L), to_chunks(v, L), to_chunks(log_a, L), to_chunks(beta, L))
    S, o = lax.scan(step, S0, xs)
    return from_chunks(o), S


def parallel_mixer(h, w_in, b_i, b_f, m_norm, conv_w, A_log, dt_bias, g_norm, w_out, C0, n0, m0, S0, conv0):
    f32 = jnp.float32
    B, T, _ = h.shape
    proj = h @ w_in
    mq, mk, mv, mo, mi, mf, gq, gk, gv, gz, ga, gb = jnp.split(proj, SPLIT_IDX, axis=-1)
    q = mq.reshape(B, T, M_HEADS, M_DK).astype(f32)
    k = mk.reshape(B, T, M_HEADS, M_DK).astype(f32) * (M_DK ** -0.5)
    v = mv.reshape(B, T, M_HEADS, M_DV).astype(f32)
    i_pre = softcap(mi.astype(f32) + b_i.astype(f32), GATE_CAP)
    logf = jax.nn.log_sigmoid(softcap(mf.astype(f32) + b_f.astype(f32), GATE_CAP))
    hm, C, n, m = mlstm_chunkwise(q, k, v, i_pre, logf, C0.astype(f32), n0.astype(f32), m0.astype(f32))
    hm = rmsnorm(hm, m_norm) * jax.nn.sigmoid(mo.reshape(B, T, M_HEADS, M_DV).astype(f32))
    qkv, conv_new = causal_conv(jnp.concatenate([gq, gk, gv], axis=-1), conv0, conv_w)
    cq, ck, cv = jnp.split(qkv, [G_HEADS * G_DK, 2 * G_HEADS * G_DK], axis=-1)
    q2 = l2norm(cq.reshape(B, T, G_HEADS, G_DK).astype(f32)) * (G_DK ** -0.5)
    k2 = l2norm(ck.reshape(B, T, G_HEADS, G_DK).astype(f32))
    v2 = cv.reshape(B, T, G_HEADS, G_DV).astype(f32)
    log_a = -jnp.exp(A_log.astype(f32)) * jax.nn.softplus(ga.astype(f32) + dt_bias.astype(f32))
    beta = jax.nn.sigmoid(gb.astype(f32))
    o, S = gdn_chunked(q2, k2, v2, log_a, beta, S0.astype(f32))
    o = rmsnorm(o, g_norm) * jax.nn.silu(gz.reshape(B, T, G_HEADS, G_DV).astype(f32))
    mixed = jnp.concatenate([hm.reshape(B, T, -1), o.reshape(B, T, -1)], axis=-1).astype(h.dtype)
    return mixed @ w_out, (C, n, m, S, conv_new)


def mem_kv(mem, g_mem, wk, wv):
    B = mem.shape[0]
    mh = rmsnorm(mem, g_mem)
    return ((mh @ wk).reshape(B, N_MEM, X_HEADS, X_DH), (mh @ wv).reshape(B, N_MEM, X_HEADS, X_DH))


def cross_attn(h, mk, mv, wq, wo):
    B, T, _ = h.shape
    q = (h @ wq).reshape(B, T, X_HEADS, X_DH)
    s = jnp.einsum('bthd,bmhd->bhtm', q, mk.astype(q.dtype)).astype(jnp.float32) * (X_DH ** -0.5)
    p = jax.nn.softmax(s, axis=-1).astype(h.dtype)
    o = jnp.einsum('bhtm,bmhd->bthd', p, mv.astype(h.dtype)).reshape(B, T, X_HEADS * X_DH)
    return o @ wo


def swiglu(h, w_gate, w_up, w_down):
    return (jax.nn.silu(h @ w_gate) * (h @ w_up)) @ w_down


def block(x, lp, mk, mv, C0, n0, m0, S0, conv0):
    (g_mix, w_in, b_i, b_f, m_norm, conv_w, A_log, dt_bias, g_norm, w_out,
     g_x, wq, wo, g_ffn, w_gate, w_up, w_down) = lp
    mix, st = parallel_mixer(rmsnorm(x, g_mix), w_in, b_i, b_f, m_norm, conv_w, A_log, dt_bias, g_norm, w_out,
                             C0, n0, m0, S0, conv0)
    x = x + mix
    x = x + cross_attn(rmsnorm(x, g_x), mk, mv, wq, wo)
    x = x + swiglu(rmsnorm(x, g_ffn), w_gate, w_up, w_down)
    return x, st


def setup_inputs(seed: int = 0) -> dict:
    key = jax.random.key(seed)
    ks = iter(jax.random.split(key, 48))
    f32 = jnp.float32

    def nrm(shape, scale):
        return scale * jax.random.normal(next(ks), shape, f32)

    def gain(shape):
        return 1.0 + nrm(shape, 0.02)

    d = {}
    d['x_prompt'] = nrm((BATCH, SEQ, D_MODEL), 1.0)
    d['x_sample'] = nrm((DEC_BATCH, DEC_SEQ, D_MODEL), 1.0)
    d['mem_prompt'] = nrm((BATCH, N_MEM, D_MODEL), 1.0)
    d['state_mlstm_C'] = nrm((DEPTH, DEC_BATCH, M_HEADS, M_DK, M_DV), 0.1)
    d['state_mlstm_n'] = nrm((DEPTH, DEC_BATCH, M_HEADS, M_DK), 0.1)
    d['state_mlstm_m'] = nrm((DEPTH, DEC_BATCH, M_HEADS), 1.0)
    d['state_gdn_S'] = nrm((DEPTH, DEC_BATCH, G_HEADS, G_DK, G_DV), 0.3)
    d['state_gdn_conv'] = nrm((DEPTH, DEC_BATCH, CONV_W - 1, G_CONV_CH), 1.0)
    d['cache_mem_k'] = nrm((DEPTH, DEC_BATCH, N_MEM, X_HEADS, X_DH), 1.0)
    d['cache_mem_v'] = nrm((DEPTH, DEC_BATCH, N_MEM, X_HEADS, X_DH), 1.0)
    d['norm_mix'] = gain((DEPTH, D_MODEL))
    d['w_in'] = nrm((DEPTH, D_MODEL, N_IN), D_MODEL ** -0.5)
    d['mlstm_b_i'] = nrm((DEPTH, M_HEADS), 0.1)
    d['mlstm_b_f'] = jnp.linspace(3.0, 6.0, M_HEADS, dtype=f32)[None, :] + nrm((DEPTH, M_HEADS), 0.1)
    d['mlstm_norm'] = gain((DEPTH, M_HEADS, M_DV))
    d['gdn_conv_w'] = nrm((DEPTH, CONV_W, G_CONV_CH), CONV_W ** -0.5)
    d['gdn_A_log'] = jnp.log(jax.random.uniform(next(ks), (DEPTH, G_HEADS), f32, 1.0, 16.0))
    dt = jnp.exp(jax.random.uniform(next(ks), (DEPTH, G_HEADS), f32, math.log(1e-3), math.log(1e-1)))
    d['gdn_dt_bias'] = dt + jnp.log(-jnp.expm1(-dt))
    d['gdn_norm'] = gain((DEPTH, G_HEADS, G_DV))
    d['w_out'] = nrm((DEPTH, D_MIX, D_MODEL), D_MIX ** -0.5)
    d['norm_xattn'] = gain((DEPTH, D_MODEL))
    d['norm_mem'] = gain((DEPTH, D_MODEL))
    d['xattn_wq'] = nrm((DEPTH, D_MODEL, X_HEADS * X_DH), D_MODEL ** -0.5)
    d['xattn_wk'] = nrm((DEPTH, D_MODEL, X_HEADS * X_DH), D_MODEL ** -0.5)
    d['xattn_wv'] = nrm((DEPTH, D_MODEL, X_HEADS * X_DH), D_MODEL ** -0.5)
    d['xattn_wo'] = nrm((DEPTH, X_HEADS * X_DH, D_MODEL), (X_HEADS * X_DH) ** -0.5)
    d['norm_ffn'] = gain((DEPTH, D_MODEL))
    d['ffn_w_gate'] = nrm((DEPTH, D_MODEL, D_FF), D_MODEL ** -0.5)
    d['ffn_w_up'] = nrm((DEPTH, D_MODEL, D_FF), D_MODEL ** -0.5)
    d['ffn_w_down'] = nrm((DEPTH, D_FF, D_MODEL), D_FF ** -0.5)
    d['norm_final'] = gain((D_MODEL,))
    return d


def reference(x_prompt, x_sample, mem_prompt, state_mlstm_C, state_mlstm_n, state_mlstm_m, state_gdn_S,
              state_gdn_conv, cache_mem_k, cache_mem_v, norm_mix, w_in, mlstm_b_i, mlstm_b_f, mlstm_norm,
              gdn_conv_w, gdn_A_log, gdn_dt_bias, gdn_norm, w_out, norm_xattn, norm_mem, xattn_wq, xattn_wk,
              xattn_wv, xattn_wo, norm_ffn, ffn_w_gate, ffn_w_up, ffn_w_down, norm_final):
    f32 = jnp.float32
    B = x_prompt.shape[0]
    yp, ys = x_prompt, x_sample
    pC, pn, pm, pS, pconv, pk, pv = [], [], [], [], [], [], []
    sC, sn, sm, sS, sconv = [], [], [], [], []
    for l in range(DEPTH):
        lp = (norm_mix[l], w_in[l], mlstm_b_i[l], mlstm_b_f[l], mlstm_norm[l], gdn_conv_w[l], gdn_A_log[l],
              gdn_dt_bias[l], gdn_norm[l], w_out[l], norm_xattn[l], xattn_wq[l], xattn_wo[l], norm_ffn[l],
              ffn_w_gate[l], ffn_w_up[l], ffn_w_down[l])
        mk_p, mv_p = mem_kv(mem_prompt, norm_mem[l], xattn_wk[l], xattn_wv[l])
        yp, (C, n, m, S, cv) = block(
            yp, lp, mk_p, mv_p,
            jnp.zeros((B, M_HEADS, M_DK, M_DV), f32), jnp.zeros((B, M_HEADS, M_DK), f32),
            jnp.zeros((B, M_HEADS), f32), jnp.zeros((B, G_HEADS, G_DK, G_DV), f32),
            jnp.zeros((B, CONV_W - 1, G_CONV_CH), x_prompt.dtype))
        pC.append(C); pn.append(n); pm.append(m); pS.append(S); pconv.append(cv); pk.append(mk_p); pv.append(mv_p)
        ys, (C, n, m, S, cv) = block(
            ys, lp, cache_mem_k[l], cache_mem_v[l], state_mlstm_C[l], state_mlstm_n[l], state_mlstm_m[l],
            state_gdn_S[l], state_gdn_conv[l])
        sC.append(C); sn.append(n); sm.append(m); sS.append(S); sconv.append(cv)
    y_prompt = rmsnorm(yp, norm_final)
    y_sample = rmsnorm(ys, norm_final)
    return (y_prompt, y_sample,
            jnp.stack(pC), jnp.stack(pn), jnp.stack(pm), jnp.stack(pS), jnp.stack(pconv), jnp.stack(pk), jnp.stack(pv),
            jnp.stack(sC), jnp.stack(sn), jnp.stack(sm), jnp.stack(sS), jnp.stack(sconv))
```

```python
import functools
import math

import jax
import jax.numpy as jnp
from jax import lax
from jax.experimental import pallas as pl
from jax.experimental.pallas import tpu as pltpu

F32 = jnp.float32
BF16 = jnp.bfloat16

D_MODEL = 2048
DEPTH = 4
M_HEADS = 4
M_DK = 128
M_DV = 256
G_HEADS = 8
G_DK = 128
G_DV = 128
CONV_W = 4
G_CONV_CH = G_HEADS * (2 * G_DK + G_DV)
X_HEADS = 4
X_DH = 128
N_MEM = 256
D_FF = 5632
GATE_CAP = 15.0
EPS = 1e-6

LANES = 128
N_BIG = 2 * M_HEADS * M_DK + 2 * M_HEADS * M_DV + 4 * G_HEADS * G_DK
ROWS_PER_TOKEN = N_BIG // LANES
GATE_I, GATE_F, GATE_A, GATE_B = 0, 4, 8, 16

VMEM_LIMIT = 52 * 1024 * 1024

M_CHUNK = 256
G_BLOCK = 256
G_CHUNK = 64
SAMPLE_BLOCK = 8

NT_DIMS = (((1,), (1,)), ((), ()))
TN_DIMS = (((0,), (0,)), ((), ()))


def _cparams(*sem):
    return pltpu.CompilerParams(dimension_semantics=sem, vmem_limit_bytes=VMEM_LIMIT)


def _dot(a, b):
    return jnp.dot(a, b, preferred_element_type=F32)


def _dot_nt(a, b):
    return lax.dot_general(a, b, NT_DIMS, preferred_element_type=F32)


def _dot_tn(a, b):
    return lax.dot_general(a, b, TN_DIMS, preferred_element_type=F32)


def _rms(x, g):
    ms = jnp.mean(x * x, axis=-1, keepdims=True)
    return x * lax.rsqrt(ms + EPS) * g


def _sigmoid(x):
    return 1.0 / (1.0 + jnp.exp(-x))


def _softplus(x):
    return jnp.maximum(x, 0.0) + jnp.log1p(jnp.exp(-jnp.abs(x)))


def _softcap(x):
    return GATE_CAP * jnp.tanh(x / GATE_CAP)


def _proj_kernel(x_ref, g_ref, w_ref, wg_ref, o_ref, og_ref, h_scr):
    @pl.when(pl.program_id(1) == 0)
    def _():
        h = _rms(x_ref[...], g_ref[...]).astype(BF16)
        h_scr[...] = h
        og_ref[...] = _dot(h, wg_ref[...])

    o_ref[...] = _dot(h_scr[...], w_ref[...])


def _proj(x, g, w_big, w_gate, tm, tn=1024):
    m = x.shape[0]
    return pl.pallas_call(
        _proj_kernel,
        grid=(m // tm, N_BIG // tn),
        in_specs=[
            pl.BlockSpec((tm, D_MODEL), lambda i, j: (i, 0)),
            pl.BlockSpec((1, D_MODEL), lambda i, j: (0, 0)),
            pl.BlockSpec((D_MODEL, tn), lambda i, j: (0, j)),
            pl.BlockSpec((D_MODEL, LANES), lambda i, j: (0, 0)),
        ],
        out_specs=[
            pl.BlockSpec((tm, tn), lambda i, j: (i, j)),
            pl.BlockSpec((tm, LANES), lambda i, j: (i, 0)),
        ],
        out_shape=[jax.ShapeDtypeStruct((m, N_BIG), F32), jax.ShapeDtypeStruct((m, LANES), F32)],
        scratch_shapes=[pltpu.VMEM((tm, D_MODEL), BF16)],
        compiler_params=_cparams("parallel", "arbitrary"),
        name="proj",
    )(x, g, w_big, w_gate)


def _norm_mm_kernel(x_ref, g_ref, w_ref, o_ref):
    o_ref[...] = _dot(_rms(x_ref[...], g_ref[...]).astype(BF16), w_ref[...])


def _norm_mm(x, g, w, tm):
    m, n = x.shape[0], w.shape[1]
    return pl.pallas_call(
        _norm_mm_kernel,
        grid=(m // tm,),
        in_specs=[
            pl.BlockSpec((tm, D_MODEL), lambda i: (i, 0)),
            pl.BlockSpec((1, D_MODEL), lambda i: (0, 0)),
            pl.BlockSpec((D_MODEL, n), lambda i: (0, 0)),
        ],
        out_specs=pl.BlockSpec((tm, n), lambda i: (i, 0)),
        out_shape=jax.ShapeDtypeStruct((m, n), F32),
        compiler_params=_cparams("parallel"),
        name="mem_kv",
    )(x, g, w)


def _mix_out_kernel(x_ref, hm_ref, og_ref, w1_ref, w2_ref, o_ref):
    acc = _dot(hm_ref[...], w1_ref[...])
    acc = acc + _dot(og_ref[...], w2_ref[...])
    o_ref[...] = x_ref[...] + acc


def _mix_out(x, hm, og, w_out, tm):
    m = x.shape[0]
    half = D_MODEL // 2
    return pl.pallas_call(
        _mix_out_kernel,
        grid=(m // tm,),
        in_specs=[
            pl.BlockSpec((tm, D_MODEL), lambda i: (i, 0)),
            pl.BlockSpec((tm, half), lambda i: (i, 0)),
            pl.BlockSpec((tm, half), lambda i: (i, 0)),
            pl.BlockSpec((half, D_MODEL), lambda i: (0, 0)),
            pl.BlockSpec((half, D_MODEL), lambda i: (1, 0)),
        ],
        out_specs=pl.BlockSpec((tm, D_MODEL), lambda i: (i, 0)),
        out_shape=jax.ShapeDtypeStruct((m, D_MODEL), F32),
        compiler_params=_cparams("parallel"),
        name="mix_out",
    )(x, hm, og, w_out, w_out)


def _ffn_kernel(x_ref, g_ref, wg_ref, wu_ref, wd_ref, o_ref, h_scr, acc_scr):
    j = pl.program_id(1)

    @pl.when(j == 0)
    def _():
        h_scr[...] = _rms(x_ref[...], g_ref[...]).astype(BF16)
        acc_scr[...] = jnp.zeros_like(acc_scr)

    h = h_scr[...]
    a = _dot(h, wg_ref[...])
    b = _dot(h, wu_ref[...])
    t = (a * _sigmoid(a)) * b
    acc_scr[...] += _dot(t.astype(BF16), wd_ref[...])

    @pl.when(j == pl.num_programs(1) - 1)
    def _():
        o_ref[...] = x_ref[...] + acc_scr[...]


def _ffn(x, g, w_gate, w_up, w_down, tm, tf=512):
    m = x.shape[0]
    return pl.pallas_call(
        _ffn_kernel,
        grid=(m // tm, D_FF // tf),
        in_specs=[
            pl.BlockSpec((tm, D_MODEL), lambda i, j: (i, 0)),
            pl.BlockSpec((1, D_MODEL), lambda i, j: (0, 0)),
            pl.BlockSpec((D_MODEL, tf), lambda i, j: (0, j)),
            pl.BlockSpec((D_MODEL, tf), lambda i, j: (0, j)),
            pl.BlockSpec((tf, D_MODEL), lambda i, j: (j, 0)),
        ],
        out_specs=pl.BlockSpec((tm, D_MODEL), lambda i, j: (i, 0)),
        out_shape=jax.ShapeDtypeStruct((m, D_MODEL), F32),
        scratch_shapes=[pltpu.VMEM((tm, D_MODEL), BF16), pltpu.VMEM((tm, D_MODEL), F32)],
        compiler_params=_cparams("parallel", "arbitrary"),
        name="ffn",
    )(x, g, w_gate, w_up, w_down)


def _final_norm_kernel(x_ref, g_ref, o_ref):
    o_ref[...] = _rms(x_ref[...], g_ref[...])


def _final_norm(x, g, tm):
    m = x.shape[0]
    return pl.pallas_call(
        _final_norm_kernel,
        grid=(m // tm,),
        in_specs=[pl.BlockSpec((tm, D_MODEL), lambda i: (i, 0)), pl.BlockSpec((1, D_MODEL), lambda i: (0, 0))],
        out_specs=pl.BlockSpec((tm, D_MODEL), lambda i: (i, 0)),
        out_shape=jax.ShapeDtypeStruct((m, D_MODEL), F32),
        compiler_params=_cparams("parallel"),
        name="final_norm",
    )(x, g)


def _softmax_rows(s):
    s = s - jnp.max(s, axis=-1, keepdims=True)
    p = jnp.exp(s)
    return p / jnp.sum(p, axis=-1, keepdims=True)


def _xattn_prompt_kernel(x_ref, g_ref, wq_ref, k_ref, v_ref, wo_ref, o_ref):
    x = x_ref[...]
    q = _dot(_rms(x, g_ref[...]).astype(BF16), wq_ref[...])
    heads = []
    for h in range(X_HEADS):
        sl = slice(h * X_DH, (h + 1) * X_DH)
        s = _dot_nt(q[:, sl].astype(BF16), k_ref[:, sl].astype(BF16)) * (X_DH ** -0.5)
        p = _softmax_rows(s)
        heads.append(_dot(p.astype(BF16), v_ref[:, sl].astype(BF16)).astype(BF16))
    o = jnp.concatenate(heads, axis=1)
    o_ref[...] = x + _dot(o, wo_ref[...])


def _xattn_prompt(x, g, wq, mk, mv, wo, tm, seq):
    m = x.shape[0]
    dq = X_HEADS * X_DH
    per_seq = seq // tm
    return pl.pallas_call(
        _xattn_prompt_kernel,
        grid=(m // tm,),
        in_specs=[
            pl.BlockSpec((tm, D_MODEL), lambda i: (i, 0)),
            pl.BlockSpec((1, D_MODEL), lambda i: (0, 0)),
            pl.BlockSpec((D_MODEL, dq), lambda i: (0, 0)),
            pl.BlockSpec((None, N_MEM, dq), lambda i: (i // per_seq, 0, 0)),
            pl.BlockSpec((None, N_MEM, dq), lambda i: (i // per_seq, 0, 0)),
            pl.BlockSpec((dq, D_MODEL), lambda i: (0, 0)),
        ],
        out_specs=pl.BlockSpec((tm, D_MODEL), lambda i: (i, 0)),
        out_shape=jax.ShapeDtypeStruct((m, D_MODEL), F32),
        compiler_params=_cparams("parallel"),
        name="xattn_prompt",
    )(x, g, wq, mk, mv, wo)


def _xattn_sample_kernel(x_ref, g_ref, wq_ref, k_ref, v_ref, wo_ref, o_ref):
    nb = x_ref.shape[0]
    dq = X_HEADS * X_DH
    x = x_ref[...]
    q = _dot(_rms(x, g_ref[...]).astype(BF16), wq_ref[...])
    row = lax.broadcasted_iota(jnp.int32, (8, dq), 0)
    head_of_lane = jnp.right_shift(lax.broadcasted_iota(jnp.int32, (8, dq), 1), int(math.log2(X_DH)))
    own = row == head_of_lane
    outs = []
    for s in range(nb):
        qm = jnp.where(own, jnp.broadcast_to(q[s:s + 1, :], (8, dq)), 0.0)
        sc = _dot_nt(qm.astype(BF16), k_ref[s].astype(BF16)) * (X_DH ** -0.5)
        p = _softmax_rows(sc)
        o8 = _dot(p.astype(BF16), v_ref[s].astype(BF16))
        outs.append(jnp.sum(jnp.where(own, o8, 0.0), axis=0, keepdims=True))
    o = jnp.concatenate(outs, axis=0).astype(BF16)
    o_ref[...] = x + _dot(o, wo_ref[...])


def _xattn_sample(x, g, wq, mk, mv, wo):
    m = x.shape[0]
    nb = SAMPLE_BLOCK
    dq = X_HEADS * X_DH
    return pl.pallas_call(
        _xattn_sample_kernel,
        grid=(m // nb,),
        in_specs=[
            pl.BlockSpec((nb, D_MODEL), lambda i: (i, 0)),
            pl.BlockSpec((1, D_MODEL), lambda i: (0, 0)),
            pl.BlockSpec((D_MODEL, dq), lambda i: (0, 0)),
            pl.BlockSpec((nb, N_MEM, dq), lambda i: (i, 0, 0)),
            pl.BlockSpec((nb, N_MEM, dq), lambda i: (i, 0, 0)),
            pl.BlockSpec((dq, D_MODEL), lambda i: (0, 0)),
        ],
        out_specs=pl.BlockSpec((nb, D_MODEL), lambda i: (i, 0)),
        out_shape=jax.ShapeDtypeStruct((m, D_MODEL), F32),
        compiler_params=_cparams("parallel"),
        name="xattn_sample",
    )(x, g, wq, mk, mv, wo)


def _mlstm_prompt_kernel(q_ref, k_ref, v_ref, mo_ref, gt_ref, gb_ref, mn_ref,
                         hm_ref, c_ref, n_ref, m_ref, c_scr, n_scr, m_scr):
    L = q_ref.shape[0]
    c = pl.program_id(1)

    @pl.when(c == 0)
    def _():
        c_scr[...] = jnp.zeros_like(c_scr)
        n_scr[...] = jnp.zeros_like(n_scr)
        m_scr[...] = jnp.zeros_like(m_scr)

    gates = gt_ref[...] + gb_ref[...]
    capped = _softcap(gates)
    logf = -_softplus(-capped)
    row = lax.broadcasted_iota(jnp.int32, (L, L), 0)
    col = lax.broadcasted_iota(jnp.int32, (L, L), 1)
    causal = row >= col
    bcum = jnp.dot(causal.astype(F32), logf, preferred_element_type=F32,
                   precision=lax.Precision.HIGHEST)
    lane = lax.broadcasted_iota(jnp.int32, (L, LANES), 1)
    z = jnp.where(lane < GATE_F, capped, bcum)
    zt = z.T

    for h in range(M_HEADS):
        i_col = z[:, GATE_I + h:GATE_I + h + 1]
        b_col = z[:, GATE_F + h:GATE_F + h + 1]
        i_row = zt[GATE_I + h:GATE_I + h + 1, :]
        b_row = zt[GATE_F + h:GATE_F + h + 1, :]
        m_prev = m_scr[0:1, h:h + 1]
        d = jnp.where(causal, b_col + (i_row - b_row), -jnp.inf)
        inter = b_col + m_prev
        m_t = jnp.maximum(inter, jnp.max(d, axis=1, keepdims=True))
        q = q_ref[:, h * M_DK:(h + 1) * M_DK]
        k = k_ref[:, h * M_DK:(h + 1) * M_DK] * (M_DK ** -0.5)
        vb = v_ref[:, h * M_DV:(h + 1) * M_DV].astype(BF16)
        qb = q.astype(BF16)
        w = jnp.exp(d - m_t) * _dot_nt(qb, k.astype(BF16))
        a = jnp.exp(inter - m_t)
        c_old = c_scr[h]
        n_old = n_scr[h]
        num = a * _dot(qb, c_old.astype(BF16)) + _dot(w.astype(BF16), vb)
        nq = a * jnp.sum(q * n_old, axis=1, keepdims=True) + jnp.sum(w, axis=1, keepdims=True)
        hh = num / jnp.maximum(jnp.abs(nq), jnp.exp(-m_t))
        y = _rms(hh, mn_ref[:, h * M_DV:(h + 1) * M_DV])
        y = y * _sigmoid(mo_ref[:, h * M_DV:(h + 1) * M_DV])
        hm_ref[:, h * M_DV:(h + 1) * M_DV] = y.astype(BF16)

        m_new = m_t[L - 1:L, :]
        b_last = b_col[L - 1:L, :]
        decay = jnp.exp(b_last + m_prev - m_new)
        kw = k * jnp.exp(b_last - b_col + i_col - m_new)
        c_scr[h] = decay * c_old + _dot_tn(kw.astype(BF16), vb)
        n_scr[h] = decay * n_old + jnp.sum(kw, axis=0, keepdims=True)
        m_scr[0:1, h:h + 1] = m_new

    @pl.when(c == pl.num_programs(1) - 1)
    def _():
        c_ref[...] = c_scr[...]
        n_ref[...] = n_scr[...]
        m_ref[...] = m_scr[...]


def _mlstm_prompt(proj3, gates3, gate_bias, m_norm):
    b, t, _ = proj3.shape
    L = M_CHUNK
    dqk = M_HEADS * M_DK
    dv = M_HEADS * M_DV
    return pl.pallas_call(
        _mlstm_prompt_kernel,
        grid=(b, t // L),
        in_specs=[
            pl.BlockSpec((None, L, dqk), lambda i, c: (i, c, 0)),
            pl.BlockSpec((None, L, dqk), lambda i, c: (i, c, 1)),
            pl.BlockSpec((None, L, dv), lambda i, c: (i, c, 1)),
            pl.BlockSpec((None, L, dv), lambda i, c: (i, c, 2)),
            pl.BlockSpec((None, L, LANES), lambda i, c: (i, c, 0)),
            pl.BlockSpec((1, LANES), lambda i, c: (0, 0)),
            pl.BlockSpec((1, dv), lambda i, c: (0, 0)),
        ],
        out_specs=[
            pl.BlockSpec((None, L, dv), lambda i, c: (i, c, 0)),
            pl.BlockSpec((None, M_HEADS, M_DK, M_DV), lambda i, c: (i, 0, 0, 0)),
            pl.BlockSpec((None, M_HEADS, 1, M_DK), lambda i, c: (i, 0, 0, 0)),
            pl.BlockSpec((None, 1, LANES), lambda i, c: (i, 0, 0)),
        ],
        out_shape=[
            jax.ShapeDtypeStruct((b, t, dv), BF16),
            jax.ShapeDtypeStruct((b, M_HEADS, M_DK, M_DV), F32),
            jax.ShapeDtypeStruct((b, M_HEADS, 1, M_DK), F32),
            jax.ShapeDtypeStruct((b, 1, LANES), F32),
        ],
        scratch_shapes=[
            pltpu.VMEM((M_HEADS, M_DK, M_DV), F32),
            pltpu.VMEM((M_HEADS, 1, M_DK), F32),
            pltpu.VMEM((1, LANES), F32),
        ],
        compiler_params=_cparams("parallel", "arbitrary"),
        name="mlstm_prompt",
    )(proj3, proj3, proj3, proj3, gates3, gate_bias, m_norm)


def _row256(ref, r):
    return jnp.concatenate([ref[r:r + 1, :], ref[r + 1:r + 2, :]], axis=1)


def _mlstm_sample_kernel(pr_ref, gt_ref, gb_ref, mn_ref, c_ref, n_ref, m_ref,
                         hm_ref, co_ref, no_ref, mo_ref):
    nb = gt_ref.shape[0]
    R = ROWS_PER_TOKEN
    gates = gt_ref[...] + gb_ref[...]
    capped = _softcap(gates)
    logf = -_softplus(-capped)
    i4 = capped[:, GATE_I:GATE_I + M_HEADS]
    f4 = logf[:, GATE_F:GATE_F + M_HEADS]
    m_prev = m_ref[...]
    inter = f4 + m_prev
    m_new = jnp.maximum(inter, i4)
    a4 = jnp.exp(inter - m_new)
    w4 = jnp.exp(i4 - m_new)
    e4 = jnp.exp(-m_new)
    mo_ref[...] = m_new

    qk = jnp.concatenate([pr_ref[s * R:s * R + 8, :] for s in range(nb)]
                         + [jnp.zeros((LANES - 8 * nb, LANES), F32)], axis=0)
    qkt = qk.T
    scale = M_DK ** -0.5
    out_rows = []
    for s in range(nb):
        head_rows = []
        for h in range(M_HEADS):
            a = a4[s:s + 1, h:h + 1]
            wi = w4[s:s + 1, h:h + 1]
            q_row = pr_ref[s * R + h:s * R + h + 1, :]
            k_row = pr_ref[s * R + 4 + h:s * R + 5 + h, :] * scale
            q_col = qkt[:, s * 8 + h:s * 8 + h + 1]
            k_col = qkt[:, s * 8 + 4 + h:s * 8 + 5 + h] * scale
            v_row = _row256(pr_ref, s * R + 8 + 2 * h)
            mo_row = _row256(pr_ref, s * R + 16 + 2 * h)
            n_row = n_ref[s, h:h + 1, :]
            c_old = c_ref[s, h]
            wts = wi * jnp.sum(q_row * k_row, axis=1, keepdims=True)
            nq = a * jnp.sum(q_row * n_row, axis=1, keepdims=True) + wts
            num = a * jnp.sum(q_col * c_old, axis=0, keepdims=True) + wts * v_row
            hh = num / jnp.maximum(jnp.abs(nq), e4[s:s + 1, h:h + 1])
            y = _rms(hh, mn_ref[:, h * M_DV:(h + 1) * M_DV]) * _sigmoid(mo_row)
            head_rows.append(y)
            co_ref[s, h] = a * c_old + (wi * k_col) * v_row
            no_ref[s, h:h + 1, :] = a * n_row + wi * k_row
        out_rows.append(jnp.concatenate(head_rows, axis=1))
    hm_ref[...] = jnp.concatenate(out_rows, axis=0).astype(BF16)


def _mlstm_sample(proj_rows, gates, gate_bias, m_norm, c0, n0, m0):
    b = gates.shape[0]
    nb = SAMPLE_BLOCK
    dv = M_HEADS * M_DV
    return pl.pallas_call(
        _mlstm_sample_kernel,
        grid=(b // nb,),
        in_specs=[
            pl.BlockSpec((nb * ROWS_PER_TOKEN, LANES), lambda i: (i, 0)),
            pl.BlockSpec((nb, LANES), lambda i: (i, 0)),
            pl.BlockSpec((1, LANES), lambda i: (0, 0)),
            pl.BlockSpec((1, dv), lambda i: (0, 0)),
            pl.BlockSpec((nb, M_HEADS, M_DK, M_DV), lambda i: (i, 0, 0, 0)),
            pl.BlockSpec((nb, M_HEADS, M_DK), lambda i: (i, 0, 0)),
            pl.BlockSpec((nb, M_HEADS), lambda i: (i, 0)),
        ],
        out_specs=[
            pl.BlockSpec((nb, dv), lambda i: (i, 0)),
            pl.BlockSpec((nb, M_HEADS, M_DK, M_DV), lambda i: (i, 0, 0, 0)),
            pl.BlockSpec((nb, M_HEADS, M_DK), lambda i: (i, 0, 0)),
            pl.BlockSpec((nb, M_HEADS), lambda i: (i, 0)),
        ],
        out_shape=[
            jax.ShapeDtypeStruct((b, dv), BF16),
            jax.ShapeDtypeStruct(c0.shape, F32),
            jax.ShapeDtypeStruct(n0.shape, F32),
            jax.ShapeDtypeStruct(m0.shape, F32),
        ],
        compiler_params=_cparams("parallel"),
        name="mlstm_sample",
    )(proj_rows, gates, gate_bias, m_norm, c0, n0, m0)


def _gdn_gates(gates, alog):
    log_a = -jnp.exp(alog) * _softplus(gates)
    beta = _sigmoid(gates)
    return log_a, beta


def _gdn_prompt_kernel(x_ref, gz_ref, gt_ref, gb_ref, al_ref, cw_ref, gn_ref,
                       og_ref, s_ref, cv_ref, s_scr, tail_scr):
    Lb = x_ref.shape[0]
    L = G_CHUNK
    c = pl.program_id(1)

    @pl.when(c == 0)
    def _():
        s_scr[...] = jnp.zeros_like(s_scr)
        tail_scr[...] = jnp.zeros_like(tail_scr)

    log_a, beta = _gdn_gates(gt_ref[...] + gb_ref[...], al_ref[...])
    row = lax.broadcasted_iota(jnp.int32, (Lb, Lb), 0)
    col = lax.broadcasted_iota(jnp.int32, (Lb, Lb), 1)
    shift = int(math.log2(L))
    same_chunk = jnp.right_shift(row, shift) == jnp.right_shift(col, shift)
    tri = jnp.where(same_chunk & (row >= col), 1.0, 0.0)
    gcum = jnp.dot(tri, log_a, preferred_element_type=F32, precision=lax.Precision.HIGHEST)
    lane = lax.broadcasted_iota(jnp.int32, (Lb, LANES), 1)
    z = jnp.where(lane < GATE_B, gcum, beta)
    zt = z.T

    r64 = lax.broadcasted_iota(jnp.int32, (L, L), 0)
    c64 = lax.broadcasted_iota(jnp.int32, (L, L), 1)
    incl = r64 >= c64
    strict = r64 > c64

    def conv_cols(c0):
        xc = x_ref[:, c0:c0 + LANES]
        xp = jnp.concatenate([tail_scr[:, c0:c0 + LANES], xc], axis=0)
        y = xc * cw_ref[3:4, c0:c0 + LANES]
        for j in range(CONV_W - 1):
            y = y + xp[5 + j:5 + j + Lb, :] * cw_ref[j:j + 1, c0:c0 + LANES]
        return y * _sigmoid(y)

    for h in range(G_HEADS):
        qc = conv_cols(h * G_DK)
        kc = conv_cols(G_HEADS * G_DK + h * G_DK)
        vc = conv_cols(2 * G_HEADS * G_DK + h * G_DV)
        q2 = qc * lax.rsqrt(jnp.sum(qc * qc, axis=1, keepdims=True) + EPS) * (G_DK ** -0.5)
        k2 = kc * lax.rsqrt(jnp.sum(kc * kc, axis=1, keepdims=True) + EPS)
        s_state = s_scr[h]
        for ci in range(Lb // L):
            r0 = ci * L
            g_col = z[r0:r0 + L, GATE_A + h:GATE_A + h + 1]
            b_col = z[r0:r0 + L, GATE_B + h:GATE_B + h + 1]
            g_row = zt[GATE_A + h:GATE_A + h + 1, r0:r0 + L]
            b_row = zt[GATE_B + h:GATE_B + h + 1, r0:r0 + L]
            qb = q2[r0:r0 + L, :].astype(BF16)
            k_c = k2[r0:r0 + L, :]
            kb = k_c.astype(BF16)
            v_c = vc[r0:r0 + L, :]
            dec = jnp.exp(jnp.where(incl, g_col - g_row, -jnp.inf))
            a_mat = jnp.where(strict, dec * _dot_nt(kb, kb) * b_row, 0.0)
            mpow = -a_mat
            qacc = mpow
            for _ in range(int(math.log2(L)) - 1):
                mb = mpow.astype(BF16)
                mpow = _dot(mb, mb)
                qacc = qacc + mpow + _dot(qacc.astype(BF16), mpow.astype(BF16))
            eg = jnp.exp(g_col)
            sb = s_state.astype(BF16)
            rhs = v_c - eg * _dot(kb, sb)
            u = rhs + _dot(qacc.astype(BF16), rhs.astype(BF16))
            wu = (b_col * u).astype(BF16)
            o = eg * _dot(qb, sb) + _dot((_dot_nt(qb, kb) * dec).astype(BF16), wu)
            g_last = g_col[L - 1:L, :]
            kd = (k_c * jnp.exp(g_last - g_col)).astype(BF16)
            s_state = jnp.exp(g_last) * s_state + _dot_tn(kd, wu)
            y = _rms(o, gn_ref[:, h * G_DV:(h + 1) * G_DV])
            gz = gz_ref[r0:r0 + L, h * G_DV:(h + 1) * G_DV]
            og_ref[r0:r0 + L, h * G_DV:(h + 1) * G_DV] = (y * (gz * _sigmoid(gz))).astype(BF16)
        s_scr[h] = s_state

    tail_scr[...] = x_ref[Lb - 8:Lb, :]

    @pl.when(c == pl.num_programs(1) - 1)
    def _():
        s_ref[...] = s_scr[...]
        cv_ref[...] = x_ref[Lb - (CONV_W - 1):Lb, :]


def _gdn_prompt(proj3, gates3, gate_bias, alog, conv_w, g_norm):
    b, t, _ = proj3.shape
    Lb = G_BLOCK
    dv = G_HEADS * G_DV
    return pl.pallas_call(
        _gdn_prompt_kernel,
        grid=(b, t // Lb),
        in_specs=[
            pl.BlockSpec((None, Lb, G_CONV_CH), lambda i, c: (i, c, 1)),
            pl.BlockSpec((None, Lb, dv), lambda i, c: (i, c, 6)),
            pl.BlockSpec((None, Lb, LANES), lambda i, c: (i, c, 0)),
            pl.BlockSpec((1, LANES), lambda i, c: (0, 0)),
            pl.BlockSpec((1, LANES), lambda i, c: (0, 0)),
            pl.BlockSpec((CONV_W, G_CONV_CH), lambda i, c: (0, 0)),
            pl.BlockSpec((1, dv), lambda i, c: (0, 0)),
        ],
        out_specs=[
            pl.BlockSpec((None, Lb, dv), lambda i, c: (i, c, 0)),
            pl.BlockSpec((None, G_HEADS, G_DK, G_DV), lambda i, c: (i, 0, 0, 0)),
            pl.BlockSpec((None, CONV_W - 1, G_CONV_CH), lambda i, c: (i, 0, 0)),
        ],
        out_shape=[
            jax.ShapeDtypeStruct((b, t, dv), BF16),
            jax.ShapeDtypeStruct((b, G_HEADS, G_DK, G_DV), F32),
            jax.ShapeDtypeStruct((b, CONV_W - 1, G_CONV_CH), F32),
        ],
        scratch_shapes=[
            pltpu.VMEM((G_HEADS, G_DK, G_DV), F32),
            pltpu.VMEM((8, G_CONV_CH), F32),
        ],
        compiler_params=_cparams("parallel", "arbitrary"),
        name="gdn_prompt",
    )(proj3, proj3, gates3, gate_bias, alog, conv_w, g_norm)


def _gdn_sample_kernel(pr_ref, gt_ref, gb_ref, al_ref, cw_ref, gn_ref, s_ref, cb_ref,
                       og_ref, so_ref, co_ref):
    nb = gt_ref.shape[0]
    R = ROWS_PER_TOKEN
    CR = G_CONV_CH // LANES
    X0 = (2 * M_HEADS * M_DK + 2 * M_HEADS * M_DV) // LANES
    log_a, beta = _gdn_gates(gt_ref[...] + gb_ref[...], al_ref[...])
    eg8 = jnp.exp(log_a)

    ys = []
    for s in range(nb):
        xr = pr_ref[s * R + X0:s * R + X0 + CR, :]
        y = xr * cw_ref[CONV_W - 1]
        for j in range(CONV_W - 1):
            y = y + cb_ref[s, j] * cw_ref[j]
        co_ref[s, 0] = cb_ref[s, 1]
        co_ref[s, 1] = cb_ref[s, 2]
        co_ref[s, 2] = xr
        y = y * _sigmoid(y)
        ss = jnp.sum(y * y, axis=1, keepdims=True)
        inv = lax.rsqrt(ss + EPS)
        qn = y[0:G_HEADS] * inv[0:G_HEADS] * (G_DK ** -0.5)
        kn = y[G_HEADS:2 * G_HEADS] * inv[G_HEADS:2 * G_HEADS]
        ys.append((qn, kn, y[2 * G_HEADS:3 * G_HEADS]))
    qk = jnp.concatenate([jnp.concatenate([qn, kn], axis=0) for qn, kn, _ in ys]
                         + ([jnp.zeros((LANES - 16 * nb, LANES), F32)] if 16 * nb < LANES else []), axis=0)
    qkt = qk.T
    out_rows = []
    for s in range(nb):
        qn, kn, vv = ys[s]
        head_rows = []
        for h in range(G_HEADS):
            eg = eg8[s:s + 1, GATE_A + h:GATE_A + h + 1]
            bt = beta[s:s + 1, GATE_B + h:GATE_B + h + 1]
            q_row, k_row, v_row = qn[h:h + 1], kn[h:h + 1], vv[h:h + 1]
            q_col = qkt[:, s * 16 + h:s * 16 + h + 1]
            k_col = qkt[:, s * 16 + 8 + h:s * 16 + 9 + h]
            s_old = s_ref[s, h]
            u = bt * (v_row - eg * jnp.sum(k_col * s_old, axis=0, keepdims=True))
            o = eg * jnp.sum(q_col * s_old, axis=0, keepdims=True) \
                + jnp.sum(q_row * k_row, axis=1, keepdims=True) * u
            so_ref[s, h] = eg * s_old + k_col * u
            y = _rms(o, gn_ref[:, h * G_DV:(h + 1) * G_DV])
            gz = pr_ref[s * R + X0 + CR + h:s * R + X0 + CR + h + 1, :]
            head_rows.append(y * (gz * _sigmoid(gz)))
        out_rows.append(jnp.concatenate(head_rows, axis=1))
    og_ref[...] = jnp.concatenate(out_rows, axis=0).astype(BF16)


def _gdn_sample(proj_rows, gates, gate_bias, alog, conv_w_rows, g_norm, s0, conv0_rows):
    b = gates.shape[0]
    nb = SAMPLE_BLOCK
    dv = G_HEADS * G_DV
    cr = G_CONV_CH // LANES
    return pl.pallas_call(
        _gdn_sample_kernel,
        grid=(b // nb,),
        in_specs=[
            pl.BlockSpec((nb * ROWS_PER_TOKEN, LANES), lambda i: (i, 0)),
            pl.BlockSpec((nb, LANES), lambda i: (i, 0)),
            pl.BlockSpec((1, LANES), lambda i: (0, 0)),
            pl.BlockSpec((1, LANES), lambda i: (0, 0)),
            pl.BlockSpec((CONV_W, cr, LANES), lambda i: (0, 0, 0)),
            pl.BlockSpec((1, dv), lambda i: (0, 0)),
            pl.BlockSpec((nb, G_HEADS, G_DK, G_DV), lambda i: (i, 0, 0, 0)),
            pl.BlockSpec((nb, CONV_W - 1, cr, LANES), lambda i: (i, 0, 0, 0)),
        ],
        out_specs=[
            pl.BlockSpec((nb, dv), lambda i: (i, 0)),
            pl.BlockSpec((nb, G_HEADS, G_DK, G_DV), lambda i: (i, 0, 0, 0)),
            pl.BlockSpec((nb, CONV_W - 1, cr, LANES), lambda i: (i, 0, 0, 0)),
        ],
        out_shape=[
            jax.ShapeDtypeStruct((b, dv), BF16),
            jax.ShapeDtypeStruct(s0.shape, F32),
            jax.ShapeDtypeStruct(conv0_rows.shape, F32),
        ],
        compiler_params=_cparams("parallel"),
        name="gdn_sample",
    )(proj_rows, gates, gate_bias, alog, conv_w_rows, g_norm, s0, conv0_rows)


def _split_w_in(w_in):
    n_m = 2 * M_HEADS * M_DK + 2 * M_HEADS * M_DV
    n_g = 4 * G_HEADS * G_DK
    g0 = n_m + 2 * M_HEADS
    big = jnp.concatenate([w_in[..., :n_m], w_in[..., g0:g0 + n_g]], axis=-1)
    gate = jnp.concatenate([w_in[..., n_m:g0], w_in[..., g0 + n_g:]], axis=-1)
    gate = jnp.pad(gate, ((0, 0), (0, 0), (0, LANES - gate.shape[-1])))
    return big.astype(BF16), gate.astype(BF16)


def _lane_row(parts, depth):
    row = jnp.zeros((depth, 1, LANES), F32)
    for off, arr in parts:
        row = row.at[:, 0, off:off + arr.shape[-1]].set(arr.astype(F32))
    return row


def kernel(x_prompt, x_sample, mem_prompt, state_mlstm_C, state_mlstm_n, state_mlstm_m, state_gdn_S,
           state_gdn_conv, cache_mem_k, cache_mem_v, norm_mix, w_in, mlstm_b_i, mlstm_b_f, mlstm_norm,
           gdn_conv_w, gdn_A_log, gdn_dt_bias, gdn_norm, w_out, norm_xattn, norm_mem, xattn_wq, xattn_wk,
           xattn_wv, xattn_wo, norm_ffn, ffn_w_gate, ffn_w_up, ffn_w_down, norm_final):
    bp, seq, _ = x_prompt.shape
    bs = x_sample.shape[0]
    depth = w_in.shape[0]
    dq = X_HEADS * X_DH
    cr = G_CONV_CH // LANES
    tm_p, tm_s = 512, bs

    w_big, w_gate = _split_w_in(w_in)
    w_out_b = w_out.astype(BF16)
    wq_b = xattn_wq.astype(BF16)
    wkv_b = jnp.concatenate([xattn_wk, xattn_wv], axis=-1).astype(BF16)
    wo_b = xattn_wo.astype(BF16)
    wg_b, wu_b, wd_b = ffn_w_gate.astype(BF16), ffn_w_up.astype(BF16), ffn_w_down.astype(BF16)
    gate_bias = _lane_row([(GATE_I, mlstm_b_i), (GATE_F, mlstm_b_f), (GATE_A, gdn_dt_bias)], depth)
    alog = _lane_row([(GATE_A, gdn_A_log)], depth)
    m_norm = mlstm_norm.reshape(depth, 1, M_HEADS * M_DV)
    g_norm = gdn_norm.reshape(depth, 1, G_HEADS * G_DV)
    conv_w_rows = gdn_conv_w.reshape(depth, CONV_W, cr, LANES)
    conv0_rows = state_gdn_conv.reshape(depth, bs, CONV_W - 1, cr, LANES)
    mem_k = cache_mem_k.reshape(depth, bs, N_MEM, dq)
    mem_v = cache_mem_v.reshape(depth, bs, N_MEM, dq)
    mem2 = mem_prompt.reshape(bp * N_MEM, D_MODEL)

    def vec(a, l):
        return a[l].reshape(1, -1)

    xp = x_prompt.reshape(bp * seq, D_MODEL)
    xs = x_sample.reshape(bs, D_MODEL)
    outs = [[] for _ in range(12)]
    for l in range(depth):
        kv = _norm_mm(mem2, vec(norm_mem, l), wkv_b[l], N_MEM)
        mk_p = kv[:, :dq].reshape(bp, N_MEM, dq)
        mv_p = kv[:, dq:].reshape(bp, N_MEM, dq)
        proj, gates = _proj(xp, vec(norm_mix, l), w_big[l], w_gate[l], tm_p)
        proj3 = proj.reshape(bp, seq, N_BIG)
        gates3 = gates.reshape(bp, seq, LANES)
        hm, c_p, n_p, m_p = _mlstm_prompt(proj3, gates3, gate_bias[l], m_norm[l])
        og, s_p, cv_p = _gdn_prompt(proj3, gates3, gate_bias[l], alog[l], gdn_conv_w[l], g_norm[l])
        xp = _mix_out(xp, hm.reshape(bp * seq, -1), og.reshape(bp * seq, -1), w_out_b[l], tm_p)
        xp = _xattn_prompt(xp, vec(norm_xattn, l), wq_b[l], mk_p, mv_p, wo_b[l], tm_p, seq)
        xp = _ffn(xp, vec(norm_ffn, l), wg_b[l], wu_b[l], wd_b[l], tm_p)
        for lst, val in zip(outs[:7], (c_p, n_p.reshape(bp, M_HEADS, M_DK), m_p[:, 0, :M_HEADS], s_p, cv_p,
                                       mk_p.reshape(bp, N_MEM, X_HEADS, X_DH),
                                       mv_p.reshape(bp, N_MEM, X_HEADS, X_DH))):
            lst.append(val)

        proj, gates = _proj(xs, vec(norm_mix, l), w_big[l], w_gate[l], tm_s)
        proj_rows = proj.reshape(bs * ROWS_PER_TOKEN, LANES)
        hm, c_s, n_s, m_s = _mlstm_sample(proj_rows, gates, gate_bias[l], m_norm[l],
                                          state_mlstm_C[l], state_mlstm_n[l], state_mlstm_m[l])
        og, s_s, cv_s = _gdn_sample(proj_rows, gates, gate_bias[l], alog[l], conv_w_rows[l], g_norm[l],
                                    state_gdn_S[l], conv0_rows[l])
        xs = _mix_out(xs, hm, og, w_out_b[l], tm_s)
        xs = _xattn_sample(xs, vec(norm_xattn, l), wq_b[l], mem_k[l], mem_v[l], wo_b[l])
        xs = _ffn(xs, vec(norm_ffn, l), wg_b[l], wu_b[l], wd_b[l], tm_s)
        for lst, val in zip(outs[7:], (c_s, n_s, m_s, s_s, cv_s.reshape(bs, CONV_W - 1, G_CONV_CH))):
            lst.append(val)

    y_prompt = _final_norm(xp, norm_final.reshape(1, -1), tm_p).reshape(bp, seq, D_MODEL)
    y_sample = _final_norm(xs, norm_final.reshape(1, -1), tm_s).reshape(bs, 1, D_MODEL)
    return (y_prompt, y_sample) + tuple(jnp.stack(o) for o in outs)
```

```python
import math

import jax
import jax.numpy as jnp
from jax import lax
from jax.experimental import pallas as pl
from jax.experimental.pallas import tpu as pltpu

F32 = jnp.float32
BF16 = jnp.bfloat16

D_MODEL = 2048
M_HEADS = 4
M_DK = 128
M_DV = 256
G_HEADS = 8
G_DK = 128
G_DV = 128
CONV_W = 4
G_CONV_CH = G_HEADS * (2 * G_DK + G_DV)
X_HEADS = 4
X_DH = 128
N_MEM = 256
D_FF = 5632
GATE_CAP = 15.0
EPS = 1e-6

LANES = 128
SUBLANES = 8
N_WIDE = 2 * M_HEADS * M_DK + 2 * M_HEADS * M_DV + 4 * G_HEADS * G_DK
PROJ_TN = 1280
N_PROJ = -(-(N_WIDE + LANES) // PROJ_TN) * PROJ_TN
ROWS_PER_TOKEN = N_PROJ // LANES
GATE_ROW = N_WIDE // LANES
GATE_I, GATE_F, GATE_A, GATE_B = 0, 4, 8, 16
ROW_MQ, ROW_MK, ROW_MV, ROW_MO = 0, 4, 8, 16
ROW_GX, ROW_GZ = 24, 48
CONV_ROWS = G_CONV_CH // LANES

VMEM_LIMIT = 52 * 1024 * 1024

M_CHUNK = 256
G_BLOCK = 256
G_CHUNK = 64
G_PAIR = 2 * G_CHUNK
SAMPLE_BLOCK = 8

NT_DIMS = (((1,), (1,)), ((), ()))
TN_DIMS = (((0,), (0,)), ((), ()))


def _cparams(*sem):
    return pltpu.CompilerParams(dimension_semantics=sem, vmem_limit_bytes=VMEM_LIMIT)


def _dot(a, b):
    return jnp.dot(a, b, preferred_element_type=F32)


def _dot_nt(a, b):
    return lax.dot_general(a, b, NT_DIMS, preferred_element_type=F32)


def _dot_tn(a, b):
    return lax.dot_general(a, b, TN_DIMS, preferred_element_type=F32)


def _dot_f32(a, b):
    return jnp.dot(a, b, preferred_element_type=F32, precision=lax.Precision.HIGHEST)


def _rms(x, g):
    ms = jnp.mean(x * x, axis=-1, keepdims=True)
    return x * lax.rsqrt(ms + EPS) * g


def _sigmoid(x):
    return 1.0 / (1.0 + jnp.exp(-x))


def _softplus(x):
    return jnp.maximum(x, 0.0) + jnp.log1p(jnp.exp(-jnp.abs(x)))


def _softcap(x):
    return GATE_CAP * jnp.tanh(x / GATE_CAP)


def _layer_spec(shape, l, tail):
    return pl.BlockSpec((None,) + shape, lambda *g: (l,) + tail(*g))


def _proj_kernel(x_ref, g_ref, w_ref, o_ref, h_scr):
    @pl.when(pl.program_id(1) == 0)
    def _():
        h_scr[...] = _rms(x_ref[...], g_ref[...]).astype(BF16)

    o_ref[...] = _dot(h_scr[...], w_ref[...])


def _proj(x, g, w, l, tm):
    m = x.shape[0]
    return pl.pallas_call(
        _proj_kernel,
        grid=(m // tm, N_PROJ // PROJ_TN),
        in_specs=[
            pl.BlockSpec((tm, D_MODEL), lambda i, j: (i, 0)),
            _layer_spec((1, D_MODEL), l, lambda i, j: (0, 0)),
            _layer_spec((D_MODEL, PROJ_TN), l, lambda i, j: (0, j)),
        ],
        out_specs=pl.BlockSpec((tm, PROJ_TN), lambda i, j: (i, j)),
        out_shape=jax.ShapeDtypeStruct((m, N_PROJ), F32),
        scratch_shapes=[pltpu.VMEM((tm, D_MODEL), BF16)],
        compiler_params=_cparams("parallel", "arbitrary"),
        name="proj",
    )(x, g, w)


def _mem_kv_kernel(x_ref, g_ref, wk_ref, wv_ref, k_ref, v_ref):
    h = _rms(x_ref[...], g_ref[...]).astype(BF16)
    k_ref[...] = _dot(h, wk_ref[...])
    v_ref[...] = _dot(h, wv_ref[...])


def _mem_kv(mem, g, wk, wv):
    depth = wk.shape[0]
    m = mem.shape[0]
    dq = X_HEADS * X_DH
    tm = N_MEM
    w_spec = pl.BlockSpec((None, D_MODEL, dq), lambda l, i: (l, 0, 0))
    o_spec = pl.BlockSpec((None, tm, dq), lambda l, i: (l, i, 0))
    return pl.pallas_call(
        _mem_kv_kernel,
        grid=(depth, m // tm),
        in_specs=[
            pl.BlockSpec((tm, D_MODEL), lambda l, i: (i, 0)),
            pl.BlockSpec((None, 1, D_MODEL), lambda l, i: (l, 0, 0)),
            w_spec, w_spec,
        ],
        out_specs=[o_spec, o_spec],
        out_shape=[jax.ShapeDtypeStruct((depth, m, dq), F32)] * 2,
        compiler_params=_cparams("parallel", "parallel"),
        name="mem_kv",
    )(mem, g, wk, wv)


def _mix_out_kernel(x_ref, hm_ref, og_ref, w1_ref, w2_ref, o_ref):
    acc = _dot(hm_ref[...], w1_ref[...])
    acc = acc + _dot(og_ref[...], w2_ref[...])
    o_ref[...] = x_ref[...] + acc


def _mix_out(x, hm, og, w_out, l, tm):
    m = x.shape[0]
    half = D_MODEL // 2
    return pl.pallas_call(
        _mix_out_kernel,
        grid=(m // tm,),
        in_specs=[
            pl.BlockSpec((tm, D_MODEL), lambda i: (i, 0)),
            pl.BlockSpec((tm, half), lambda i: (i, 0)),
            pl.BlockSpec((tm, half), lambda i: (i, 0)),
            _layer_spec((half, D_MODEL), l, lambda i: (0, 0)),
            _layer_spec((half, D_MODEL), l, lambda i: (1, 0)),
        ],
        out_specs=pl.BlockSpec((tm, D_MODEL), lambda i: (i, 0)),
        out_shape=jax.ShapeDtypeStruct((m, D_MODEL), F32),
        compiler_params=_cparams("parallel"),
        name="mix_out",
    )(x, hm, og, w_out, w_out)


def _ffn_kernel(x_ref, g_ref, wg_ref, wu_ref, wd_ref, o_ref, h_scr, acc_scr):
    j = pl.program_id(1)

    @pl.when(j == 0)
    def _():
        h_scr[...] = _rms(x_ref[...], g_ref[...]).astype(BF16)
        acc_scr[...] = jnp.zeros_like(acc_scr)

    h = h_scr[...]
    a = _dot(h, wg_ref[...])
    b = _dot(h, wu_ref[...])
    t = (a * _sigmoid(a)) * b
    acc_scr[...] += _dot(t.astype(BF16), wd_ref[...])

    @pl.when(j == pl.num_programs(1) - 1)
    def _():
        o_ref[...] = x_ref[...] + acc_scr[...]


def _ffn(x, g, w_gate, w_up, w_down, l, tm, tf=512):
    m = x.shape[0]
    return pl.pallas_call(
        _ffn_kernel,
        grid=(m // tm, D_FF // tf),
        in_specs=[
            pl.BlockSpec((tm, D_MODEL), lambda i, j: (i, 0)),
            _layer_spec((1, D_MODEL), l, lambda i, j: (0, 0)),
            _layer_spec((D_MODEL, tf), l, lambda i, j: (0, j)),
            _layer_spec((D_MODEL, tf), l, lambda i, j: (0, j)),
            _layer_spec((tf, D_MODEL), l, lambda i, j: (j, 0)),
        ],
        out_specs=pl.BlockSpec((tm, D_MODEL), lambda i, j: (i, 0)),
        out_shape=jax.ShapeDtypeStruct((m, D_MODEL), F32),
        scratch_shapes=[pltpu.VMEM((tm, D_MODEL), BF16), pltpu.VMEM((tm, D_MODEL), F32)],
        compiler_params=_cparams("parallel", "arbitrary"),
        name="ffn",
    )(x, g, w_gate, w_up, w_down)


def _final_norm_kernel(x_ref, g_ref, o_ref):
    o_ref[...] = _rms(x_ref[...], g_ref[...])


def _final_norm(x, g, tm):
    m = x.shape[0]
    return pl.pallas_call(
        _final_norm_kernel,
        grid=(m // tm,),
        in_specs=[pl.BlockSpec((tm, D_MODEL), lambda i: (i, 0)), pl.BlockSpec((1, D_MODEL), lambda i: (0, 0))],
        out_specs=pl.BlockSpec((tm, D_MODEL), lambda i: (i, 0)),
        out_shape=jax.ShapeDtypeStruct((m, D_MODEL), F32),
        compiler_params=_cparams("parallel"),
        name="final_norm",
    )(x, g)


def _softmax_rows(s):
    s = s - jnp.max(s, axis=-1, keepdims=True)
    p = jnp.exp(s)
    return p / jnp.sum(p, axis=-1, keepdims=True)


def _xattn_prompt_kernel(x_ref, g_ref, wq_ref, k_ref, v_ref, wo_ref, o_ref):
    x = x_ref[...]
    q = _dot(_rms(x, g_ref[...]).astype(BF16), wq_ref[...])
    heads = []
    for h in range(X_HEADS):
        sl = slice(h * X_DH, (h + 1) * X_DH)
        s = _dot_nt(q[:, sl].astype(BF16), k_ref[:, sl].astype(BF16)) * (X_DH ** -0.5)
        p = _softmax_rows(s)
        heads.append(_dot(p.astype(BF16), v_ref[:, sl].astype(BF16)).astype(BF16))
    o = jnp.concatenate(heads, axis=1)
    o_ref[...] = x + _dot(o, wo_ref[...])


def _xattn_prompt(x, g, wq, mk, mv, wo, l, tm, seq):
    m = x.shape[0]
    dq = X_HEADS * X_DH
    per_seq = seq // tm
    kv_spec = pl.BlockSpec((None, N_MEM, dq), lambda i: (l, i // per_seq, 0))
    return pl.pallas_call(
        _xattn_prompt_kernel,
        grid=(m // tm,),
        in_specs=[
            pl.BlockSpec((tm, D_MODEL), lambda i: (i, 0)),
            _layer_spec((1, D_MODEL), l, lambda i: (0, 0)),
            _layer_spec((D_MODEL, dq), l, lambda i: (0, 0)),
            kv_spec, kv_spec,
            _layer_spec((dq, D_MODEL), l, lambda i: (0, 0)),
        ],
        out_specs=pl.BlockSpec((tm, D_MODEL), lambda i: (i, 0)),
        out_shape=jax.ShapeDtypeStruct((m, D_MODEL), F32),
        compiler_params=_cparams("parallel"),
        name="xattn_prompt",
    )(x, g, wq, mk, mv, wo)


def _xattn_sample_kernel(x_ref, g_ref, wq_ref, k_ref, v_ref, wo_ref, o_ref):
    nb = x_ref.shape[0]
    nk = N_MEM * X_HEADS
    x = x_ref[...]
    q = _dot(_rms(x, g_ref[...]).astype(BF16), wq_ref[...])
    qh = [q[:, h * X_DH:(h + 1) * X_DH] for h in range(X_HEADS)]
    row = lax.broadcasted_iota(jnp.int32, (SUBLANES, nk), 0)
    col = lax.broadcasted_iota(jnp.int32, (SUBLANES, nk), 1)
    own = (col & (X_HEADS - 1)) == (row & (X_HEADS - 1))
    row8 = lax.broadcasted_iota(jnp.int32, (SUBLANES, X_DH), 0)
    outs = [jnp.zeros((nb, X_DH), F32) for _ in range(X_HEADS)]
    scores = []
    for s in range(nb):
        q8 = jnp.zeros((SUBLANES, X_DH), F32)
        for h in range(X_HEADS):
            q8 = jnp.where(row8 == h, qh[h][s:s + 1, :], q8)
        scores.append(_dot_nt(q8.astype(BF16), k_ref[s].astype(BF16)) * (X_DH ** -0.5))
    probs = [_softmax_rows(jnp.where(own, sc, -jnp.inf)).astype(BF16) for sc in scores]
    o8s = [_dot(probs[s], v_ref[s].astype(BF16)) for s in range(nb)]
    for s in range(nb):
        for h in range(X_HEADS):
            outs[h] = jnp.where(row8[:nb] == s, o8s[s][h:h + 1, :], outs[h])
    o = jnp.concatenate(outs, axis=1).astype(BF16)
    o_ref[...] = x + _dot(o, wo_ref[...])


def _xattn_sample(x, g, wq, mk, mv, wo, l):
    m = x.shape[0]
    nb = SAMPLE_BLOCK
    dq = X_HEADS * X_DH
    kv_spec = _layer_spec((nb, N_MEM * X_HEADS, X_DH), l, lambda i: (i, 0, 0))
    return pl.pallas_call(
        _xattn_sample_kernel,
        grid=(m // nb,),
        in_specs=[
            pl.BlockSpec((nb, D_MODEL), lambda i: (i, 0)),
            _layer_spec((1, D_MODEL), l, lambda i: (0, 0)),
            _layer_spec((D_MODEL, dq), l, lambda i: (0, 0)),
            kv_spec, kv_spec,
            _layer_spec((dq, D_MODEL), l, lambda i: (0, 0)),
        ],
        out_specs=pl.BlockSpec((nb, D_MODEL), lambda i: (i, 0)),
        out_shape=jax.ShapeDtypeStruct((m, D_MODEL), F32),
        compiler_params=_cparams("parallel"),
        name="xattn_sample",
    )(x, g, wq, mk, mv, wo)


def _mlstm_prompt_kernel(q_ref, k_ref, v_ref, mo_ref, gt_ref, gb_ref, mn_ref,
                         hm_ref, c_ref, n_ref, m_ref, c_scr, n_scr, m_scr):
    L = q_ref.shape[0]
    c = pl.program_id(1)

    @pl.when(c == 0)
    def _():
        c_scr[...] = jnp.zeros_like(c_scr)
        n_scr[...] = jnp.zeros_like(n_scr)
        m_scr[...] = jnp.zeros_like(m_scr)

    gates = gt_ref[...] + gb_ref[...]
    capped = _softcap(gates)
    logf = -_softplus(-capped)
    row = lax.broadcasted_iota(jnp.int32, (L, L), 0)
    col = lax.broadcasted_iota(jnp.int32, (L, L), 1)
    causal = row >= col
    bcum = _dot_f32(causal.astype(F32), logf)
    lane = lax.broadcasted_iota(jnp.int32, (L, LANES), 1)
    z = jnp.where(lane < GATE_F, capped, bcum)
    zt = z.T

    for h in range(M_HEADS):
        i_col = z[:, GATE_I + h:GATE_I + h + 1]
        b_col = z[:, GATE_F + h:GATE_F + h + 1]
        i_row = zt[GATE_I + h:GATE_I + h + 1, :]
        b_row = zt[GATE_F + h:GATE_F + h + 1, :]
        m_prev = m_scr[0:1, h:h + 1]
        d = jnp.where(causal, b_col + (i_row - b_row), -jnp.inf)
        inter = b_col + m_prev
        m_t = jnp.maximum(inter, jnp.max(d, axis=1, keepdims=True))
        q = q_ref[:, h * M_DK:(h + 1) * M_DK]
        k = k_ref[:, h * M_DK:(h + 1) * M_DK] * (M_DK ** -0.5)
        vb = v_ref[:, h * M_DV:(h + 1) * M_DV].astype(BF16)
        qb = q.astype(BF16)
        w = jnp.exp(d - m_t) * _dot_nt(qb, k.astype(BF16))
        a = jnp.exp(inter - m_t)
        c_old = c_scr[h]
        n_old = n_scr[h]
        num = a * _dot(qb, c_old.astype(BF16)) + _dot(w.astype(BF16), vb)
        nq = a * jnp.sum(q * n_old, axis=1, keepdims=True) + jnp.sum(w, axis=1, keepdims=True)
        hh = num / jnp.maximum(jnp.abs(nq), jnp.exp(-m_t))
        y = _rms(hh, mn_ref[:, h * M_DV:(h + 1) * M_DV])
        y = y * _sigmoid(mo_ref[:, h * M_DV:(h + 1) * M_DV])
        hm_ref[:, h * M_DV:(h + 1) * M_DV] = y.astype(BF16)

        m_new = m_t[L - 1:L, :]
        b_last = b_col[L - 1:L, :]
        decay = jnp.exp(b_last + m_prev - m_new)
        kw = k * jnp.exp(b_last - b_col + i_col - m_new)
        c_scr[h] = decay * c_old + _dot_tn(kw.astype(BF16), vb)
        n_scr[h] = decay * n_old + jnp.sum(kw, axis=0, keepdims=True)
        m_scr[0:1, h:h + 1] = m_new

    @pl.when(c == pl.num_programs(1) - 1)
    def _():
        c_ref[...] = c_scr[...]
        n_ref[...] = n_scr[...]
        m_ref[...] = m_scr[...]


def _mlstm_prompt(proj3, gate_bias, m_norm, l):
    b, t, _ = proj3.shape
    L = M_CHUNK
    dqk = M_HEADS * M_DK
    dv = M_HEADS * M_DV
    return pl.pallas_call(
        _mlstm_prompt_kernel,
        grid=(b, t // L),
        in_specs=[
            pl.BlockSpec((None, L, dqk), lambda i, c: (i, c, 0)),
            pl.BlockSpec((None, L, dqk), lambda i, c: (i, c, 1)),
            pl.BlockSpec((None, L, dv), lambda i, c: (i, c, 1)),
            pl.BlockSpec((None, L, dv), lambda i, c: (i, c, 2)),
            pl.BlockSpec((None, L, LANES), lambda i, c: (i, c, GATE_ROW)),
            _layer_spec((1, LANES), l, lambda i, c: (0, 0)),
            _layer_spec((1, dv), l, lambda i, c: (0, 0)),
        ],
        out_specs=[
            pl.BlockSpec((None, L, dv), lambda i, c: (i, c, 0)),
            pl.BlockSpec((None, M_HEADS, M_DK, M_DV), lambda i, c: (i, 0, 0, 0)),
            pl.BlockSpec((None, M_HEADS, 1, M_DK), lambda i, c: (i, 0, 0, 0)),
            pl.BlockSpec((None, 1, LANES), lambda i, c: (i, 0, 0)),
        ],
        out_shape=[
            jax.ShapeDtypeStruct((b, t, dv), BF16),
            jax.ShapeDtypeStruct((b, M_HEADS, M_DK, M_DV), F32),
            jax.ShapeDtypeStruct((b, M_HEADS, 1, M_DK), F32),
            jax.ShapeDtypeStruct((b, 1, LANES), F32),
        ],
        scratch_shapes=[
            pltpu.VMEM((M_HEADS, M_DK, M_DV), F32),
            pltpu.VMEM((M_HEADS, 1, M_DK), F32),
            pltpu.VMEM((1, LANES), F32),
        ],
        compiler_params=_cparams("parallel", "arbitrary"),
        name="mlstm_prompt",
    )(proj3, proj3, proj3, proj3, proj3, gate_bias, m_norm)


def _token_rows(pr_ref, nb, off):
    return pr_ref[pl.ds(off, nb, stride=ROWS_PER_TOKEN), :]


def _token_rows2(pr_ref, nb, off):
    return jnp.concatenate([_token_rows(pr_ref, nb, off), _token_rows(pr_ref, nb, off + 1)], axis=1)


def _pick_row(tiles):
    rid = lax.broadcasted_iota(jnp.int32, tiles[0].shape, 0)
    out = tiles[0]
    for s in range(1, len(tiles)):
        out = jnp.where(rid == s, tiles[s], out)
    return out


def _columns(tiles):
    pad = LANES - SUBLANES * len(tiles)
    stack = jnp.concatenate(tiles + ([jnp.zeros((pad, LANES), F32)] if pad else []), axis=0)
    return stack.T


def _mlstm_sample_kernel(pr_ref, gb_ref, mn_ref, c_ref, n_ref, m_ref, acc_ref,
                         hm_ref, co_ref, no_ref, mo_ref):
    del acc_ref
    nb = m_ref.shape[0]
    gates = _token_rows(pr_ref, nb, GATE_ROW) + gb_ref[...]
    capped = _softcap(gates)
    logf = -_softplus(-capped)
    i4 = capped[:, GATE_I:GATE_I + M_HEADS]
    f4 = logf[:, GATE_F:GATE_F + M_HEADS]
    inter = f4 + m_ref[...]
    m_new = jnp.maximum(inter, i4)
    a4 = jnp.exp(inter - m_new)
    w4 = jnp.exp(i4 - m_new)
    e4 = jnp.exp(-m_new)
    mo_ref[...] = m_new

    scale = M_DK ** -0.5
    ys, n_rows = [], []
    for h in range(M_HEADS):
        a, wi, e = a4[:, h:h + 1], w4[:, h:h + 1], e4[:, h:h + 1]
        q = _token_rows(pr_ref, nb, ROW_MQ + h)
        k = _token_rows(pr_ref, nb, ROW_MK + h) * scale
        v = _token_rows2(pr_ref, nb, ROW_MV + 2 * h)
        og = _token_rows2(pr_ref, nb, ROW_MO + 2 * h)
        n_old = n_ref[:, h * M_DK:(h + 1) * M_DK]
        wk = wi * k
        n_rows.append(a * n_old + wk)
        wts = wi * jnp.sum(q * k, axis=1, keepdims=True)
        nq = a * jnp.sum(q * n_old, axis=1, keepdims=True) + wts
        wk_cols = _columns([wk])
        a_rows = jnp.broadcast_to(a, (nb, M_DV))
        qb = q.astype(BF16)
        qc = []
        for s in range(nb):
            c_old = c_ref[s, h]
            qc.append(_dot(qb, c_old.astype(BF16)))
            co_ref[s, h] = a_rows[s:s + 1, :] * c_old + wk_cols[:, s:s + 1] * v[s:s + 1, :]
        num = a * _pick_row(qc) + wts * v
        hh = num / jnp.maximum(jnp.abs(nq), e)
        ys.append(_rms(hh, mn_ref[:, h * M_DV:(h + 1) * M_DV]) * _sigmoid(og))
    hm_ref[...] = jnp.concatenate(ys, axis=1).astype(BF16)
    no_ref[...] = jnp.concatenate(n_rows, axis=1)


def _mlstm_sample(proj_rows, gate_bias, m_norm, c_all, n_all, m_all, c_acc, l):
    depth, b = m_all.shape[:2]
    nb = SAMPLE_BLOCK
    dv = M_HEADS * M_DV
    dk = M_HEADS * M_DK
    c_spec = _layer_spec((nb, M_HEADS, M_DK, M_DV), l, lambda i: (i, 0, 0, 0))
    return pl.pallas_call(
        _mlstm_sample_kernel,
        grid=(b // nb,),
        in_specs=[
            pl.BlockSpec((nb * ROWS_PER_TOKEN, LANES), lambda i: (i, 0)),
            _layer_spec((1, LANES), l, lambda i: (0, 0)),
            _layer_spec((1, dv), l, lambda i: (0, 0)),
            c_spec,
            _layer_spec((nb, dk), l, lambda i: (i, 0)),
            _layer_spec((nb, M_HEADS), l, lambda i: (i, 0)),
            pl.BlockSpec(memory_space=pl.ANY),
        ],
        out_specs=[
            pl.BlockSpec((nb, dv), lambda i: (i, 0)),
            c_spec,
            pl.BlockSpec((nb, dk), lambda i: (i, 0)),
            pl.BlockSpec((nb, M_HEADS), lambda i: (i, 0)),
        ],
        out_shape=[
            jax.ShapeDtypeStruct((b, dv), BF16),
            jax.ShapeDtypeStruct(c_all.shape, F32),
            jax.ShapeDtypeStruct((b, dk), F32),
            jax.ShapeDtypeStruct((b, M_HEADS), F32),
        ],
        input_output_aliases={6: 1},
        compiler_params=_cparams("parallel"),
        name="mlstm_sample",
    )(proj_rows, gate_bias, m_norm, c_all, n_all, m_all, c_acc)


def _gdn_gates(gates, alog):
    log_a = -jnp.exp(alog) * _softplus(gates)
    beta = _sigmoid(gates)
    return log_a, beta


def _gdn_prompt_kernel(x_ref, gz_ref, gt_ref, gb_ref, al_ref, cw_ref, gn_ref,
                       og_ref, s_ref, cv_ref, s_scr, tail_scr):
    Lb = x_ref.shape[0]
    L, P = G_CHUNK, G_PAIR
    n_pair = Lb // P
    c = pl.program_id(1)

    @pl.when(c == 0)
    def _():
        s_scr[...] = jnp.zeros_like(s_scr)
        tail_scr[...] = jnp.zeros_like(tail_scr)

    log_a, beta = _gdn_gates(gt_ref[...] + gb_ref[...], al_ref[...])
    row = lax.broadcasted_iota(jnp.int32, (Lb, Lb), 0)
    col = lax.broadcasted_iota(jnp.int32, (Lb, Lb), 1)
    shift = int(math.log2(L))
    same_chunk = jnp.right_shift(row, shift) == jnp.right_shift(col, shift)
    gcum = _dot_f32(jnp.where(same_chunk & (row >= col), 1.0, 0.0), log_a)
    gtot = _dot_f32(jnp.where(same_chunk, 1.0, 0.0), log_a)
    lane = lax.broadcasted_iota(jnp.int32, (Lb, LANES), 1)
    zt = jnp.where(lane < GATE_B, gcum, beta).T

    rp = lax.broadcasted_iota(jnp.int32, (P, P), 0)
    cp = lax.broadcasted_iota(jnp.int32, (P, P), 1)
    same = jnp.right_shift(rp, shift) == jnp.right_shift(cp, shift)
    incl = same & (rp >= cp)
    strict = same & (rp > cp)

    def conv_cols(c0):
        xc = x_ref[:, c0:c0 + LANES]
        xp = jnp.concatenate([tail_scr[:, c0:c0 + LANES], xc], axis=0)
        y = xc * cw_ref[3:4, c0:c0 + LANES]
        for j in range(CONV_W - 1):
            y = y + xp[5 + j:5 + j + Lb, :] * cw_ref[j:j + 1, c0:c0 + LANES]
        return y * _sigmoid(y)

    heads = []
    for h in range(G_HEADS):
        qc = conv_cols(h * G_DK)
        kc = conv_cols(G_HEADS * G_DK + h * G_DK)
        vc = conv_cols(2 * G_HEADS * G_DK + h * G_DV)
        q2 = qc * lax.rsqrt(jnp.sum(qc * qc, axis=1, keepdims=True) + EPS) * (G_DK ** -0.5)
        k2 = kc * lax.rsqrt(jnp.sum(kc * kc, axis=1, keepdims=True) + EPS)
        g_col = gcum[:, GATE_A + h:GATE_A + h + 1]
        t_col = gtot[:, GATE_A + h:GATE_A + h + 1]
        b_col = beta[:, GATE_B + h:GATE_B + h + 1]
        eg = jnp.exp(g_col)
        heads.append(dict(
            qb=q2.astype(BF16), kb=k2.astype(BF16), eg=eg, b_col=b_col,
            vek=jnp.concatenate([vc, eg * k2], axis=1),
            kd=(k2 * jnp.exp(t_col - g_col)).astype(BF16),
            dS=jnp.exp(t_col),
        ))

    blocks = []
    for h in range(G_HEADS):
        hd = heads[h]
        for b in range(n_pair):
            r0 = b * P
            kb = hd["kb"][r0:r0 + P]
            g_col = gcum[r0:r0 + P, GATE_A + h:GATE_A + h + 1]
            g_row = zt[GATE_A + h:GATE_A + h + 1, r0:r0 + P]
            b_row = zt[GATE_B + h:GATE_B + h + 1, r0:r0 + P]
            dec = jnp.exp(jnp.where(incl, g_col - g_row, -jnp.inf))
            a_mat = jnp.where(strict, dec * _dot_nt(kb, kb) * b_row, 0.0)
            qkd = (_dot_nt(hd["qb"][r0:r0 + P], kb) * dec).astype(BF16)
            blocks.append(dict(h=h, r0=r0, mpow=-a_mat, qacc=-a_mat, qkd=qkd))

    for _ in range(int(math.log2(L)) - 1):
        for blk in blocks:
            mb = blk["mpow"].astype(BF16)
            blk["mpow"] = _dot(mb, mb)
        for blk in blocks:
            blk["qacc"] = blk["qacc"] + blk["mpow"] + _dot(blk["qacc"].astype(BF16), blk["mpow"].astype(BF16))

    for blk in blocks:
        hd = heads[blk["h"]]
        r0 = blk["r0"]
        vek = hd["vek"][r0:r0 + P]
        t = (vek + _dot(blk["qacc"].astype(BF16), vek.astype(BF16))) * hd["b_col"][r0:r0 + P]
        blk["uv"] = t[:, :G_DV]
        blk["kw"] = t[:, G_DV:].astype(BF16)

    states = [s_scr[h] for h in range(G_HEADS)]
    wus = [[] for _ in range(G_HEADS)]
    qss = [[] for _ in range(G_HEADS)]
    for ci in range(Lb // L):
        r0 = ci * L
        for h in range(G_HEADS):
            hd = heads[h]
            blk = blocks[h * n_pair + r0 // P]
            o0 = r0 - blk["r0"]
            lhs = jnp.concatenate([blk["kw"][o0:o0 + L], hd["qb"][r0:r0 + L]], axis=0)
            r = _dot(lhs, states[h].astype(BF16))
            wu = (blk["uv"][o0:o0 + L] - r[:L]).astype(BF16)
            qss[h].append(r[L:])
            wus[h].append(wu)
            states[h] = hd["dS"][r0:r0 + 1] * states[h] + _dot_tn(hd["kd"][r0:r0 + L], wu)
    for h in range(G_HEADS):
        s_scr[h] = states[h]

    for h in range(G_HEADS):
        hd = heads[h]
        for b in range(n_pair):
            r0 = b * P
            blk = blocks[h * n_pair + b]
            i0 = r0 // L
            wu = jnp.concatenate(wus[h][i0:i0 + P // L], axis=0)
            qs = jnp.concatenate(qss[h][i0:i0 + P // L], axis=0)
            o = hd["eg"][r0:r0 + P] * qs + _dot(blk["qkd"], wu)
            y = _rms(o, gn_ref[:, h * G_DV:(h + 1) * G_DV])
            gz = gz_ref[r0:r0 + P, h * G_DV:(h + 1) * G_DV]
            og_ref[r0:r0 + P, h * G_DV:(h + 1) * G_DV] = (y * (gz * _sigmoid(gz))).astype(BF16)

    tail_scr[...] = x_ref[Lb - SUBLANES:Lb, :]

    @pl.when(c == pl.num_programs(1) - 1)
    def _():
        s_ref[...] = s_scr[...]
        cv_ref[...] = x_ref[Lb - (CONV_W - 1):Lb, :]


def _gdn_prompt(proj3, gate_bias, alog, conv_w, g_norm, l):
    b, t, _ = proj3.shape
    Lb = G_BLOCK
    dv = G_HEADS * G_DV
    return pl.pallas_call(
        _gdn_prompt_kernel,
        grid=(b, t // Lb),
        in_specs=[
            pl.BlockSpec((None, Lb, G_CONV_CH), lambda i, c: (i, c, 1)),
            pl.BlockSpec((None, Lb, dv), lambda i, c: (i, c, 6)),
            pl.BlockSpec((None, Lb, LANES), lambda i, c: (i, c, GATE_ROW)),
            _layer_spec((1, LANES), l, lambda i, c: (0, 0)),
            _layer_spec((1, LANES), l, lambda i, c: (0, 0)),
            _layer_spec((CONV_W, G_CONV_CH), l, lambda i, c: (0, 0)),
            _layer_spec((1, dv), l, lambda i, c: (0, 0)),
        ],
        out_specs=[
            pl.BlockSpec((None, Lb, dv), lambda i, c: (i, c, 0)),
            pl.BlockSpec((None, G_HEADS, G_DK, G_DV), lambda i, c: (i, 0, 0, 0)),
            pl.BlockSpec((None, CONV_W - 1, G_CONV_CH), lambda i, c: (i, 0, 0)),
        ],
        out_shape=[
            jax.ShapeDtypeStruct((b, t, dv), BF16),
            jax.ShapeDtypeStruct((b, G_HEADS, G_DK, G_DV), F32),
            jax.ShapeDtypeStruct((b, CONV_W - 1, G_CONV_CH), F32),
        ],
        scratch_shapes=[
            pltpu.VMEM((G_HEADS, G_DK, G_DV), F32),
            pltpu.VMEM((SUBLANES, G_CONV_CH), F32),
        ],
        compiler_params=_cparams("parallel", "arbitrary"),
        name="gdn_prompt",
    )(proj3, proj3, proj3, gate_bias, alog, conv_w, g_norm)


def _gdn_sample_kernel(pr_ref, gb_ref, al_ref, cw_ref, gn_ref, s_ref, cb_ref, acc_ref,
                       og_ref, so_ref, co_ref):
    del acc_ref
    nb = cb_ref.shape[2]
    log_a, beta = _gdn_gates(_token_rows(pr_ref, nb, GATE_ROW) + gb_ref[...], al_ref[...])
    eg8 = jnp.exp(log_a)

    ys = []
    for r in range(CONV_ROWS):
        xr = _token_rows(pr_ref, nb, ROW_GX + r)
        y = xr * cw_ref[CONV_W - 1, r:r + 1, :]
        for j in range(CONV_W - 1):
            y = y + cb_ref[j, r] * cw_ref[j, r:r + 1, :]
        co_ref[0, r] = cb_ref[1, r]
        co_ref[1, r] = cb_ref[2, r]
        co_ref[2, r] = xr
        ys.append(y * _sigmoid(y))

    outs = []
    for h in range(G_HEADS):
        qc, kc, v = ys[h], ys[G_HEADS + h], ys[2 * G_HEADS + h]
        q = qc * lax.rsqrt(jnp.sum(qc * qc, axis=1, keepdims=True) + EPS) * (G_DK ** -0.5)
        k = kc * lax.rsqrt(jnp.sum(kc * kc, axis=1, keepdims=True) + EPS)
        eg = eg8[:, GATE_A + h:GATE_A + h + 1]
        bt = beta[:, GATE_B + h:GATE_B + h + 1]
        qb, kb = q.astype(BF16), k.astype(BF16)
        ks, qs = [], []
        for s in range(nb):
            sb = s_ref[s, h].astype(BF16)
            ks.append(_dot(kb, sb))
            qs.append(_dot(qb, sb))
        u = bt * (v - eg * _pick_row(ks))
        o = eg * _pick_row(qs) + jnp.sum(q * k, axis=1, keepdims=True) * u
        k_cols = _columns([k])
        eg_rows = jnp.broadcast_to(eg, (nb, G_DV))
        for s in range(nb):
            so_ref[s, h] = eg_rows[s:s + 1, :] * s_ref[s, h] + k_cols[:, s:s + 1] * u[s:s + 1, :]
        gz = _token_rows(pr_ref, nb, ROW_GZ + h)
        outs.append(_rms(o, gn_ref[:, h * G_DV:(h + 1) * G_DV]) * (gz * _sigmoid(gz)))
    og_ref[...] = jnp.concatenate(outs, axis=1).astype(BF16)


def _gdn_sample(proj_rows, gate_bias, alog, conv_w_rows, g_norm, s_all, conv_t, s_acc, l):
    depth, b = s_all.shape[:2]
    nb = SAMPLE_BLOCK
    dv = G_HEADS * G_DV
    s_spec = _layer_spec((nb, G_HEADS, G_DK, G_DV), l, lambda i: (i, 0, 0, 0))
    return pl.pallas_call(
        _gdn_sample_kernel,
        grid=(b // nb,),
        in_specs=[
            pl.BlockSpec((nb * ROWS_PER_TOKEN, LANES), lambda i: (i, 0)),
            _layer_spec((1, LANES), l, lambda i: (0, 0)),
            _layer_spec((1, LANES), l, lambda i: (0, 0)),
            _layer_spec((CONV_W, CONV_ROWS, LANES), l, lambda i: (0, 0, 0)),
            _layer_spec((1, dv), l, lambda i: (0, 0)),
            s_spec,
            _layer_spec((CONV_W - 1, CONV_ROWS, nb, LANES), l, lambda i: (0, 0, i, 0)),
            pl.BlockSpec(memory_space=pl.ANY),
        ],
        out_specs=[
            pl.BlockSpec((nb, dv), lambda i: (i, 0)),
            s_spec,
            pl.BlockSpec((CONV_W - 1, CONV_ROWS, nb, LANES), lambda i: (0, 0, i, 0)),
        ],
        out_shape=[
            jax.ShapeDtypeStruct((b, dv), BF16),
            jax.ShapeDtypeStruct(s_all.shape, F32),
            jax.ShapeDtypeStruct((CONV_W - 1, CONV_ROWS, b, LANES), F32),
        ],
        input_output_aliases={7: 1},
        compiler_params=_cparams("parallel"),
        name="gdn_sample",
    )(proj_rows, gate_bias, alog, conv_w_rows, g_norm, s_all, conv_t, s_acc)


def _arrange_w_in(w_in):
    n_m = 2 * M_HEADS * M_DK + 2 * M_HEADS * M_DV
    n_g = 4 * G_HEADS * G_DK
    g0 = n_m + 2 * M_HEADS
    cols = jnp.concatenate([w_in[..., :n_m], w_in[..., g0:g0 + n_g], w_in[..., n_m:g0], w_in[..., g0 + n_g:]],
                           axis=-1).astype(BF16)
    return jnp.pad(cols, ((0, 0), (0, 0), (0, N_PROJ - cols.shape[-1])))


def _lane_row(parts, depth):
    row = jnp.zeros((depth, 1, LANES), F32)
    for off, arr in parts:
        row = row.at[:, 0, off:off + arr.shape[-1]].set(arr.astype(F32))
    return row


def kernel(x_prompt, x_sample, mem_prompt, state_mlstm_C, state_mlstm_n, state_mlstm_m, state_gdn_S,
           state_gdn_conv, cache_mem_k, cache_mem_v, norm_mix, w_in, mlstm_b_i, mlstm_b_f, mlstm_norm,
           gdn_conv_w, gdn_A_log, gdn_dt_bias, gdn_norm, w_out, norm_xattn, norm_mem, xattn_wq, xattn_wk,
           xattn_wv, xattn_wo, norm_ffn, ffn_w_gate, ffn_w_up, ffn_w_down, norm_final):
    bp, seq, _ = x_prompt.shape
    bs = x_sample.shape[0]
    depth = w_in.shape[0]
    dq = X_HEADS * X_DH
    tm_p, tm_s = 512, bs

    w_in_b = _arrange_w_in(w_in)
    w_out_b = w_out.astype(BF16)
    wq_b, wk_b, wv_b, wo_b = (w.astype(BF16) for w in (xattn_wq, xattn_wk, xattn_wv, xattn_wo))
    wg_b, wu_b, wd_b = ffn_w_gate.astype(BF16), ffn_w_up.astype(BF16), ffn_w_down.astype(BF16)
    gate_bias = _lane_row([(GATE_I, mlstm_b_i), (GATE_F, mlstm_b_f), (GATE_A, gdn_dt_bias)], depth)
    alog = _lane_row([(GATE_A, gdn_A_log)], depth)
    g_mix, g_x, g_mem, g_ffn = (g.reshape(depth, 1, D_MODEL) for g in (norm_mix, norm_xattn, norm_mem, norm_ffn))
    m_norm = mlstm_norm.reshape(depth, 1, M_HEADS * M_DV)
    g_norm = gdn_norm.reshape(depth, 1, G_HEADS * G_DV)
    conv_w_rows = gdn_conv_w.reshape(depth, CONV_W, CONV_ROWS, LANES)
    conv_t = jnp.transpose(state_gdn_conv.reshape(depth, bs, CONV_W - 1, CONV_ROWS, LANES), (0, 2, 3, 1, 4))
    n_all = state_mlstm_n.reshape(depth, bs, M_HEADS * M_DK)
    mem_k = cache_mem_k.reshape(depth, bs, N_MEM * X_HEADS, X_DH)
    mem_v = cache_mem_v.reshape(depth, bs, N_MEM * X_HEADS, X_DH)

    mk_all, mv_all = _mem_kv(mem_prompt.reshape(bp * N_MEM, D_MODEL), g_mem, wk_b, wv_b)

    xp = x_prompt.reshape(bp * seq, D_MODEL)
    xs = x_sample.reshape(bs, D_MODEL)
    c_acc = jnp.zeros(state_mlstm_C.shape, F32)
    s_acc = jnp.zeros(state_gdn_S.shape, F32)
    outs = [[] for _ in range(8)]
    for l in range(depth):
        proj3 = _proj(xp, g_mix, w_in_b, l, tm_p).reshape(bp, seq, N_PROJ)
        hm, c_p, n_p, m_p = _mlstm_prompt(proj3, gate_bias, m_norm, l)
        og, s_p, cv_p = _gdn_prompt(proj3, gate_bias, alog, gdn_conv_w, g_norm, l)
        xp = _mix_out(xp, hm.reshape(bp * seq, -1), og.reshape(bp * seq, -1), w_out_b, l, tm_p)
        xp = _xattn_prompt(xp, g_x, wq_b, mk_all, mv_all, wo_b, l, tm_p, seq)
        xp = _ffn(xp, g_ffn, wg_b, wu_b, wd_b, l, tm_p)
        for lst, val in zip(outs[:5], (c_p, n_p.reshape(bp, M_HEADS, M_DK), m_p[:, 0, :M_HEADS], s_p, cv_p)):
            lst.append(val)

        proj_rows = _proj(xs, g_mix, w_in_b, l, tm_s).reshape(bs * ROWS_PER_TOKEN, LANES)
        hm, c_acc, n_s, m_s = _mlstm_sample(proj_rows, gate_bias, m_norm, state_mlstm_C, n_all, state_mlstm_m,
                                            c_acc, l)
        og, s_acc, cv_s = _gdn_sample(proj_rows, gate_bias, alog, conv_w_rows, g_norm, state_gdn_S, conv_t,
                                      s_acc, l)
        xs = _mix_out(xs, hm, og, w_out_b, l, tm_s)
        xs = _xattn_sample(xs, g_x, wq_b, mem_k, mem_v, wo_b, l)
        xs = _ffn(xs, g_ffn, wg_b, wu_b, wd_b, l, tm_s)
        cv_s = jnp.transpose(cv_s, (2, 0, 1, 3)).reshape(bs, CONV_W - 1, G_CONV_CH)
        for lst, val in zip(outs[5:], (n_s.reshape(bs, M_HEADS, M_DK), m_s, cv_s)):
            lst.append(val)

    y_prompt = _final_norm(xp, norm_final.reshape(1, -1), tm_p).reshape(bp, seq, D_MODEL)
    y_sample = _final_norm(xs, norm_final.reshape(1, -1), tm_s).reshape(bs, 1, D_MODEL)
    pc, pn, pm, ps, pconv, sn, sm, sconv = (jnp.stack(o) for o in outs)
    mem_shape = (depth, bp, N_MEM, X_HEADS, X_DH)
    return (y_prompt, y_sample, pc, pn, pm, ps, pconv, mk_all.reshape(mem_shape), mv_all.reshape(mem_shape),
            c_acc, sn, sm, s_acc, sconv)
```

```python
import functools
import math

import jax
import jax.numpy as jnp
from jax import lax
from jax.experimental import pallas as pl
from jax.experimental.pallas import tpu as pltpu

F32 = jnp.float32
BF16 = jnp.bfloat16

D_MODEL = 2048
M_HEADS = 4
M_DK = 128
M_DV = 256
G_HEADS = 8
G_DK = 128
G_DV = 128
CONV_W = 4
G_CONV_CH = G_HEADS * (2 * G_DK + G_DV)
X_HEADS = 4
X_DH = 128
N_MEM = 256
D_FF = 5632
GATE_CAP = 15.0
EPS = 1e-6

LANES = 128
SUBLANES = 8
N_WIDE = 2 * M_HEADS * M_DK + 2 * M_HEADS * M_DV + 4 * G_HEADS * G_DK
PROJ_TN = 1280
N_PROJ = -(-(N_WIDE + LANES) // PROJ_TN) * PROJ_TN
ROWS_PER_TOKEN = N_PROJ // LANES
GATE_ROW = N_WIDE // LANES
GATE_I, GATE_F, GATE_A, GATE_B = 0, 4, 8, 16
ROW_MQ, ROW_MK, ROW_MV, ROW_MO = 0, 4, 8, 16
ROW_GX, ROW_GZ = 24, 48
CONV_ROWS = G_CONV_CH // LANES

VMEM_LIMIT = 52 * 1024 * 1024

M_CHUNK = 256
G_BLOCK = 256
G_CHUNK = 64
G_PAIR = 2 * G_CHUNK
SAMPLE_BLOCK = 8

NT_DIMS = (((1,), (1,)), ((), ()))
TN_DIMS = (((0,), (0,)), ((), ()))


def _cparams(*sem):
    return pltpu.CompilerParams(dimension_semantics=sem, vmem_limit_bytes=VMEM_LIMIT)


def _dot(a, b):
    return jnp.dot(a, b, preferred_element_type=F32)


def _dot_nt(a, b):
    return lax.dot_general(a, b, NT_DIMS, preferred_element_type=F32)


def _dot_tn(a, b):
    return lax.dot_general(a, b, TN_DIMS, preferred_element_type=F32)


def _dot_f32(a, b):
    return jnp.dot(a, b, preferred_element_type=F32, precision=lax.Precision.HIGHEST)


def _rms(x, g):
    ms = jnp.mean(x * x, axis=-1, keepdims=True)
    return x * lax.rsqrt(ms + EPS) * g


def _sigmoid(x):
    return 1.0 / (1.0 + jnp.exp(-x))


def _softplus(x):
    return jnp.maximum(x, 0.0) + jnp.log1p(jnp.exp(-jnp.abs(x)))


def _softcap(x):
    return GATE_CAP * jnp.tanh(x / GATE_CAP)


def _layer_spec(shape, l, tail):
    return pl.BlockSpec((None,) + shape, lambda *g: (l,) + tail(*g))


def _proj_kernel(x_ref, g_ref, w_ref, o_ref, h_scr):
    @pl.when(pl.program_id(1) == 0)
    def _():
        h_scr[...] = _rms(x_ref[...], g_ref[...]).astype(BF16)

    o_ref[...] = _dot(h_scr[...], w_ref[...])


def _proj(x, g, w, l, tm):
    m = x.shape[0]
    return pl.pallas_call(
        _proj_kernel,
        grid=(m // tm, N_PROJ // PROJ_TN),
        in_specs=[
            pl.BlockSpec((tm, D_MODEL), lambda i, j: (i, 0)),
            _layer_spec((1, D_MODEL), l, lambda i, j: (0, 0)),
            _layer_spec((D_MODEL, PROJ_TN), l, lambda i, j: (0, j)),
        ],
        out_specs=pl.BlockSpec((tm, PROJ_TN), lambda i, j: (i, j)),
        out_shape=jax.ShapeDtypeStruct((m, N_PROJ), F32),
        scratch_shapes=[pltpu.VMEM((tm, D_MODEL), BF16)],
        compiler_params=_cparams("parallel", "arbitrary"),
        name="proj",
    )(x, g, w)


def _mem_kv_kernel(x_ref, g_ref, wk_ref, wv_ref, k_ref, v_ref):
    h = _rms(x_ref[...], g_ref[...]).astype(BF16)
    k_ref[...] = _dot(h, wk_ref[...])
    v_ref[...] = _dot(h, wv_ref[...])


def _mem_kv(mem, g, wk, wv):
    depth = wk.shape[0]
    m = mem.shape[0]
    dq = X_HEADS * X_DH
    tm = N_MEM
    w_spec = pl.BlockSpec((None, D_MODEL, dq), lambda l, i: (l, 0, 0))
    o_spec = pl.BlockSpec((None, tm, dq), lambda l, i: (l, i, 0))
    return pl.pallas_call(
        _mem_kv_kernel,
        grid=(depth, m // tm),
        in_specs=[
            pl.BlockSpec((tm, D_MODEL), lambda l, i: (i, 0)),
            pl.BlockSpec((None, 1, D_MODEL), lambda l, i: (l, 0, 0)),
            w_spec, w_spec,
        ],
        out_specs=[o_spec, o_spec],
        out_shape=[jax.ShapeDtypeStruct((depth, m, dq), F32)] * 2,
        compiler_params=_cparams("parallel", "parallel"),
        name="mem_kv",
    )(mem, g, wk, wv)


def _mix_out_kernel(x_ref, hm_ref, og_ref, w1_ref, w2_ref, o_ref):
    acc = _dot(hm_ref[...], w1_ref[...])
    acc = acc + _dot(og_ref[...], w2_ref[...])
    o_ref[...] = x_ref[...] + acc


def _mix_out(x, hm, og, w_out, l, tm):
    m = x.shape[0]
    half = D_MODEL // 2
    return pl.pallas_call(
        _mix_out_kernel,
        grid=(m // tm,),
        in_specs=[
            pl.BlockSpec((tm, D_MODEL), lambda i: (i, 0)),
            pl.BlockSpec((tm, half), lambda i: (i, 0)),
            pl.BlockSpec((tm, half), lambda i: (i, 0)),
            _layer_spec((half, D_MODEL), l, lambda i: (0, 0)),
            _layer_spec((half, D_MODEL), l, lambda i: (1, 0)),
        ],
        out_specs=pl.BlockSpec((tm, D_MODEL), lambda i: (i, 0)),
        out_shape=jax.ShapeDtypeStruct((m, D_MODEL), F32),
        compiler_params=_cparams("parallel"),
        name="mix_out",
    )(x, hm, og, w_out, w_out)


def _ffn_kernel(x_ref, g_ref, wg_ref, wu_ref, wd_ref, gf_ref, o_ref, h_scr, acc_scr, *, final):
    j = pl.program_id(1)

    @pl.when(j == 0)
    def _():
        h_scr[...] = _rms(x_ref[...], g_ref[...]).astype(BF16)
        acc_scr[...] = jnp.zeros_like(acc_scr)

    h = h_scr[...]
    a = _dot(h, wg_ref[...])
    b = _dot(h, wu_ref[...])
    t = (a * _sigmoid(a)) * b
    acc_scr[...] += _dot(t.astype(BF16), wd_ref[...])

    @pl.when(j == pl.num_programs(1) - 1)
    def _():
        y = x_ref[...] + acc_scr[...]
        o_ref[...] = _rms(y, gf_ref[...]) if final else y


def _ffn(x, g, w_gate, w_up, w_down, g_final, l, tm, final, tf=512):
    m = x.shape[0]
    return pl.pallas_call(
        functools.partial(_ffn_kernel, final=final),
        grid=(m // tm, D_FF // tf),
        in_specs=[
            pl.BlockSpec((tm, D_MODEL), lambda i, j: (i, 0)),
            _layer_spec((1, D_MODEL), l, lambda i, j: (0, 0)),
            _layer_spec((D_MODEL, tf), l, lambda i, j: (0, j)),
            _layer_spec((D_MODEL, tf), l, lambda i, j: (0, j)),
            _layer_spec((tf, D_MODEL), l, lambda i, j: (j, 0)),
            pl.BlockSpec((1, D_MODEL), lambda i, j: (0, 0)),
        ],
        out_specs=pl.BlockSpec((tm, D_MODEL), lambda i, j: (i, 0)),
        out_shape=jax.ShapeDtypeStruct((m, D_MODEL), F32),
        scratch_shapes=[pltpu.VMEM((tm, D_MODEL), BF16), pltpu.VMEM((tm, D_MODEL), F32)],
        compiler_params=_cparams("parallel", "arbitrary"),
        name="ffn",
    )(x, g, w_gate, w_up, w_down, g_final)


def _softmax_rows(s):
    s = s - jnp.max(s, axis=-1, keepdims=True)
    p = jnp.exp(s)
    return p / jnp.sum(p, axis=-1, keepdims=True)


def _xattn_prompt_kernel(x_ref, g_ref, wq_ref, k_ref, v_ref, wo_ref, o_ref):
    x = x_ref[...]
    q = _dot(_rms(x, g_ref[...]).astype(BF16), wq_ref[...])
    heads = []
    for h in range(X_HEADS):
        sl = slice(h * X_DH, (h + 1) * X_DH)
        s = _dot_nt(q[:, sl].astype(BF16), k_ref[:, sl].astype(BF16)) * (X_DH ** -0.5)
        p = _softmax_rows(s)
        heads.append(_dot(p.astype(BF16), v_ref[:, sl].astype(BF16)).astype(BF16))
    o = jnp.concatenate(heads, axis=1)
    o_ref[...] = x + _dot(o, wo_ref[...])


def _xattn_prompt(x, g, wq, mk, mv, wo, l, tm, seq):
    m = x.shape[0]
    dq = X_HEADS * X_DH
    per_seq = seq // tm
    kv_spec = pl.BlockSpec((None, N_MEM, dq), lambda i: (l, i // per_seq, 0))
    return pl.pallas_call(
        _xattn_prompt_kernel,
        grid=(m // tm,),
        in_specs=[
            pl.BlockSpec((tm, D_MODEL), lambda i: (i, 0)),
            _layer_spec((1, D_MODEL), l, lambda i: (0, 0)),
            _layer_spec((D_MODEL, dq), l, lambda i: (0, 0)),
            kv_spec, kv_spec,
            _layer_spec((dq, D_MODEL), l, lambda i: (0, 0)),
        ],
        out_specs=pl.BlockSpec((tm, D_MODEL), lambda i: (i, 0)),
        out_shape=jax.ShapeDtypeStruct((m, D_MODEL), F32),
        compiler_params=_cparams("parallel"),
        name="xattn_prompt",
    )(x, g, wq, mk, mv, wo)


def _xattn_sample_kernel(x_ref, g_ref, wq_ref, k_ref, v_ref, wo_ref, o_ref):
    nb = x_ref.shape[0]
    nk = N_MEM * X_HEADS
    x = x_ref[...]
    q = _dot(_rms(x, g_ref[...]).astype(BF16), wq_ref[...])
    qh = [q[:, h * X_DH:(h + 1) * X_DH] for h in range(X_HEADS)]
    row = lax.broadcasted_iota(jnp.int32, (SUBLANES, nk), 0)
    col = lax.broadcasted_iota(jnp.int32, (SUBLANES, nk), 1)
    own = (col & (X_HEADS - 1)) == (row & (X_HEADS - 1))
    row8 = lax.broadcasted_iota(jnp.int32, (SUBLANES, X_DH), 0)
    outs = [jnp.zeros((nb, X_DH), F32) for _ in range(X_HEADS)]
    scores = []
    for s in range(nb):
        q8 = jnp.zeros((SUBLANES, X_DH), F32)
        for h in range(X_HEADS):
            q8 = jnp.where(row8 == h, qh[h][s:s + 1, :], q8)
        scores.append(_dot_nt(q8.astype(BF16), k_ref[s].astype(BF16)) * (X_DH ** -0.5))
    probs = [_softmax_rows(jnp.where(own, sc, -jnp.inf)).astype(BF16) for sc in scores]
    o8s = [_dot(probs[s], v_ref[s].astype(BF16)) for s in range(nb)]
    for s in range(nb):
        for h in range(X_HEADS):
            outs[h] = jnp.where(row8[:nb] == s, o8s[s][h:h + 1, :], outs[h])
    o = jnp.concatenate(outs, axis=1).astype(BF16)
    o_ref[...] = x + _dot(o, wo_ref[...])


def _xattn_sample(x, g, wq, mk, mv, wo, l):
    m = x.shape[0]
    nb = SAMPLE_BLOCK
    dq = X_HEADS * X_DH
    kv_spec = _layer_spec((nb, N_MEM * X_HEADS, X_DH), l, lambda i: (i, 0, 0))
    return pl.pallas_call(
        _xattn_sample_kernel,
        grid=(m // nb,),
        in_specs=[
            pl.BlockSpec((nb, D_MODEL), lambda i: (i, 0)),
            _layer_spec((1, D_MODEL), l, lambda i: (0, 0)),
            _layer_spec((D_MODEL, dq), l, lambda i: (0, 0)),
            kv_spec, kv_spec,
            _layer_spec((dq, D_MODEL), l, lambda i: (0, 0)),
        ],
        out_specs=pl.BlockSpec((nb, D_MODEL), lambda i: (i, 0)),
        out_shape=jax.ShapeDtypeStruct((m, D_MODEL), F32),
        compiler_params=_cparams("parallel"),
        name="xattn_sample",
    )(x, g, wq, mk, mv, wo)


def _mlstm_prompt_kernel(q_ref, k_ref, v_ref, mo_ref, gt_ref, gb_ref, mn_ref,
                         hm_ref, c_ref, n_ref, m_ref, c_scr, n_scr, m_scr):
    L = q_ref.shape[0]
    c = pl.program_id(1)

    @pl.when(c == 0)
    def _():
        c_scr[...] = jnp.zeros_like(c_scr)
        n_scr[...] = jnp.zeros_like(n_scr)
        m_scr[...] = jnp.zeros_like(m_scr)

    gates = gt_ref[...] + gb_ref[...]
    capped = _softcap(gates)
    logf = -_softplus(-capped)
    row = lax.broadcasted_iota(jnp.int32, (L, L), 0)
    col = lax.broadcasted_iota(jnp.int32, (L, L), 1)
    causal = row >= col
    bcum = _dot_f32(causal.astype(F32), logf)
    lane = lax.broadcasted_iota(jnp.int32, (L, LANES), 1)
    z = jnp.where(lane < GATE_F, capped, bcum)
    zt = z.T

    hs = range(M_HEADS)
    i_col = [z[:, GATE_I + h:GATE_I + h + 1] for h in hs]
    b_col = [z[:, GATE_F + h:GATE_F + h + 1] for h in hs]
    m_prev = [m_scr[0:1, h:h + 1] for h in hs]
    d = [jnp.where(causal, b_col[h] + (zt[GATE_I + h:GATE_I + h + 1, :] - zt[GATE_F + h:GATE_F + h + 1, :]),
                   -jnp.inf) for h in hs]
    inter = [b_col[h] + m_prev[h] for h in hs]
    m_t = [jnp.maximum(inter[h], jnp.max(d[h], axis=1, keepdims=True)) for h in hs]
    q = [q_ref[:, h * M_DK:(h + 1) * M_DK] for h in hs]
    k = [k_ref[:, h * M_DK:(h + 1) * M_DK] * (M_DK ** -0.5) for h in hs]
    qb = [q[h].astype(BF16) for h in hs]
    vb = [v_ref[:, h * M_DV:(h + 1) * M_DV].astype(BF16) for h in hs]
    qk = [_dot_nt(qb[h], k[h].astype(BF16)) for h in hs]
    qc = [_dot(qb[h], c_scr[h].astype(BF16)) for h in hs]
    w = [jnp.exp(d[h] - m_t[h]) * qk[h] for h in hs]
    a = [jnp.exp(inter[h] - m_t[h]) for h in hs]
    wv = [_dot(w[h].astype(BF16), vb[h]) for h in hs]
    nq = [a[h] * jnp.sum(q[h] * n_scr[h], axis=1, keepdims=True) + jnp.sum(w[h], axis=1, keepdims=True) for h in hs]
    hh = [(a[h] * qc[h] + wv[h]) / jnp.maximum(jnp.abs(nq[h]), jnp.exp(-m_t[h])) for h in hs]
    for h in hs:
        y = _rms(hh[h], mn_ref[:, h * M_DV:(h + 1) * M_DV])
        y = y * _sigmoid(mo_ref[:, h * M_DV:(h + 1) * M_DV])
        hm_ref[:, h * M_DV:(h + 1) * M_DV] = y.astype(BF16)

    m_new = [m_t[h][L - 1:L, :] for h in hs]
    b_last = [b_col[h][L - 1:L, :] for h in hs]
    kw = [k[h] * jnp.exp(b_last[h] - b_col[h] + i_col[h] - m_new[h]) for h in hs]
    upd = [_dot_tn(kw[h].astype(BF16), vb[h]) for h in hs]
    for h in hs:
        decay = jnp.exp(b_last[h] + m_prev[h] - m_new[h])
        c_scr[h] = decay * c_scr[h] + upd[h]
        n_scr[h] = decay * n_scr[h] + jnp.sum(kw[h], axis=0, keepdims=True)
        m_scr[0:1, h:h + 1] = m_new[h]

    @pl.when(c == pl.num_programs(1) - 1)
    def _():
        c_ref[...] = c_scr[...]
        n_ref[...] = n_scr[...]
        m_ref[...] = m_scr[...]


def _mlstm_prompt(proj3, gate_bias, m_norm, l):
    b, t, _ = proj3.shape
    L = M_CHUNK
    dqk = M_HEADS * M_DK
    dv = M_HEADS * M_DV
    return pl.pallas_call(
        _mlstm_prompt_kernel,
        grid=(b, t // L),
        in_specs=[
            pl.BlockSpec((None, L, dqk), lambda i, c: (i, c, 0)),
            pl.BlockSpec((None, L, dqk), lambda i, c: (i, c, 1)),
            pl.BlockSpec((None, L, dv), lambda i, c: (i, c, 1)),
            pl.BlockSpec((None, L, dv), lambda i, c: (i, c, 2)),
            pl.BlockSpec((None, L, LANES), lambda i, c: (i, c, GATE_ROW)),
            _layer_spec((1, LANES), l, lambda i, c: (0, 0)),
            _layer_spec((1, dv), l, lambda i, c: (0, 0)),
        ],
        out_specs=[
            pl.BlockSpec((None, L, dv), lambda i, c: (i, c, 0)),
            pl.BlockSpec((None, M_HEADS, M_DK, M_DV), lambda i, c: (i, 0, 0, 0)),
            pl.BlockSpec((None, M_HEADS, 1, M_DK), lambda i, c: (i, 0, 0, 0)),
            pl.BlockSpec((None, 1, LANES), lambda i, c: (i, 0, 0)),
        ],
        out_shape=[
            jax.ShapeDtypeStruct((b, t, dv), BF16),
            jax.ShapeDtypeStruct((b, M_HEADS, M_DK, M_DV), F32),
            jax.ShapeDtypeStruct((b, M_HEADS, 1, M_DK), F32),
            jax.ShapeDtypeStruct((b, 1, LANES), F32),
        ],
        scratch_shapes=[
            pltpu.VMEM((M_HEADS, M_DK, M_DV), F32),
            pltpu.VMEM((M_HEADS, 1, M_DK), F32),
            pltpu.VMEM((1, LANES), F32),
        ],
        compiler_params=_cparams("parallel", "arbitrary"),
        name="mlstm_prompt",
    )(proj3, proj3, proj3, proj3, proj3, gate_bias, m_norm)


def _token_rows(pr_ref, nb, off):
    return pr_ref[pl.ds(off, nb, stride=ROWS_PER_TOKEN), :]


def _token_rows2(pr_ref, nb, off):
    return jnp.concatenate([_token_rows(pr_ref, nb, off), _token_rows(pr_ref, nb, off + 1)], axis=1)


def _pick_row(tiles):
    rid = lax.broadcasted_iota(jnp.int32, tiles[0].shape, 0)
    out = tiles[0]
    for s in range(1, len(tiles)):
        out = jnp.where(rid == s, tiles[s], out)
    return out


def _columns(tiles):
    pad = LANES - SUBLANES * len(tiles)
    stack = jnp.concatenate(tiles + ([jnp.zeros((pad, LANES), F32)] if pad else []), axis=0)
    return stack.T


def _mlstm_sample_kernel(pr_ref, gb_ref, mn_ref, c_ref, n_ref, m_ref, acc_ref,
                         hm_ref, co_ref, no_ref, mo_ref):
    del acc_ref
    nb = m_ref.shape[0]
    gates = _token_rows(pr_ref, nb, GATE_ROW) + gb_ref[...]
    capped = _softcap(gates)
    logf = -_softplus(-capped)
    i4 = capped[:, GATE_I:GATE_I + M_HEADS]
    f4 = logf[:, GATE_F:GATE_F + M_HEADS]
    inter = f4 + m_ref[...]
    m_new = jnp.maximum(inter, i4)
    a4 = jnp.exp(inter - m_new)
    w4 = jnp.exp(i4 - m_new)
    e4 = jnp.exp(-m_new)
    mo_ref[...] = m_new

    scale = M_DK ** -0.5
    ys, n_rows = [], []
    for h in range(M_HEADS):
        a, wi, e = a4[:, h:h + 1], w4[:, h:h + 1], e4[:, h:h + 1]
        q = _token_rows(pr_ref, nb, ROW_MQ + h)
        k = _token_rows(pr_ref, nb, ROW_MK + h) * scale
        v = _token_rows2(pr_ref, nb, ROW_MV + 2 * h)
        og = _token_rows2(pr_ref, nb, ROW_MO + 2 * h)
        n_old = n_ref[:, h * M_DK:(h + 1) * M_DK]
        wk = wi * k
        n_rows.append(a * n_old + wk)
        wts = wi * jnp.sum(q * k, axis=1, keepdims=True)
        nq = a * jnp.sum(q * n_old, axis=1, keepdims=True) + wts
        wk_cols = _columns([wk])
        a_rows = jnp.broadcast_to(a, (nb, M_DV))
        qb = q.astype(BF16)
        qc = []
        for s in range(nb):
            c_old = c_ref[s, h]
            qc.append(_dot(qb, c_old.astype(BF16)))
            co_ref[s, h] = a_rows[s:s + 1, :] * c_old + wk_cols[:, s:s + 1] * v[s:s + 1, :]
        num = a * _pick_row(qc) + wts * v
        hh = num / jnp.maximum(jnp.abs(nq), e)
        ys.append(_rms(hh, mn_ref[:, h * M_DV:(h + 1) * M_DV]) * _sigmoid(og))
    hm_ref[...] = jnp.concatenate(ys, axis=1).astype(BF16)
    no_ref[...] = jnp.concatenate(n_rows, axis=1)


def _mlstm_sample(proj_rows, gate_bias, m_norm, c_all, n_all, m_all, c_acc, l):
    depth, b = m_all.shape[:2]
    nb = SAMPLE_BLOCK
    dv = M_HEADS * M_DV
    dk = M_HEADS * M_DK
    c_spec = _layer_spec((nb, M_HEADS, M_DK, M_DV), l, lambda i: (i, 0, 0, 0))
    return pl.pallas_call(
        _mlstm_sample_kernel,
        grid=(b // nb,),
        in_specs=[
            pl.BlockSpec((nb * ROWS_PER_TOKEN, LANES), lambda i: (i, 0)),
            _layer_spec((1, LANES), l, lambda i: (0, 0)),
            _layer_spec((1, dv), l, lambda i: (0, 0)),
            c_spec,
            _layer_spec((nb, dk), l, lambda i: (i, 0)),
            _layer_spec((nb, M_HEADS), l, lambda i: (i, 0)),
            pl.BlockSpec(memory_space=pl.ANY),
        ],
        out_specs=[
            pl.BlockSpec((nb, dv), lambda i: (i, 0)),
            c_spec,
            pl.BlockSpec((nb, dk), lambda i: (i, 0)),
            pl.BlockSpec((nb, M_HEADS), lambda i: (i, 0)),
        ],
        out_shape=[
            jax.ShapeDtypeStruct((b, dv), BF16),
            jax.ShapeDtypeStruct(c_all.shape, F32),
            jax.ShapeDtypeStruct((b, dk), F32),
            jax.ShapeDtypeStruct((b, M_HEADS), F32),
        ],
        input_output_aliases={6: 1},
        compiler_params=_cparams("parallel"),
        name="mlstm_sample",
    )(proj_rows, gate_bias, m_norm, c_all, n_all, m_all, c_acc)


def _gdn_gates(gates, alog):
    log_a = -jnp.exp(alog) * _softplus(gates)
    beta = _sigmoid(gates)
    return log_a, beta


def _gdn_prompt_kernel(x_ref, gz_ref, gt_ref, gb_ref, al_ref, cw_ref, gn_ref,
                       og_ref, s_ref, cv_ref, s_scr, tail_scr):
    Lb = x_ref.shape[0]
    L, P = G_CHUNK, G_PAIR
    n_pair = Lb // P
    c = pl.program_id(1)

    @pl.when(c == 0)
    def _():
        s_scr[...] = jnp.zeros_like(s_scr)
        tail_scr[...] = jnp.zeros_like(tail_scr)

    log_a, beta = _gdn_gates(gt_ref[...] + gb_ref[...], al_ref[...])
    row = lax.broadcasted_iota(jnp.int32, (Lb, Lb), 0)
    col = lax.broadcasted_iota(jnp.int32, (Lb, Lb), 1)
    shift = int(math.log2(L))
    same_chunk = jnp.right_shift(row, shift) == jnp.right_shift(col, shift)
    gcum = _dot_f32(jnp.where(same_chunk & (row >= col), 1.0, 0.0), log_a)
    gtot = _dot_f32(jnp.where(same_chunk, 1.0, 0.0), log_a)
    lane = lax.broadcasted_iota(jnp.int32, (Lb, LANES), 1)
    zt = jnp.where(lane < GATE_B, gcum, beta).T

    rp = lax.broadcasted_iota(jnp.int32, (P, P), 0)
    cp = lax.broadcasted_iota(jnp.int32, (P, P), 1)
    same = jnp.right_shift(rp, shift) == jnp.right_shift(cp, shift)
    incl = same & (rp >= cp)
    strict = same & (rp > cp)

    def conv_cols(c0):
        xc = x_ref[:, c0:c0 + LANES]
        xp = jnp.concatenate([tail_scr[:, c0:c0 + LANES], xc], axis=0)
        y = xc * cw_ref[3:4, c0:c0 + LANES]
        for j in range(CONV_W - 1):
            y = y + xp[5 + j:5 + j + Lb, :] * cw_ref[j:j + 1, c0:c0 + LANES]
        return y * _sigmoid(y)

    heads = []
    for h in range(G_HEADS):
        qc = conv_cols(h * G_DK)
        kc = conv_cols(G_HEADS * G_DK + h * G_DK)
        vc = conv_cols(2 * G_HEADS * G_DK + h * G_DV)
        q2 = qc * (lax.rsqrt(jnp.sum(qc * qc, axis=1, keepdims=True) + EPS) * (G_DK ** -0.5))
        k2 = kc * lax.rsqrt(jnp.sum(kc * kc, axis=1, keepdims=True) + EPS)
        g_col = gcum[:, GATE_A + h:GATE_A + h + 1]
        t_col = gtot[:, GATE_A + h:GATE_A + h + 1]
        b_col = beta[:, GATE_B + h:GATE_B + h + 1]
        eg = jnp.exp(g_col)
        heads.append(dict(
            qb=q2.astype(BF16), kb=k2.astype(BF16), eg=eg, b_col=b_col,
            vek=jnp.concatenate([vc, eg * k2], axis=1),
            kd=(k2 * jnp.exp(t_col - g_col)).astype(BF16),
            dS=jnp.exp(t_col),
        ))

    blocks = []
    for h in range(G_HEADS):
        hd = heads[h]
        for b in range(n_pair):
            r0 = b * P
            kb = hd["kb"][r0:r0 + P]
            g_col = gcum[r0:r0 + P, GATE_A + h:GATE_A + h + 1]
            g_row = zt[GATE_A + h:GATE_A + h + 1, r0:r0 + P]
            b_row = zt[GATE_B + h:GATE_B + h + 1, r0:r0 + P]
            dec = jnp.exp(jnp.where(incl, g_col - g_row, -jnp.inf))
            a_mat = jnp.where(strict, dec * _dot_nt(kb, kb) * b_row, 0.0)
            qkd = (_dot_nt(hd["qb"][r0:r0 + P], kb) * dec).astype(BF16)
            blocks.append(dict(h=h, r0=r0, mpow=-a_mat, qacc=-a_mat, qkd=qkd))

    for _ in range(int(math.log2(L)) - 1):
        for blk in blocks:
            mb = blk["mpow"].astype(BF16)
            blk["mpow"] = _dot(mb, mb)
        for blk in blocks:
            blk["qacc"] = blk["qacc"] + blk["mpow"] + _dot(blk["qacc"].astype(BF16), blk["mpow"].astype(BF16))

    for blk in blocks:
        hd = heads[blk["h"]]
        r0 = blk["r0"]
        vek = hd["vek"][r0:r0 + P]
        t = (vek + _dot(blk["qacc"].astype(BF16), vek.astype(BF16))) * hd["b_col"][r0:r0 + P]
        blk["uv"] = t[:, :G_DV]
        blk["kw"] = t[:, G_DV:].astype(BF16)

    states = [s_scr[h] for h in range(G_HEADS)]
    wus = [[] for _ in range(G_HEADS)]
    qss = [[] for _ in range(G_HEADS)]
    for ci in range(Lb // L):
        r0 = ci * L
        for h in range(G_HEADS):
            hd = heads[h]
            blk = blocks[h * n_pair + r0 // P]
            o0 = r0 - blk["r0"]
            lhs = jnp.concatenate([blk["kw"][o0:o0 + L], hd["qb"][r0:r0 + L]], axis=0)
            r = _dot(lhs, states[h].astype(BF16))
            wu = (blk["uv"][o0:o0 + L] - r[:L]).astype(BF16)
            qss[h].append(r[L:])
            wus[h].append(wu)
            states[h] = hd["dS"][r0:r0 + 1] * states[h] + _dot_tn(hd["kd"][r0:r0 + L], wu)
    for h in range(G_HEADS):
        s_scr[h] = states[h]

    for h in range(G_HEADS):
        hd = heads[h]
        for b in range(n_pair):
            r0 = b * P
            blk = blocks[h * n_pair + b]
            i0 = r0 // L
            wu = jnp.concatenate(wus[h][i0:i0 + P // L], axis=0)
            qs = jnp.concatenate(qss[h][i0:i0 + P // L], axis=0)
            o = hd["eg"][r0:r0 + P] * qs + _dot(blk["qkd"], wu)
            y = _rms(o, gn_ref[:, h * G_DV:(h + 1) * G_DV])
            gz = gz_ref[r0:r0 + P, h * G_DV:(h + 1) * G_DV]
            og_ref[r0:r0 + P, h * G_DV:(h + 1) * G_DV] = (y * (gz * _sigmoid(gz))).astype(BF16)

    tail_scr[...] = x_ref[Lb - SUBLANES:Lb, :]

    @pl.when(c == pl.num_programs(1) - 1)
    def _():
        s_ref[...] = s_scr[...]
        cv_ref[...] = x_ref[Lb - (CONV_W - 1):Lb, :]


def _gdn_prompt(proj3, gate_bias, alog, conv_w, g_norm, l):
    b, t, _ = proj3.shape
    Lb = G_BLOCK
    dv = G_HEADS * G_DV
    return pl.pallas_call(
        _gdn_prompt_kernel,
        grid=(b, t // Lb),
        in_specs=[
            pl.BlockSpec((None, Lb, G_CONV_CH), lambda i, c: (i, c, 1)),
            pl.BlockSpec((None, Lb, dv), lambda i, c: (i, c, 6)),
            pl.BlockSpec((None, Lb, LANES), lambda i, c: (i, c, GATE_ROW)),
            _layer_spec((1, LANES), l, lambda i, c: (0, 0)),
            _layer_spec((1, LANES), l, lambda i, c: (0, 0)),
            _layer_spec((CONV_W, G_CONV_CH), l, lambda i, c: (0, 0)),
            _layer_spec((1, dv), l, lambda i, c: (0, 0)),
        ],
        out_specs=[
            pl.BlockSpec((None, Lb, dv), lambda i, c: (i, c, 0)),
            pl.BlockSpec((None, G_HEADS, G_DK, G_DV), lambda i, c: (i, 0, 0, 0)),
            pl.BlockSpec((None, CONV_W - 1, G_CONV_CH), lambda i, c: (i, 0, 0)),
        ],
        out_shape=[
            jax.ShapeDtypeStruct((b, t, dv), BF16),
            jax.ShapeDtypeStruct((b, G_HEADS, G_DK, G_DV), F32),
            jax.ShapeDtypeStruct((b, CONV_W - 1, G_CONV_CH), F32),
        ],
        scratch_shapes=[
            pltpu.VMEM((G_HEADS, G_DK, G_DV), F32),
            pltpu.VMEM((SUBLANES, G_CONV_CH), F32),
        ],
        compiler_params=_cparams("parallel", "arbitrary"),
        name="gdn_prompt",
    )(proj3, proj3, proj3, gate_bias, alog, conv_w, g_norm)


def _gdn_sample_kernel(pr_ref, gb_ref, al_ref, cw_ref, gn_ref, s_ref, cb_ref, acc_ref,
                       og_ref, so_ref, co_ref):
    del acc_ref
    nb = cb_ref.shape[2]
    log_a, beta = _gdn_gates(_token_rows(pr_ref, nb, GATE_ROW) + gb_ref[...], al_ref[...])
    eg8 = jnp.exp(log_a)

    ys = []
    for r in range(CONV_ROWS):
        xr = _token_rows(pr_ref, nb, ROW_GX + r)
        y = xr * cw_ref[CONV_W - 1, r:r + 1, :]
        for j in range(CONV_W - 1):
            y = y + cb_ref[j, r] * cw_ref[j, r:r + 1, :]
        co_ref[0, r] = cb_ref[1, r]
        co_ref[1, r] = cb_ref[2, r]
        co_ref[2, r] = xr
        ys.append(y * _sigmoid(y))

    outs = []
    for h in range(G_HEADS):
        qc, kc, v = ys[h], ys[G_HEADS + h], ys[2 * G_HEADS + h]
        q = qc * lax.rsqrt(jnp.sum(qc * qc, axis=1, keepdims=True) + EPS) * (G_DK ** -0.5)
        k = kc * lax.rsqrt(jnp.sum(kc * kc, axis=1, keepdims=True) + EPS)
        eg = eg8[:, GATE_A + h:GATE_A + h + 1]
        bt = beta[:, GATE_B + h:GATE_B + h + 1]
        qb, kb = q.astype(BF16), k.astype(BF16)
        ks, qs = [], []
        for s in range(nb):
            sb = s_ref[s, h].astype(BF16)
            ks.append(_dot(kb, sb))
            qs.append(_dot(qb, sb))
        u = bt * (v - eg * _pick_row(ks))
        o = eg * _pick_row(qs) + jnp.sum(q * k, axis=1, keepdims=True) * u
        k_cols = _columns([k])
        eg_rows = jnp.broadcast_to(eg, (nb, G_DV))
        for s in range(nb):
            so_ref[s, h] = eg_rows[s:s + 1, :] * s_ref[s, h] + k_cols[:, s:s + 1] * u[s:s + 1, :]
        gz = _token_rows(pr_ref, nb, ROW_GZ + h)
        outs.append(_rms(o, gn_ref[:, h * G_DV:(h + 1) * G_DV]) * (gz * _sigmoid(gz)))
    og_ref[...] = jnp.concatenate(outs, axis=1).astype(BF16)


def _gdn_sample(proj_rows, gate_bias, alog, conv_w_rows, g_norm, s_all, conv_t, s_acc, l):
    depth, b = s_all.shape[:2]
    nb = SAMPLE_BLOCK
    dv = G_HEADS * G_DV
    s_spec = _layer_spec((nb, G_HEADS, G_DK, G_DV), l, lambda i: (i, 0, 0, 0))
    return pl.pallas_call(
        _gdn_sample_kernel,
        grid=(b // nb,),
        in_specs=[
            pl.BlockSpec((nb * ROWS_PER_TOKEN, LANES), lambda i: (i, 0)),
            _layer_spec((1, LANES), l, lambda i: (0, 0)),
            _layer_spec((1, LANES), l, lambda i: (0, 0)),
            _layer_spec((CONV_W, CONV_ROWS, LANES), l, lambda i: (0, 0, 0)),
            _layer_spec((1, dv), l, lambda i: (0, 0)),
            s_spec,
            _layer_spec((CONV_W - 1, CONV_ROWS, nb, LANES), l, lambda i: (0, 0, i, 0)),
            pl.BlockSpec(memory_space=pl.ANY),
        ],
        out_specs=[
            pl.BlockSpec((nb, dv), lambda i: (i, 0)),
            s_spec,
            pl.BlockSpec((CONV_W - 1, CONV_ROWS, nb, LANES), lambda i: (0, 0, i, 0)),
        ],
        out_shape=[
            jax.ShapeDtypeStruct((b, dv), BF16),
            jax.ShapeDtypeStruct(s_all.shape, F32),
            jax.ShapeDtypeStruct((CONV_W - 1, CONV_ROWS, b, LANES), F32),
        ],
        input_output_aliases={7: 1},
        compiler_params=_cparams("parallel"),
        name="gdn_sample",
    )(proj_rows, gate_bias, alog, conv_w_rows, g_norm, s_all, conv_t, s_acc)


def _arrange_w_in_kernel(w_ref, o_ref):
    n_m = 2 * M_HEADS * M_DK + 2 * M_HEADS * M_DV
    n_g = 4 * G_HEADS * G_DK
    g0 = n_m + 2 * M_HEADS
    rows = w_ref.shape[0]
    o_ref[:, :n_m] = w_ref[:, :n_m].astype(BF16)
    o_ref[:, n_m:N_WIDE] = w_ref[:, g0:g0 + n_g].astype(BF16)
    n_gate = w_ref.shape[1] - N_WIDE
    gate = jnp.concatenate([w_ref[:, n_m:g0], w_ref[:, g0 + n_g:], jnp.zeros((rows, LANES - n_gate), F32)], axis=1)
    o_ref[:, N_WIDE:N_WIDE + LANES] = gate.astype(BF16)
    o_ref[:, N_WIDE + LANES:] = jnp.zeros((rows, N_PROJ - N_WIDE - LANES), BF16)


def _arrange_w_in(w_in, tr=256):
    depth, rows, n_in = w_in.shape
    return pl.pallas_call(
        _arrange_w_in_kernel,
        grid=(depth, rows // tr),
        in_specs=[pl.BlockSpec((None, tr, n_in), lambda l, i: (l, i, 0))],
        out_specs=pl.BlockSpec((None, tr, N_PROJ), lambda l, i: (l, i, 0)),
        out_shape=jax.ShapeDtypeStruct((depth, rows, N_PROJ), BF16),
        compiler_params=_cparams("parallel", "parallel"),
        name="arrange_w_in",
    )(w_in)


def _lane_row(parts, depth):
    row = jnp.zeros((depth, 1, LANES), F32)
    for off, arr in parts:
        row = row.at[:, 0, off:off + arr.shape[-1]].set(arr.astype(F32))
    return row


def kernel(x_prompt, x_sample, mem_prompt, state_mlstm_C, state_mlstm_n, state_mlstm_m, state_gdn_S,
           state_gdn_conv, cache_mem_k, cache_mem_v, norm_mix, w_in, mlstm_b_i, mlstm_b_f, mlstm_norm,
           gdn_conv_w, gdn_A_log, gdn_dt_bias, gdn_norm, w_out, norm_xattn, norm_mem, xattn_wq, xattn_wk,
           xattn_wv, xattn_wo, norm_ffn, ffn_w_gate, ffn_w_up, ffn_w_down, norm_final):
    bp, seq, _ = x_prompt.shape
    bs = x_sample.shape[0]
    depth = w_in.shape[0]
    dq = X_HEADS * X_DH
    tm_p, tm_s = 512, bs

    w_in_b = _arrange_w_in(w_in)
    w_out_b = w_out.astype(BF16)
    wq_b, wk_b, wv_b, wo_b = (w.astype(BF16) for w in (xattn_wq, xattn_wk, xattn_wv, xattn_wo))
    wg_b, wu_b, wd_b = ffn_w_gate.astype(BF16), ffn_w_up.astype(BF16), ffn_w_down.astype(BF16)
    gate_bias = _lane_row([(GATE_I, mlstm_b_i), (GATE_F, mlstm_b_f), (GATE_A, gdn_dt_bias)], depth)
    alog = _lane_row([(GATE_A, gdn_A_log)], depth)
    g_mix, g_x, g_mem, g_ffn = (g.reshape(depth, 1, D_MODEL) for g in (norm_mix, norm_xattn, norm_mem, norm_ffn))
    g_final = norm_final.reshape(1, D_MODEL)
    m_norm = mlstm_norm.reshape(depth, 1, M_HEADS * M_DV)
    g_norm = gdn_norm.reshape(depth, 1, G_HEADS * G_DV)
    conv_w_rows = gdn_conv_w.reshape(depth, CONV_W, CONV_ROWS, LANES)
    conv_t = jnp.transpose(state_gdn_conv.reshape(depth, bs, CONV_W - 1, CONV_ROWS, LANES), (0, 2, 3, 1, 4))
    n_all = state_mlstm_n.reshape(depth, bs, M_HEADS * M_DK)
    mem_k = cache_mem_k.reshape(depth, bs, N_MEM * X_HEADS, X_DH)
    mem_v = cache_mem_v.reshape(depth, bs, N_MEM * X_HEADS, X_DH)

    mk_all, mv_all = _mem_kv(mem_prompt.reshape(bp * N_MEM, D_MODEL), g_mem, wk_b, wv_b)

    xp = x_prompt.reshape(bp * seq, D_MODEL)
    xs = x_sample.reshape(bs, D_MODEL)
    c_acc = jnp.zeros(state_mlstm_C.shape, F32)
    s_acc = jnp.zeros(state_gdn_S.shape, F32)
    outs = [[] for _ in range(8)]
    for l in range(depth):
        proj3 = _proj(xp, g_mix, w_in_b, l, 2 * tm_p).reshape(bp, seq, N_PROJ)
        hm, c_p, n_p, m_p = _mlstm_prompt(proj3, gate_bias, m_norm, l)
        og, s_p, cv_p = _gdn_prompt(proj3, gate_bias, alog, gdn_conv_w, g_norm, l)
        xp = _mix_out(xp, hm.reshape(bp * seq, -1), og.reshape(bp * seq, -1), w_out_b, l, tm_p)
        xp = _xattn_prompt(xp, g_x, wq_b, mk_all, mv_all, wo_b, l, tm_p, seq)
        xp = _ffn(xp, g_ffn, wg_b, wu_b, wd_b, g_final, l, tm_p, l == depth - 1)
        for lst, val in zip(outs[:5], (c_p, n_p.reshape(bp, M_HEADS, M_DK), m_p[:, 0, :M_HEADS], s_p, cv_p)):
            lst.append(val)

        proj_rows = _proj(xs, g_mix, w_in_b, l, tm_s).reshape(bs * ROWS_PER_TOKEN, LANES)
        hm, c_acc, n_s, m_s = _mlstm_sample(proj_rows, gate_bias, m_norm, state_mlstm_C, n_all, state_mlstm_m,
                                            c_acc, l)
        og, s_acc, cv_s = _gdn_sample(proj_rows, gate_bias, alog, conv_w_rows, g_norm, state_gdn_S, conv_t,
                                      s_acc, l)
        xs = _mix_out(xs, hm, og, w_out_b, l, tm_s)
        xs = _xattn_sample(xs, g_x, wq_b, mem_k, mem_v, wo_b, l)
        xs = _ffn(xs, g_ffn, wg_b, wu_b, wd_b, g_final, l, tm_s, l == depth - 1)
        cv_s = jnp.transpose(cv_s, (2, 0, 1, 3)).reshape(bs, CONV_W - 1, G_CONV_CH)
        for lst, val in zip(outs[5:], (n_s.reshape(bs, M_HEADS, M_DK), m_s, cv_s)):
            lst.append(val)

    y_prompt = xp.reshape(bp, seq, D_MODEL)
    y_sample = xs.reshape(bs, 1, D_MODEL)
    pc, pn, pm, ps, pconv, sn, sm, sconv = (jnp.stack(o) for o in outs)
    mem_shape = (depth, bp, N_MEM, X_HEADS, X_DH)
    return (y_prompt, y_sample, pc, pn, pm, ps, pconv, mk_all.reshape(mem_shape), mv_all.reshape(mem_shape),
            c_acc, sn, sm, s_acc, sconv)
```

```python
import functools
import math

import jax
import jax.numpy as jnp
from jax import lax
from jax.experimental import pallas as pl
from jax.experimental.pallas import tpu as pltpu

F32 = jnp.float32
BF16 = jnp.bfloat16

D_MODEL = 2048
M_HEADS = 4
M_DK = 128
M_DV = 256
G_HEADS = 8
G_DK = 128
G_DV = 128
CONV_W = 4
G_CONV_CH = G_HEADS * (2 * G_DK + G_DV)
X_HEADS = 4
X_DH = 128
N_MEM = 256
D_FF = 5632
GATE_CAP = 15.0
EPS = 1e-6

LANES = 128
SUBLANES = 8
N_WIDE = 2 * M_HEADS * M_DK + 2 * M_HEADS * M_DV + 4 * G_HEADS * G_DK
PROJ_TN = 1280
N_PROJ = -(-(N_WIDE + LANES) // PROJ_TN) * PROJ_TN
ROWS_PER_TOKEN = N_PROJ // LANES
GATE_ROW = N_WIDE // LANES
GATE_I, GATE_F, GATE_A, GATE_B = 0, 4, 8, 16
ROW_MQ, ROW_MK, ROW_MV, ROW_MO = 0, 4, 8, 16
ROW_GX, ROW_GZ = 24, 48
CONV_ROWS = G_CONV_CH // LANES

VMEM_LIMIT = 56 * 1024 * 1024

M_CHUNK = 256
G_BLOCK = 256
G_CHUNK = 64
G_PAIR = 2 * G_CHUNK
SAMPLE_BLOCK = 8

NT_DIMS = (((1,), (1,)), ((), ()))
TN_DIMS = (((0,), (0,)), ((), ()))


def _cparams(*sem):
    return pltpu.CompilerParams(dimension_semantics=sem, vmem_limit_bytes=VMEM_LIMIT)


def _dot(a, b):
    return jnp.dot(a, b, preferred_element_type=F32)


def _dot_nt(a, b):
    return lax.dot_general(a, b, NT_DIMS, preferred_element_type=F32)


def _dot_tn(a, b):
    return lax.dot_general(a, b, TN_DIMS, preferred_element_type=F32)


def _dot_f32(a, b):
    return jnp.dot(a, b, preferred_element_type=F32, precision=lax.Precision.HIGHEST)


def _rms(x, g):
    ms = jnp.mean(x * x, axis=-1, keepdims=True)
    return x * lax.rsqrt(ms + EPS) * g


def _sigmoid(x):
    return 1.0 / (1.0 + jnp.exp(-x))


def _softplus(x):
    return jnp.maximum(x, 0.0) + jnp.log1p(jnp.exp(-jnp.abs(x)))


def _softcap(x):
    return GATE_CAP * jnp.tanh(x / GATE_CAP)


def _layer_spec(shape, l, tail):
    return pl.BlockSpec((None,) + shape, lambda *g: (l,) + tail(*g))


def _proj_kernel(x_ref, g_ref, w_ref, o_ref, h_scr):
    @pl.when(pl.program_id(1) == 0)
    def _():
        h_scr[...] = _rms(x_ref[...], g_ref[...]).astype(BF16)

    o_ref[...] = _dot(h_scr[...], w_ref[...])


def _proj(x, g, w, l, tm):
    m = x.shape[0]
    return pl.pallas_call(
        _proj_kernel,
        grid=(m // tm, N_PROJ // PROJ_TN),
        in_specs=[
            pl.BlockSpec((tm, D_MODEL), lambda i, j: (i, 0)),
            _layer_spec((1, D_MODEL), l, lambda i, j: (0, 0)),
            _layer_spec((D_MODEL, PROJ_TN), l, lambda i, j: (0, j)),
        ],
        out_specs=pl.BlockSpec((tm, PROJ_TN), lambda i, j: (i, j)),
        out_shape=jax.ShapeDtypeStruct((m, N_PROJ), F32),
        scratch_shapes=[pltpu.VMEM((tm, D_MODEL), BF16)],
        compiler_params=_cparams("parallel", "arbitrary"),
        name="proj",
    )(x, g, w)


def _mem_kv_kernel(x_ref, g_ref, wk_ref, wv_ref, k_ref, v_ref):
    h = _rms(x_ref[...], g_ref[...]).astype(BF16)
    k_ref[...] = _dot(h, wk_ref[...])
    v_ref[...] = _dot(h, wv_ref[...])


def _mem_kv(mem, g, wk, wv):
    depth = wk.shape[0]
    m = mem.shape[0]
    dq = X_HEADS * X_DH
    tm = N_MEM
    w_spec = pl.BlockSpec((None, D_MODEL, dq), lambda l, i: (l, 0, 0))
    o_spec = pl.BlockSpec((None, tm, dq), lambda l, i: (l, i, 0))
    return pl.pallas_call(
        _mem_kv_kernel,
        grid=(depth, m // tm),
        in_specs=[
            pl.BlockSpec((tm, D_MODEL), lambda l, i: (i, 0)),
            pl.BlockSpec((None, 1, D_MODEL), lambda l, i: (l, 0, 0)),
            w_spec, w_spec,
        ],
        out_specs=[o_spec, o_spec],
        out_shape=[jax.ShapeDtypeStruct((depth, m, dq), F32)] * 2,
        compiler_params=_cparams("parallel", "parallel"),
        name="mem_kv",
    )(mem, g, wk, wv)


def _mix_out_kernel(x_ref, hm_ref, og_ref, w1_ref, w2_ref, o_ref):
    acc = _dot(hm_ref[...], w1_ref[...])
    acc = acc + _dot(og_ref[...], w2_ref[...])
    o_ref[...] = x_ref[...] + acc


def _mix_out(x, hm, og, w_out, l, tm):
    m = x.shape[0]
    half = D_MODEL // 2
    return pl.pallas_call(
        _mix_out_kernel,
        grid=(m // tm,),
        in_specs=[
            pl.BlockSpec((tm, D_MODEL), lambda i: (i, 0)),
            pl.BlockSpec((tm, half), lambda i: (i, 0)),
            pl.BlockSpec((tm, half), lambda i: (i, 0)),
            _layer_spec((half, D_MODEL), l, lambda i: (0, 0)),
            _layer_spec((half, D_MODEL), l, lambda i: (1, 0)),
        ],
        out_specs=pl.BlockSpec((tm, D_MODEL), lambda i: (i, 0)),
        out_shape=jax.ShapeDtypeStruct((m, D_MODEL), F32),
        compiler_params=_cparams("parallel"),
        name="mix_out",
    )(x, hm, og, w_out, w_out)


def _ffn_kernel(x_ref, g_ref, wg_ref, wu_ref, wd_ref, gf_ref, o_ref, h_scr, *, final):
    j = pl.program_id(1)

    @pl.when(j == 0)
    def _():
        x = x_ref[...]
        h_scr[...] = _rms(x, g_ref[...]).astype(BF16)
        o_ref[...] = x

    h = h_scr[...]
    a = _dot(h, wg_ref[...])
    b = _dot(h, wu_ref[...])
    t = (a * _sigmoid(a)) * b
    o_ref[...] += _dot(t.astype(BF16), wd_ref[...])

    if final:
        @pl.when(j == pl.num_programs(1) - 1)
        def _():
            o_ref[...] = _rms(o_ref[...], gf_ref[...])


def _ffn(x, g, w_gate, w_up, w_down, g_final, l, tm, final, tf=512):
    m = x.shape[0]
    return pl.pallas_call(
        functools.partial(_ffn_kernel, final=final),
        grid=(m // tm, D_FF // tf),
        in_specs=[
            pl.BlockSpec((tm, D_MODEL), lambda i, j: (i, 0)),
            _layer_spec((1, D_MODEL), l, lambda i, j: (0, 0)),
            _layer_spec((D_MODEL, tf), l, lambda i, j: (0, j)),
            _layer_spec((D_MODEL, tf), l, lambda i, j: (0, j)),
            _layer_spec((tf, D_MODEL), l, lambda i, j: (j, 0)),
            pl.BlockSpec((1, D_MODEL), lambda i, j: (0, 0)),
        ],
        out_specs=pl.BlockSpec((tm, D_MODEL), lambda i, j: (i, 0)),
        out_shape=jax.ShapeDtypeStruct((m, D_MODEL), F32),
        scratch_shapes=[pltpu.VMEM((tm, D_MODEL), BF16)],
        compiler_params=_cparams("parallel", "arbitrary"),
        name="ffn",
    )(x, g, w_gate, w_up, w_down, g_final)


def _softmax_rows(s):
    s = s - jnp.max(s, axis=-1, keepdims=True)
    p = jnp.exp(s)
    return p / jnp.sum(p, axis=-1, keepdims=True)


def _xattn_prompt_kernel(x_ref, g_ref, wq_ref, k_ref, v_ref, wo_ref, o_ref):
    x = x_ref[...]
    q = _dot(_rms(x, g_ref[...]).astype(BF16), wq_ref[...])
    heads = []
    for h in range(X_HEADS):
        sl = slice(h * X_DH, (h + 1) * X_DH)
        s = _dot_nt(q[:, sl].astype(BF16), k_ref[:, sl].astype(BF16)) * (X_DH ** -0.5)
        p = _softmax_rows(s)
        heads.append(_dot(p.astype(BF16), v_ref[:, sl].astype(BF16)).astype(BF16))
    o = jnp.concatenate(heads, axis=1)
    o_ref[...] = x + _dot(o, wo_ref[...])


def _xattn_prompt(x, g, wq, mk, mv, wo, l, tm, seq):
    m = x.shape[0]
    dq = X_HEADS * X_DH
    per_seq = seq // tm
    kv_spec = pl.BlockSpec((None, N_MEM, dq), lambda i: (l, i // per_seq, 0))
    return pl.pallas_call(
        _xattn_prompt_kernel,
        grid=(m // tm,),
        in_specs=[
            pl.BlockSpec((tm, D_MODEL), lambda i: (i, 0)),
            _layer_spec((1, D_MODEL), l, lambda i: (0, 0)),
            _layer_spec((D_MODEL, dq), l, lambda i: (0, 0)),
            kv_spec, kv_spec,
            _layer_spec((dq, D_MODEL), l, lambda i: (0, 0)),
        ],
        out_specs=pl.BlockSpec((tm, D_MODEL), lambda i: (i, 0)),
        out_shape=jax.ShapeDtypeStruct((m, D_MODEL), F32),
        compiler_params=_cparams("parallel"),
        name="xattn_prompt",
    )(x, g, wq, mk, mv, wo)


def _xattn_sample_kernel(x_ref, g_ref, wq_ref, k_ref, v_ref, wo_ref, o_ref):
    nb = x_ref.shape[0]
    nk = N_MEM * X_HEADS
    x = x_ref[...]
    q = _dot(_rms(x, g_ref[...]).astype(BF16), wq_ref[...])
    qh = [q[:, h * X_DH:(h + 1) * X_DH] for h in range(X_HEADS)]
    row = lax.broadcasted_iota(jnp.int32, (SUBLANES, nk), 0)
    col = lax.broadcasted_iota(jnp.int32, (SUBLANES, nk), 1)
    own = (col & (X_HEADS - 1)) == (row & (X_HEADS - 1))
    row8 = lax.broadcasted_iota(jnp.int32, (SUBLANES, X_DH), 0)
    outs = [jnp.zeros((nb, X_DH), F32) for _ in range(X_HEADS)]
    scores = []
    for s in range(nb):
        q8 = jnp.zeros((SUBLANES, X_DH), F32)
        for h in range(X_HEADS):
            q8 = jnp.where(row8 == h, qh[h][s:s + 1, :], q8)
        scores.append(_dot_nt(q8.astype(BF16), k_ref[s].astype(BF16)) * (X_DH ** -0.5))
    probs = [_softmax_rows(jnp.where(own, sc, -jnp.inf)).astype(BF16) for sc in scores]
    o8s = [_dot(probs[s], v_ref[s].astype(BF16)) for s in range(nb)]
    for s in range(nb):
        for h in range(X_HEADS):
            outs[h] = jnp.where(row8[:nb] == s, o8s[s][h:h + 1, :], outs[h])
    o = jnp.concatenate(outs, axis=1).astype(BF16)
    o_ref[...] = x + _dot(o, wo_ref[...])


def _xattn_sample(x, g, wq, mk, mv, wo, l):
    m = x.shape[0]
    nb = SAMPLE_BLOCK
    dq = X_HEADS * X_DH
    kv_spec = _layer_spec((nb, N_MEM * X_HEADS, X_DH), l, lambda i: (i, 0, 0))
    return pl.pallas_call(
        _xattn_sample_kernel,
        grid=(m // nb,),
        in_specs=[
            pl.BlockSpec((nb, D_MODEL), lambda i: (i, 0)),
            _layer_spec((1, D_MODEL), l, lambda i: (0, 0)),
            _layer_spec((D_MODEL, dq), l, lambda i: (0, 0)),
            kv_spec, kv_spec,
            _layer_spec((dq, D_MODEL), l, lambda i: (0, 0)),
        ],
        out_specs=pl.BlockSpec((nb, D_MODEL), lambda i: (i, 0)),
        out_shape=jax.ShapeDtypeStruct((m, D_MODEL), F32),
        compiler_params=_cparams("parallel"),
        name="xattn_sample",
    )(x, g, wq, mk, mv, wo)


def _mlstm_prompt_kernel(q_ref, k_ref, v_ref, mo_ref, gt_ref, gb_ref, mn_ref,
                         hm_ref, c_ref, n_ref, m_ref, c_scr, n_scr, m_scr):
    L = q_ref.shape[0]
    c = pl.program_id(1)

    @pl.when(c == 0)
    def _():
        c_scr[...] = jnp.zeros_like(c_scr)
        n_scr[...] = jnp.zeros_like(n_scr)
        m_scr[...] = jnp.zeros_like(m_scr)

    gates = gt_ref[...] + gb_ref[...]
    capped = _softcap(gates)
    logf = -_softplus(-capped)
    row = lax.broadcasted_iota(jnp.int32, (L, L), 0)
    col = lax.broadcasted_iota(jnp.int32, (L, L), 1)
    causal = row >= col
    bcum = _dot_f32(causal.astype(F32), logf)
    lane = lax.broadcasted_iota(jnp.int32, (L, LANES), 1)
    z = jnp.where(lane < GATE_F, capped, bcum)
    zt = z.T

    hs = range(M_HEADS)
    i_col = [z[:, GATE_I + h:GATE_I + h + 1] for h in hs]
    b_col = [z[:, GATE_F + h:GATE_F + h + 1] for h in hs]
    m_prev = [m_scr[0:1, h:h + 1] for h in hs]
    d = [jnp.where(causal, b_col[h] + (zt[GATE_I + h:GATE_I + h + 1, :] - zt[GATE_F + h:GATE_F + h + 1, :]),
                   -jnp.inf) for h in hs]
    inter = [b_col[h] + m_prev[h] for h in hs]
    m_t = [jnp.maximum(inter[h], jnp.max(d[h], axis=1, keepdims=True)) for h in hs]
    q = [q_ref[:, h * M_DK:(h + 1) * M_DK] for h in hs]
    k = [k_ref[:, h * M_DK:(h + 1) * M_DK] * (M_DK ** -0.5) for h in hs]
    qb = [q[h].astype(BF16) for h in hs]
    vb = [v_ref[:, h * M_DV:(h + 1) * M_DV].astype(BF16) for h in hs]
    qk = [_dot_nt(qb[h], k[h].astype(BF16)) for h in hs]
    qc = [_dot(qb[h], c_scr[h].astype(BF16)) for h in hs]
    w = [jnp.exp(d[h] - m_t[h]) * qk[h] for h in hs]
    a = [jnp.exp(inter[h] - m_t[h]) for h in hs]
    wv = [_dot(w[h].astype(BF16), vb[h]) for h in hs]
    nq = [a[h] * jnp.sum(q[h] * n_scr[h], axis=1, keepdims=True) + jnp.sum(w[h], axis=1, keepdims=True) for h in hs]
    hh = [(a[h] * qc[h] + wv[h]) / jnp.maximum(jnp.abs(nq[h]), jnp.exp(-m_t[h])) for h in hs]
    for h in hs:
        y = _rms(hh[h], mn_ref[:, h * M_DV:(h + 1) * M_DV])
        y = y * _sigmoid(mo_ref[:, h * M_DV:(h + 1) * M_DV])
        hm_ref[:, h * M_DV:(h + 1) * M_DV] = y.astype(BF16)

    m_new = [m_t[h][L - 1:L, :] for h in hs]
    b_last = [b_col[h][L - 1:L, :] for h in hs]
    kw = [k[h] * jnp.exp(b_last[h] - b_col[h] + i_col[h] - m_new[h]) for h in hs]
    upd = [_dot_tn(kw[h].astype(BF16), vb[h]) for h in hs]
    for h in hs:
        decay = jnp.exp(b_last[h] + m_prev[h] - m_new[h])
        c_scr[h] = decay * c_scr[h] + upd[h]
        n_scr[h] = decay * n_scr[h] + jnp.sum(kw[h], axis=0, keepdims=True)
        m_scr[0:1, h:h + 1] = m_new[h]

    @pl.when(c == pl.num_programs(1) - 1)
    def _():
        c_ref[...] = c_scr[...]
        n_ref[...] = n_scr[...]
        m_ref[...] = m_scr[...]


def _mlstm_prompt(proj3, gate_bias, m_norm, l):
    b, t, _ = proj3.shape
    L = M_CHUNK
    dqk = M_HEADS * M_DK
    dv = M_HEADS * M_DV
    return pl.pallas_call(
        _mlstm_prompt_kernel,
        grid=(b, t // L),
        in_specs=[
            pl.BlockSpec((None, L, dqk), lambda i, c: (i, c, 0)),
            pl.BlockSpec((None, L, dqk), lambda i, c: (i, c, 1)),
            pl.BlockSpec((None, L, dv), lambda i, c: (i, c, 1)),
            pl.BlockSpec((None, L, dv), lambda i, c: (i, c, 2)),
            pl.BlockSpec((None, L, LANES), lambda i, c: (i, c, GATE_ROW)),
            _layer_spec((1, LANES), l, lambda i, c: (0, 0)),
            _layer_spec((1, dv), l, lambda i, c: (0, 0)),
        ],
        out_specs=[
            pl.BlockSpec((None, L, dv), lambda i, c: (i, c, 0)),
            pl.BlockSpec((None, M_HEADS, M_DK, M_DV), lambda i, c: (i, 0, 0, 0)),
            pl.BlockSpec((None, M_HEADS, 1, M_DK), lambda i, c: (i, 0, 0, 0)),
            pl.BlockSpec((None, 1, LANES), lambda i, c: (i, 0, 0)),
        ],
        out_shape=[
            jax.ShapeDtypeStruct((b, t, dv), BF16),
            jax.ShapeDtypeStruct((b, M_HEADS, M_DK, M_DV), F32),
            jax.ShapeDtypeStruct((b, M_HEADS, 1, M_DK), F32),
            jax.ShapeDtypeStruct((b, 1, LANES), F32),
        ],
        scratch_shapes=[
            pltpu.VMEM((M_HEADS, M_DK, M_DV), F32),
            pltpu.VMEM((M_HEADS, 1, M_DK), F32),
            pltpu.VMEM((1, LANES), F32),
        ],
        compiler_params=_cparams("parallel", "arbitrary"),
        name="mlstm_prompt",
    )(proj3, proj3, proj3, proj3, proj3, gate_bias, m_norm)


def _token_rows(pr_ref, nb, off):
    return pr_ref[pl.ds(off, nb, stride=ROWS_PER_TOKEN), :]


def _token_rows2(pr_ref, nb, off):
    return jnp.concatenate([_token_rows(pr_ref, nb, off), _token_rows(pr_ref, nb, off + 1)], axis=1)


def _pick_row(tiles):
    rid = lax.broadcasted_iota(jnp.int32, tiles[0].shape, 0)
    out = tiles[0]
    for s in range(1, len(tiles)):
        out = jnp.where(rid == s, tiles[s], out)
    return out


def _columns(tiles):
    pad = LANES - SUBLANES * len(tiles)
    stack = jnp.concatenate(tiles + ([jnp.zeros((pad, LANES), F32)] if pad else []), axis=0)
    return stack.T


def _mlstm_sample_kernel(pr_ref, gb_ref, mn_ref, c_ref, n_ref, m_ref, acc_ref,
                         hm_ref, co_ref, no_ref, mo_ref):
    del acc_ref
    nb = m_ref.shape[0]
    gates = _token_rows(pr_ref, nb, GATE_ROW) + gb_ref[...]
    capped = _softcap(gates)
    logf = -_softplus(-capped)
    i4 = capped[:, GATE_I:GATE_I + M_HEADS]
    f4 = logf[:, GATE_F:GATE_F + M_HEADS]
    inter = f4 + m_ref[...]
    m_new = jnp.maximum(inter, i4)
    a4 = jnp.exp(inter - m_new)
    w4 = jnp.exp(i4 - m_new)
    e4 = jnp.exp(-m_new)
    mo_ref[...] = m_new

    scale = M_DK ** -0.5
    ys, n_rows = [], []
    for h in range(M_HEADS):
        a, wi, e = a4[:, h:h + 1], w4[:, h:h + 1], e4[:, h:h + 1]
        q = _token_rows(pr_ref, nb, ROW_MQ + h)
        k = _token_rows(pr_ref, nb, ROW_MK + h) * scale
        v = _token_rows2(pr_ref, nb, ROW_MV + 2 * h)
        og = _token_rows2(pr_ref, nb, ROW_MO + 2 * h)
        n_old = n_ref[:, h * M_DK:(h + 1) * M_DK]
        wk = wi * k
        n_rows.append(a * n_old + wk)
        wts = wi * jnp.sum(q * k, axis=1, keepdims=True)
        nq = a * jnp.sum(q * n_old, axis=1, keepdims=True) + wts
        wk_cols = _columns([wk])
        a_rows = jnp.broadcast_to(a, (nb, M_DV))
        qb = q.astype(BF16)
        qc = []
        for s in range(nb):
            c_old = c_ref[s, h]
            qc.append(_dot(qb, c_old.astype(BF16)))
            co_ref[s, h] = a_rows[s:s + 1, :] * c_old + wk_cols[:, s:s + 1] * v[s:s + 1, :]
        num = a * _pick_row(qc) + wts * v
        hh = num / jnp.maximum(jnp.abs(nq), e)
        ys.append(_rms(hh, mn_ref[:, h * M_DV:(h + 1) * M_DV]) * _sigmoid(og))
    hm_ref[...] = jnp.concatenate(ys, axis=1).astype(BF16)
    no_ref[...] = jnp.concatenate(n_rows, axis=1)


def _mlstm_sample(proj_rows, gate_bias, m_norm, c_all, n_all, m_all, c_acc, l):
    depth, b = m_all.shape[:2]
    nb = SAMPLE_BLOCK
    dv = M_HEADS * M_DV
    dk = M_HEADS * M_DK
    c_spec = _layer_spec((nb, M_HEADS, M_DK, M_DV), l, lambda i: (i, 0, 0, 0))
    return pl.pallas_call(
        _mlstm_sample_kernel,
        grid=(b // nb,),
        in_specs=[
            pl.BlockSpec((nb * ROWS_PER_TOKEN, LANES), lambda i: (i, 0)),
            _layer_spec((1, LANES), l, lambda i: (0, 0)),
            _layer_spec((1, dv), l, lambda i: (0, 0)),
            c_spec,
            _layer_spec((nb, dk), l, lambda i: (i, 0)),
            _layer_spec((nb, M_HEADS), l, lambda i: (i, 0)),
            pl.BlockSpec(memory_space=pl.ANY),
        ],
        out_specs=[
            pl.BlockSpec((nb, dv), lambda i: (i, 0)),
            c_spec,
            pl.BlockSpec((nb, dk), lambda i: (i, 0)),
            pl.BlockSpec((nb, M_HEADS), lambda i: (i, 0)),
        ],
        out_shape=[
            jax.ShapeDtypeStruct((b, dv), BF16),
            jax.ShapeDtypeStruct(c_all.shape, F32),
            jax.ShapeDtypeStruct((b, dk), F32),
            jax.ShapeDtypeStruct((b, M_HEADS), F32),
        ],
        input_output_aliases={6: 1},
        compiler_params=_cparams("parallel"),
        name="mlstm_sample",
    )(proj_rows, gate_bias, m_norm, c_all, n_all, m_all, c_acc)


def _gdn_gates(gates, alog):
    log_a = -jnp.exp(alog) * _softplus(gates)
    beta = _sigmoid(gates)
    return log_a, beta


def _gdn_prompt_kernel(x_ref, gz_ref, gt_ref, gb_ref, al_ref, cw_ref, gn_ref,
                       og_ref, s_ref, cv_ref, s_scr, tail_scr):
    Lb = x_ref.shape[0]
    L, P = G_CHUNK, G_PAIR
    n_pair = Lb // P
    c = pl.program_id(1)

    @pl.when(c == 0)
    def _():
        s_scr[...] = jnp.zeros_like(s_scr)
        tail_scr[...] = jnp.zeros_like(tail_scr)

    log_a, beta = _gdn_gates(gt_ref[...] + gb_ref[...], al_ref[...])
    row = lax.broadcasted_iota(jnp.int32, (Lb, Lb), 0)
    col = lax.broadcasted_iota(jnp.int32, (Lb, Lb), 1)
    shift = int(math.log2(L))
    same_chunk = jnp.right_shift(row, shift) == jnp.right_shift(col, shift)
    gcum = _dot_f32(jnp.where(same_chunk & (row >= col), 1.0, 0.0), log_a)
    gtot = _dot_f32(jnp.where(same_chunk, 1.0, 0.0), log_a)
    lane = lax.broadcasted_iota(jnp.int32, (Lb, LANES), 1)
    zt = jnp.where(lane < GATE_B, gcum, beta).T

    rp = lax.broadcasted_iota(jnp.int32, (P, P), 0)
    cp = lax.broadcasted_iota(jnp.int32, (P, P), 1)
    same = jnp.right_shift(rp, shift) == jnp.right_shift(cp, shift)
    incl = same & (rp >= cp)
    strict = same & (rp > cp)

    def conv_cols(c0):
        xc = x_ref[:, c0:c0 + LANES]
        xp = jnp.concatenate([tail_scr[:, c0:c0 + LANES], xc], axis=0)
        y = xc * cw_ref[3:4, c0:c0 + LANES]
        for j in range(CONV_W - 1):
            y = y + xp[5 + j:5 + j + Lb, :] * cw_ref[j:j + 1, c0:c0 + LANES]
        return y * _sigmoid(y)

    heads = []
    for h in range(G_HEADS):
        qc = conv_cols(h * G_DK)
        kc = conv_cols(G_HEADS * G_DK + h * G_DK)
        vc = conv_cols(2 * G_HEADS * G_DK + h * G_DV)
        q2 = qc * (lax.rsqrt(jnp.sum(qc * qc, axis=1, keepdims=True) + EPS) * (G_DK ** -0.5))
        k2 = kc * lax.rsqrt(jnp.sum(kc * kc, axis=1, keepdims=True) + EPS)
        g_col = gcum[:, GATE_A + h:GATE_A + h + 1]
        t_col = gtot[:, GATE_A + h:GATE_A + h + 1]
        b_col = beta[:, GATE_B + h:GATE_B + h + 1]
        eg = jnp.exp(g_col)
        heads.append(dict(
            qb=q2.astype(BF16), kb=k2.astype(BF16), eg=eg, b_col=b_col,
            vek=jnp.concatenate([vc, eg * k2], axis=1),
            kd=(k2 * jnp.exp(t_col - g_col)).astype(BF16),
            dS=jnp.exp(t_col),
        ))

    blocks = []
    for h in range(G_HEADS):
        hd = heads[h]
        for b in range(n_pair):
            r0 = b * P
            kb = hd["kb"][r0:r0 + P]
            g_col = gcum[r0:r0 + P, GATE_A + h:GATE_A + h + 1]
            g_row = zt[GATE_A + h:GATE_A + h + 1, r0:r0 + P]
            b_row = zt[GATE_B + h:GATE_B + h + 1, r0:r0 + P]
            dec = jnp.exp(jnp.where(incl, g_col - g_row, -jnp.inf))
            a_mat = jnp.where(strict, dec * _dot_nt(kb, kb) * b_row, 0.0)
            qkd = (_dot_nt(hd["qb"][r0:r0 + P], kb) * dec).astype(BF16)
            blocks.append(dict(h=h, r0=r0, mpow=-a_mat, qacc=-a_mat, qkd=qkd))

    for blk in blocks:
        blk["mb"] = blk["mpow"].astype(BF16)
    for _ in range(int(math.log2(L)) - 1):
        for blk in blocks:
            blk["mpow"] = _dot(blk["mb"], blk["mb"])
            blk["mb"] = blk["mpow"].astype(BF16)
        for blk in blocks:
            blk["qacc"] = blk["qacc"] + blk["mpow"] + _dot(blk["qacc"].astype(BF16), blk["mb"])

    for blk in blocks:
        hd = heads[blk["h"]]
        r0 = blk["r0"]
        vek = hd["vek"][r0:r0 + P]
        t = (vek + _dot(blk["qacc"].astype(BF16), vek.astype(BF16))) * hd["b_col"][r0:r0 + P]
        blk["uv"] = t[:, :G_DV]
        blk["kw"] = t[:, G_DV:].astype(BF16)

    states = [s_scr[h] for h in range(G_HEADS)]
    wus = [[] for _ in range(G_HEADS)]
    qss = [[] for _ in range(G_HEADS)]
    for ci in range(Lb // L):
        r0 = ci * L
        for h in range(G_HEADS):
            hd = heads[h]
            blk = blocks[h * n_pair + r0 // P]
            o0 = r0 - blk["r0"]
            lhs = jnp.concatenate([blk["kw"][o0:o0 + L], hd["qb"][r0:r0 + L]], axis=0)
            r = _dot(lhs, states[h].astype(BF16))
            wu = (blk["uv"][o0:o0 + L] - r[:L]).astype(BF16)
            qss[h].append(r[L:])
            wus[h].append(wu)
            states[h] = hd["dS"][r0:r0 + 1] * states[h] + _dot_tn(hd["kd"][r0:r0 + L], wu)
    for h in range(G_HEADS):
        s_scr[h] = states[h]

    for h in range(G_HEADS):
        hd = heads[h]
        for b in range(n_pair):
            r0 = b * P
            blk = blocks[h * n_pair + b]
            i0 = r0 // L
            wu = jnp.concatenate(wus[h][i0:i0 + P // L], axis=0)
            qs = jnp.concatenate(qss[h][i0:i0 + P // L], axis=0)
            o = hd["eg"][r0:r0 + P] * qs + _dot(blk["qkd"], wu)
            y = _rms(o, gn_ref[:, h * G_DV:(h + 1) * G_DV])
            gz = gz_ref[r0:r0 + P, h * G_DV:(h + 1) * G_DV]
            og_ref[r0:r0 + P, h * G_DV:(h + 1) * G_DV] = (y * (gz * _sigmoid(gz))).astype(BF16)

    tail_scr[...] = x_ref[Lb - SUBLANES:Lb, :]

    @pl.when(c == pl.num_programs(1) - 1)
    def _():
        s_ref[...] = s_scr[...]
        cv_ref[...] = x_ref[Lb - (CONV_W - 1):Lb, :]


def _gdn_prompt(proj3, gate_bias, alog, conv_w, g_norm, l):
    b, t, _ = proj3.shape
    Lb = G_BLOCK
    dv = G_HEADS * G_DV
    return pl.pallas_call(
        _gdn_prompt_kernel,
        grid=(b, t // Lb),
        in_specs=[
            pl.BlockSpec((None, Lb, G_CONV_CH), lambda i, c: (i, c, 1)),
            pl.BlockSpec((None, Lb, dv), lambda i, c: (i, c, 6)),
            pl.BlockSpec((None, Lb, LANES), lambda i, c: (i, c, GATE_ROW)),
            _layer_spec((1, LANES), l, lambda i, c: (0, 0)),
            _layer_spec((1, LANES), l, lambda i, c: (0, 0)),
            _layer_spec((CONV_W, G_CONV_CH), l, lambda i, c: (0, 0)),
            _layer_spec((1, dv), l, lambda i, c: (0, 0)),
        ],
        out_specs=[
            pl.BlockSpec((None, Lb, dv), lambda i, c: (i, c, 0)),
            pl.BlockSpec((None, G_HEADS, G_DK, G_DV), lambda i, c: (i, 0, 0, 0)),
            pl.BlockSpec((None, CONV_W - 1, G_CONV_CH), lambda i, c: (i, 0, 0)),
        ],
        out_shape=[
            jax.ShapeDtypeStruct((b, t, dv), BF16),
            jax.ShapeDtypeStruct((b, G_HEADS, G_DK, G_DV), F32),
            jax.ShapeDtypeStruct((b, CONV_W - 1, G_CONV_CH), F32),
        ],
        scratch_shapes=[
            pltpu.VMEM((G_HEADS, G_DK, G_DV), F32),
            pltpu.VMEM((SUBLANES, G_CONV_CH), F32),
        ],
        compiler_params=_cparams("parallel", "arbitrary"),
        name="gdn_prompt",
    )(proj3, proj3, proj3, gate_bias, alog, conv_w, g_norm)


def _gdn_sample_kernel(pr_ref, gb_ref, al_ref, cw_ref, gn_ref, s_ref, cb_ref, acc_ref,
                       og_ref, so_ref, co_ref):
    del acc_ref
    nb = cb_ref.shape[2]
    log_a, beta = _gdn_gates(_token_rows(pr_ref, nb, GATE_ROW) + gb_ref[...], al_ref[...])
    eg8 = jnp.exp(log_a)

    ys = []
    for r in range(CONV_ROWS):
        xr = _token_rows(pr_ref, nb, ROW_GX + r)
        y = xr * cw_ref[CONV_W - 1, r:r + 1, :]
        for j in range(CONV_W - 1):
            y = y + cb_ref[j, r] * cw_ref[j, r:r + 1, :]
        co_ref[0, r] = cb_ref[1, r]
        co_ref[1, r] = cb_ref[2, r]
        co_ref[2, r] = xr
        ys.append(y * _sigmoid(y))

    outs = []
    for h in range(G_HEADS):
        qc, kc, v = ys[h], ys[G_HEADS + h], ys[2 * G_HEADS + h]
        q = qc * lax.rsqrt(jnp.sum(qc * qc, axis=1, keepdims=True) + EPS) * (G_DK ** -0.5)
        k = kc * lax.rsqrt(jnp.sum(kc * kc, axis=1, keepdims=True) + EPS)
        eg = eg8[:, GATE_A + h:GATE_A + h + 1]
        bt = beta[:, GATE_B + h:GATE_B + h + 1]
        qb, kb = q.astype(BF16), k.astype(BF16)
        ks, qs = [], []
        for s in range(nb):
            sb = s_ref[s, h].astype(BF16)
            ks.append(_dot(kb, sb))
            qs.append(_dot(qb, sb))
        u = bt * (v - eg * _pick_row(ks))
        o = eg * _pick_row(qs) + jnp.sum(q * k, axis=1, keepdims=True) * u
        k_cols = _columns([k])
        eg_rows = jnp.broadcast_to(eg, (nb, G_DV))
        for s in range(nb):
            so_ref[s, h] = eg_rows[s:s + 1, :] * s_ref[s, h] + k_cols[:, s:s + 1] * u[s:s + 1, :]
        gz = _token_rows(pr_ref, nb, ROW_GZ + h)
        outs.append(_rms(o, gn_ref[:, h * G_DV:(h + 1) * G_DV]) * (gz * _sigmoid(gz)))
    og_ref[...] = jnp.concatenate(outs, axis=1).astype(BF16)


def _gdn_sample(proj_rows, gate_bias, alog, conv_w_rows, g_norm, s_all, conv_t, s_acc, l):
    depth, b = s_all.shape[:2]
    nb = SAMPLE_BLOCK
    dv = G_HEADS * G_DV
    s_spec = _layer_spec((nb, G_HEADS, G_DK, G_DV), l, lambda i: (i, 0, 0, 0))
    return pl.pallas_call(
        _gdn_sample_kernel,
        grid=(b // nb,),
        in_specs=[
            pl.BlockSpec((nb * ROWS_PER_TOKEN, LANES), lambda i: (i, 0)),
            _layer_spec((1, LANES), l, lambda i: (0, 0)),
            _layer_spec((1, LANES), l, lambda i: (0, 0)),
            _layer_spec((CONV_W, CONV_ROWS, LANES), l, lambda i: (0, 0, 0)),
            _layer_spec((1, dv), l, lambda i: (0, 0)),
            s_spec,
            _layer_spec((CONV_W - 1, CONV_ROWS, nb, LANES), l, lambda i: (0, 0, i, 0)),
            pl.BlockSpec(memory_space=pl.ANY),
        ],
        out_specs=[
            pl.BlockSpec((nb, dv), lambda i: (i, 0)),
            s_spec,
            pl.BlockSpec((CONV_W - 1, CONV_ROWS, nb, LANES), lambda i: (0, 0, i, 0)),
        ],
        out_shape=[
            jax.ShapeDtypeStruct((b, dv), BF16),
            jax.ShapeDtypeStruct(s_all.shape, F32),
            jax.ShapeDtypeStruct((CONV_W - 1, CONV_ROWS, b, LANES), F32),
        ],
        input_output_aliases={7: 1},
        compiler_params=_cparams("parallel"),
        name="gdn_sample",
    )(proj_rows, gate_bias, alog, conv_w_rows, g_norm, s_all, conv_t, s_acc)


def _arrange_w_in_kernel(wt_ref, o_ref):
    n_m = 2 * M_HEADS * M_DK + 2 * M_HEADS * M_DV
    n_g = 4 * G_HEADS * G_DK
    g0 = n_m + 2 * M_HEADS
    tk = wt_ref.shape[1]
    step = 2 * LANES
    for dst, src, n in ((0, 0, n_m), (n_m, g0, n_g)):
        for c in range(0, n, step):
            o_ref[:, dst + c:dst + c + step] = wt_ref[src + c:src + c + step, :].T.astype(BF16)
    n_gate = wt_ref.shape[0] - N_WIDE
    gate = jnp.concatenate([wt_ref[n_m:g0, :], wt_ref[g0 + n_g:, :], jnp.zeros((LANES - n_gate, tk), F32)], axis=0)
    o_ref[:, N_WIDE:N_WIDE + LANES] = gate.T.astype(BF16)
    o_ref[:, N_WIDE + LANES:] = jnp.zeros((tk, N_PROJ - N_WIDE - LANES), BF16)


def _arrange_w_in(w_in, tk=256):
    depth, rows, n_in = w_in.shape
    return pl.pallas_call(
        _arrange_w_in_kernel,
        grid=(depth, rows // tk),
        in_specs=[pl.BlockSpec((None, n_in, tk), lambda l, i: (l, 0, i))],
        out_specs=pl.BlockSpec((None, tk, N_PROJ), lambda l, i: (l, i, 0)),
        out_shape=jax.ShapeDtypeStruct((depth, rows, N_PROJ), BF16),
        compiler_params=_cparams("parallel", "parallel"),
        name="arrange_w_in",
    )(jnp.swapaxes(w_in, 1, 2))


def _lane_row(parts, depth):
    row = jnp.zeros((depth, 1, LANES), F32)
    for off, arr in parts:
        row = row.at[:, 0, off:off + arr.shape[-1]].set(arr.astype(F32))
    return row


def kernel(x_prompt, x_sample, mem_prompt, state_mlstm_C, state_mlstm_n, state_mlstm_m, state_gdn_S,
           state_gdn_conv, cache_mem_k, cache_mem_v, norm_mix, w_in, mlstm_b_i, mlstm_b_f, mlstm_norm,
           gdn_conv_w, gdn_A_log, gdn_dt_bias, gdn_norm, w_out, norm_xattn, norm_mem, xattn_wq, xattn_wk,
           xattn_wv, xattn_wo, norm_ffn, ffn_w_gate, ffn_w_up, ffn_w_down, norm_final):
    bp, seq, _ = x_prompt.shape
    bs = x_sample.shape[0]
    depth = w_in.shape[0]
    dq = X_HEADS * X_DH
    tm_p, tm_s = 512, bs

    w_in_b = _arrange_w_in(w_in)
    w_out_b = w_out.astype(BF16)
    wq_b, wk_b, wv_b, wo_b = (w.astype(BF16) for w in (xattn_wq, xattn_wk, xattn_wv, xattn_wo))
    wg_b, wu_b, wd_b = ffn_w_gate.astype(BF16), ffn_w_up.astype(BF16), ffn_w_down.astype(BF16)
    gate_bias = _lane_row([(GATE_I, mlstm_b_i), (GATE_F, mlstm_b_f), (GATE_A, gdn_dt_bias)], depth)
    alog = _lane_row([(GATE_A, gdn_A_log)], depth)
    g_mix, g_x, g_mem, g_ffn = (g.reshape(depth, 1, D_MODEL) for g in (norm_mix, norm_xattn, norm_mem, norm_ffn))
    g_final = norm_final.reshape(1, D_MODEL)
    m_norm = mlstm_norm.reshape(depth, 1, M_HEADS * M_DV)
    g_norm = gdn_norm.reshape(depth, 1, G_HEADS * G_DV)
    conv_w_rows = gdn_conv_w.reshape(depth, CONV_W, CONV_ROWS, LANES)
    conv_t = jnp.transpose(state_gdn_conv.reshape(depth, bs, CONV_W - 1, CONV_ROWS, LANES), (0, 2, 3, 1, 4))
    n_all = state_mlstm_n.reshape(depth, bs, M_HEADS * M_DK)
    mem_k = cache_mem_k.reshape(depth, bs, N_MEM * X_HEADS, X_DH)
    mem_v = cache_mem_v.reshape(depth, bs, N_MEM * X_HEADS, X_DH)

    mk_all, mv_all = _mem_kv(mem_prompt.reshape(bp * N_MEM, D_MODEL), g_mem, wk_b, wv_b)

    xp = x_prompt.reshape(bp * seq, D_MODEL)
    xs = x_sample.reshape(bs, D_MODEL)
    c_acc = jnp.zeros(state_mlstm_C.shape, F32)
    s_acc = jnp.zeros(state_gdn_S.shape, F32)
    outs = [[] for _ in range(8)]
    for l in range(depth):
        proj3 = _proj(xp, g_mix, w_in_b, l, 2 * tm_p).reshape(bp, seq, N_PROJ)
        hm, c_p, n_p, m_p = _mlstm_prompt(proj3, gate_bias, m_norm, l)
        og, s_p, cv_p = _gdn_prompt(proj3, gate_bias, alog, gdn_conv_w, g_norm, l)
        xp = _mix_out(xp, hm.reshape(bp * seq, -1), og.reshape(bp * seq, -1), w_out_b, l, tm_p)
        xp = _xattn_prompt(xp, g_x, wq_b, mk_all, mv_all, wo_b, l, tm_p, seq)
        xp = _ffn(xp, g_ffn, wg_b, wu_b, wd_b, g_final, l, 2 * tm_p, l == depth - 1)
        for lst, val in zip(outs[:5], (c_p, n_p.reshape(bp, M_HEADS, M_DK), m_p[:, 0, :M_HEADS], s_p, cv_p)):
            lst.append(val)

        proj_rows = _proj(xs, g_mix, w_in_b, l, tm_s).reshape(bs * ROWS_PER_TOKEN, LANES)
        hm, c_acc, n_s, m_s = _mlstm_sample(proj_rows, gate_bias, m_norm, state_mlstm_C, n_all, state_mlstm_m,
                                            c_acc, l)
        og, s_acc, cv_s = _gdn_sample(proj_rows, gate_bias, alog, conv_w_rows, g_norm, state_gdn_S, conv_t,
                                      s_acc, l)
        xs = _mix_out(xs, hm, og, w_out_b, l, tm_s)
        xs = _xattn_sample(xs, g_x, wq_b, mem_k, mem_v, wo_b, l)
        xs = _ffn(xs, g_ffn, wg_b, wu_b, wd_b, g_final, l, tm_s, l == depth - 1)
        cv_s = jnp.transpose(cv_s, (2, 0, 1, 3)).reshape(bs, CONV_W - 1, G_CONV_CH)
        for lst, val in zip(outs[5:], (n_s.reshape(bs, M_HEADS, M_DK), m_s, cv_s)):
            lst.append(val)

    y_prompt = xp.reshape(bp, seq, D_MODEL)
    y_sample = xs.reshape(bs, 1, D_MODEL)
    pc, pn, pm, ps, pconv, sn, sm, sconv = (jnp.stack(o) for o in outs)
    mem_shape = (depth, bp, N_MEM, X_HEADS, X_DH)
    return (y_prompt, y_sample, pc, pn, pm, ps, pconv, mk_all.reshape(mem_shape), mv_all.reshape(mem_shape),
            c_acc, sn, sm, s_acc, sconv)
```

```python
import functools
import math

import jax
import jax.numpy as jnp
from jax import lax
from jax.experimental import pallas as pl
from jax.experimental.pallas import tpu as pltpu

F32 = jnp.float32
BF16 = jnp.bfloat16

D_MODEL = 2048
M_HEADS = 4
M_DK = 128
M_DV = 256
G_HEADS = 8
G_DK = 128
G_DV = 128
CONV_W = 4
G_CONV_CH = G_HEADS * (2 * G_DK + G_DV)
X_HEADS = 4
X_DH = 128
N_MEM = 256
D_FF = 5632
GATE_CAP = 15.0
EPS = 1e-6

LANES = 128
SUBLANES = 8
N_WIDE = 2 * M_HEADS * M_DK + 2 * M_HEADS * M_DV + 4 * G_HEADS * G_DK
PROJ_TN = 1280
N_PROJ = -(-(N_WIDE + LANES) // PROJ_TN) * PROJ_TN
ROWS_PER_TOKEN = N_PROJ // LANES
GATE_ROW = N_WIDE // LANES
GATE_I, GATE_F, GATE_A, GATE_B = 0, 4, 8, 16
ROW_MQ, ROW_MK, ROW_MV, ROW_MO = 0, 4, 8, 16
ROW_GX, ROW_GZ = 24, 48
CONV_ROWS = G_CONV_CH // LANES

VMEM_LIMIT = 56 * 1024 * 1024

M_CHUNK = 256
G_BLOCK = 256
G_CHUNK = 64
G_PAIR = 2 * G_CHUNK
SAMPLE_BLOCK = 8

NT_DIMS = (((1,), (1,)), ((), ()))
TN_DIMS = (((0,), (0,)), ((), ()))


def _cparams(*sem):
    return pltpu.CompilerParams(dimension_semantics=sem, vmem_limit_bytes=VMEM_LIMIT)


def _dot(a, b):
    return jnp.dot(a, b, preferred_element_type=F32)


def _dot_nt(a, b):
    return lax.dot_general(a, b, NT_DIMS, preferred_element_type=F32)


def _dot_tn(a, b):
    return lax.dot_general(a, b, TN_DIMS, preferred_element_type=F32)


def _dot_f32(a, b):
    return jnp.dot(a, b, preferred_element_type=F32, precision=lax.Precision.HIGHEST)


def _rms(x, g):
    ms = jnp.mean(x * x, axis=-1, keepdims=True)
    return x * lax.rsqrt(ms + EPS) * g


def _sigmoid(x):
    return 1.0 / (1.0 + jnp.exp(-x))


def _softplus(x):
    return jnp.maximum(x, 0.0) + jnp.log1p(jnp.exp(-jnp.abs(x)))


def _softcap(x):
    return GATE_CAP * jnp.tanh(x / GATE_CAP)


def _layer_spec(shape, l, tail):
    return pl.BlockSpec((None,) + shape, lambda *g: (l,) + tail(*g))


def _sample_share(m_prompt, m_sample, tm):
    tiles = m_prompt // tm
    rs = m_sample // tiles
    assert tiles * tm == m_prompt and tiles * rs == m_sample and rs % SUBLANES == 0, (m_prompt, m_sample, tm)
    return tiles, rs


def _proj_kernel(xp_ref, xs_ref, g_ref, w_ref, op_ref, os_ref, h_scr):
    tm = xp_ref.shape[0]

    @pl.when(pl.program_id(1) == 0)
    def _():
        h_scr[:tm] = _rms(xp_ref[...], g_ref[...]).astype(BF16)
        h_scr[tm:] = _rms(xs_ref[...], g_ref[...]).astype(BF16)

    r = _dot(h_scr[...], w_ref[...])
    op_ref[...] = r[:tm]
    os_ref[...] = r[tm:]


def _proj(xp, xs, g, w, l, tm):
    mp, ms = xp.shape[0], xs.shape[0]
    tiles, rs = _sample_share(mp, ms, tm)
    return pl.pallas_call(
        _proj_kernel,
        grid=(tiles, N_PROJ // PROJ_TN),
        in_specs=[
            pl.BlockSpec((tm, D_MODEL), lambda i, j: (i, 0)),
            pl.BlockSpec((rs, D_MODEL), lambda i, j: (i, 0)),
            _layer_spec((1, D_MODEL), l, lambda i, j: (0, 0)),
            _layer_spec((D_MODEL, PROJ_TN), l, lambda i, j: (0, j)),
        ],
        out_specs=[pl.BlockSpec((tm, PROJ_TN), lambda i, j: (i, j)), pl.BlockSpec((rs, PROJ_TN), lambda i, j: (i, j))],
        out_shape=[jax.ShapeDtypeStruct((mp, N_PROJ), F32), jax.ShapeDtypeStruct((ms, N_PROJ), F32)],
        scratch_shapes=[pltpu.VMEM((tm + rs, D_MODEL), BF16)],
        compiler_params=_cparams("parallel", "arbitrary"),
        name="proj",
    )(xp, xs, g, w)


def _mem_kv_kernel(x_ref, g_ref, wk_ref, wv_ref, k_ref, v_ref):
    h = _rms(x_ref[...], g_ref[...]).astype(BF16)
    k_ref[...] = _dot(h, wk_ref[...])
    v_ref[...] = _dot(h, wv_ref[...])


def _mem_kv(mem, g, wk, wv):
    depth = wk.shape[0]
    m = mem.shape[0]
    dq = X_HEADS * X_DH
    tm = N_MEM
    w_spec = pl.BlockSpec((None, D_MODEL, dq), lambda l, i: (l, 0, 0))
    o_spec = pl.BlockSpec((None, tm, dq), lambda l, i: (l, i, 0))
    return pl.pallas_call(
        _mem_kv_kernel,
        grid=(depth, m // tm),
        in_specs=[
            pl.BlockSpec((tm, D_MODEL), lambda l, i: (i, 0)),
            pl.BlockSpec((None, 1, D_MODEL), lambda l, i: (l, 0, 0)),
            w_spec, w_spec,
        ],
        out_specs=[o_spec, o_spec],
        out_shape=[jax.ShapeDtypeStruct((depth, m, dq), F32)] * 2,
        compiler_params=_cparams("parallel", "parallel"),
        name="mem_kv",
    )(mem, g, wk, wv)


def _mix_out_kernel(xp_ref, xs_ref, hmp_ref, ogp_ref, hms_ref, ogs_ref, w1_ref, w2_ref, op_ref, os_ref):
    tm = xp_ref.shape[0]
    hm = jnp.concatenate([hmp_ref[...], hms_ref[...]], axis=0)
    og = jnp.concatenate([ogp_ref[...], ogs_ref[...]], axis=0)
    acc = _dot(hm, w1_ref[...])
    acc = acc + _dot(og, w2_ref[...])
    op_ref[...] = xp_ref[...] + acc[:tm]
    os_ref[...] = xs_ref[...] + acc[tm:]


def _mix_out(xp, xs, hmp, ogp, hms, ogs, w_out, l, tm):
    mp, ms = xp.shape[0], xs.shape[0]
    tiles, rs = _sample_share(mp, ms, tm)
    half = D_MODEL // 2
    return pl.pallas_call(
        _mix_out_kernel,
        grid=(tiles,),
        in_specs=[
            pl.BlockSpec((tm, D_MODEL), lambda i: (i, 0)),
            pl.BlockSpec((rs, D_MODEL), lambda i: (i, 0)),
            pl.BlockSpec((tm, half), lambda i: (i, 0)),
            pl.BlockSpec((tm, half), lambda i: (i, 0)),
            pl.BlockSpec((rs, half), lambda i: (i, 0)),
            pl.BlockSpec((rs, half), lambda i: (i, 0)),
            _layer_spec((half, D_MODEL), l, lambda i: (0, 0)),
            _layer_spec((half, D_MODEL), l, lambda i: (1, 0)),
        ],
        out_specs=[pl.BlockSpec((tm, D_MODEL), lambda i: (i, 0)), pl.BlockSpec((rs, D_MODEL), lambda i: (i, 0))],
        out_shape=[jax.ShapeDtypeStruct((mp, D_MODEL), F32), jax.ShapeDtypeStruct((ms, D_MODEL), F32)],
        compiler_params=_cparams("parallel"),
        name="mix_out",
    )(xp, xs, hmp, ogp, hms, ogs, w_out, w_out)


def _ffn_kernel(xp_ref, xs_ref, g_ref, wg_ref, wu_ref, wd_ref, gf_ref, op_ref, os_ref, h_scr, *, final):
    j = pl.program_id(1)
    tm = xp_ref.shape[0]

    @pl.when(j == 0)
    def _():
        for x_ref, o_ref, rows in ((xp_ref, op_ref, slice(0, tm)), (xs_ref, os_ref, slice(tm, None))):
            x = x_ref[...]
            h_scr[rows] = _rms(x, g_ref[...]).astype(BF16)
            o_ref[...] = x

    h = h_scr[...]
    a = _dot(h, wg_ref[...])
    b = _dot(h, wu_ref[...])
    t = (a * _sigmoid(a)) * b
    r = _dot(t.astype(BF16), wd_ref[...])
    op_ref[...] += r[:tm]
    os_ref[...] += r[tm:]

    if final:
        @pl.when(j == pl.num_programs(1) - 1)
        def _():
            op_ref[...] = _rms(op_ref[...], gf_ref[...])
            os_ref[...] = _rms(os_ref[...], gf_ref[...])


def _ffn(xp, xs, g, w_gate, w_up, w_down, g_final, l, tm, final, tf=512):
    mp, ms = xp.shape[0], xs.shape[0]
    tiles, rs = _sample_share(mp, ms, tm)
    return pl.pallas_call(
        functools.partial(_ffn_kernel, final=final),
        grid=(tiles, D_FF // tf),
        in_specs=[
            pl.BlockSpec((tm, D_MODEL), lambda i, j: (i, 0)),
            pl.BlockSpec((rs, D_MODEL), lambda i, j: (i, 0)),
            _layer_spec((1, D_MODEL), l, lambda i, j: (0, 0)),
            _layer_spec((D_MODEL, tf), l, lambda i, j: (0, j)),
            _layer_spec((D_MODEL, tf), l, lambda i, j: (0, j)),
            _layer_spec((tf, D_MODEL), l, lambda i, j: (j, 0)),
            pl.BlockSpec((1, D_MODEL), lambda i, j: (0, 0)),
        ],
        out_specs=[pl.BlockSpec((tm, D_MODEL), lambda i, j: (i, 0)), pl.BlockSpec((rs, D_MODEL), lambda i, j: (i, 0))],
        out_shape=[jax.ShapeDtypeStruct((mp, D_MODEL), F32), jax.ShapeDtypeStruct((ms, D_MODEL), F32)],
        scratch_shapes=[pltpu.VMEM((tm + rs, D_MODEL), BF16)],
        compiler_params=_cparams("parallel", "arbitrary"),
        name="ffn",
    )(xp, xs, g, w_gate, w_up, w_down, g_final)


def _softmax_rows(s):
    s = s - jnp.max(s, axis=-1, keepdims=True)
    p = jnp.exp(s)
    return p / jnp.sum(p, axis=-1, keepdims=True)


def _prompt_attention(q, k_ref, v_ref):
    heads = []
    for h in range(X_HEADS):
        sl = slice(h * X_DH, (h + 1) * X_DH)
        s = _dot_nt(q[:, sl].astype(BF16), k_ref[:, sl].astype(BF16)) * (X_DH ** -0.5)
        p = _softmax_rows(s)
        heads.append(_dot(p.astype(BF16), v_ref[:, sl].astype(BF16)))
    return jnp.concatenate(heads, axis=1)


def _sample_attention(q, k_ref, v_ref):
    nb = q.shape[0]
    nk = N_MEM * X_HEADS
    qh = [q[:, h * X_DH:(h + 1) * X_DH] for h in range(X_HEADS)]
    row = lax.broadcasted_iota(jnp.int32, (SUBLANES, nk), 0)
    col = lax.broadcasted_iota(jnp.int32, (SUBLANES, nk), 1)
    own = (col & (X_HEADS - 1)) == (row & (X_HEADS - 1))
    row8 = lax.broadcasted_iota(jnp.int32, (SUBLANES, X_DH), 0)
    outs = [jnp.zeros((nb, X_DH), F32) for _ in range(X_HEADS)]
    scores = []
    for s in range(nb):
        q8 = jnp.zeros((SUBLANES, X_DH), F32)
        for h in range(X_HEADS):
            q8 = jnp.where(row8 == h, qh[h][s:s + 1, :], q8)
        scores.append(_dot_nt(q8.astype(BF16), k_ref[s].astype(BF16)) * (X_DH ** -0.5))
    probs = [_softmax_rows(jnp.where(own, sc, -jnp.inf)).astype(BF16) for sc in scores]
    o8s = [_dot(probs[s], v_ref[s].astype(BF16)) for s in range(nb)]
    for s in range(nb):
        for h in range(X_HEADS):
            outs[h] = jnp.where(row8[:nb] == s, o8s[s][h:h + 1, :], outs[h])
    return jnp.concatenate(outs, axis=1)


def _xattn_kernel(xp_ref, xs_ref, g_ref, wq_ref, kp_ref, vp_ref, ks_ref, vs_ref, wo_ref, op_ref, os_ref):
    tm = xp_ref.shape[0]
    h = jnp.concatenate([_rms(xp_ref[...], g_ref[...]), _rms(xs_ref[...], g_ref[...])], axis=0)
    q = _dot(h.astype(BF16), wq_ref[...])
    o = jnp.concatenate([_prompt_attention(q[:tm], kp_ref, vp_ref), _sample_attention(q[tm:], ks_ref, vs_ref)],
                        axis=0)
    r = _dot(o.astype(BF16), wo_ref[...])
    op_ref[...] = xp_ref[...] + r[:tm]
    os_ref[...] = xs_ref[...] + r[tm:]


def _xattn(xp, xs, g, wq, mk_p, mv_p, mk_s, mv_s, wo, l, tm, seq):
    mp, ms = xp.shape[0], xs.shape[0]
    tiles, rs = _sample_share(mp, ms, tm)
    dq = X_HEADS * X_DH
    per_seq = seq // tm
    kvp_spec = pl.BlockSpec((None, N_MEM, dq), lambda i: (l, i // per_seq, 0))
    kvs_spec = _layer_spec((rs, N_MEM * X_HEADS, X_DH), l, lambda i: (i, 0, 0))
    return pl.pallas_call(
        _xattn_kernel,
        grid=(tiles,),
        in_specs=[
            pl.BlockSpec((tm, D_MODEL), lambda i: (i, 0)),
            pl.BlockSpec((rs, D_MODEL), lambda i: (i, 0)),
            _layer_spec((1, D_MODEL), l, lambda i: (0, 0)),
            _layer_spec((D_MODEL, dq), l, lambda i: (0, 0)),
            kvp_spec, kvp_spec, kvs_spec, kvs_spec,
            _layer_spec((dq, D_MODEL), l, lambda i: (0, 0)),
        ],
        out_specs=[pl.BlockSpec((tm, D_MODEL), lambda i: (i, 0)), pl.BlockSpec((rs, D_MODEL), lambda i: (i, 0))],
        out_shape=[jax.ShapeDtypeStruct((mp, D_MODEL), F32), jax.ShapeDtypeStruct((ms, D_MODEL), F32)],
        compiler_params=_cparams("parallel"),
        name="xattn",
    )(xp, xs, g, wq, mk_p, mv_p, mk_s, mv_s, wo)


def _mlstm_prompt_kernel(q_ref, k_ref, v_ref, mo_ref, gt_ref, gb_ref, mn_ref,
                         hm_ref, c_ref, n_ref, m_ref, c_scr, n_scr, m_scr):
    L = q_ref.shape[0]
    c = pl.program_id(1)

    @pl.when(c == 0)
    def _():
        c_scr[...] = jnp.zeros_like(c_scr)
        n_scr[...] = jnp.zeros_like(n_scr)
        m_scr[...] = jnp.zeros_like(m_scr)

    gates = gt_ref[...] + gb_ref[...]
    capped = _softcap(gates)
    logf = -_softplus(-capped)
    row = lax.broadcasted_iota(jnp.int32, (L, L), 0)
    col = lax.broadcasted_iota(jnp.int32, (L, L), 1)
    causal = row >= col
    bcum = _dot_f32(causal.astype(F32), logf)
    lane = lax.broadcasted_iota(jnp.int32, (L, LANES), 1)
    z = jnp.where(lane < GATE_F, capped, bcum)
    zt = z.T

    hs = range(M_HEADS)
    i_col = [z[:, GATE_I + h:GATE_I + h + 1] for h in hs]
    b_col = [z[:, GATE_F + h:GATE_F + h + 1] for h in hs]
    m_prev = [m_scr[0:1, h:h + 1] for h in hs]
    d = [jnp.where(causal, b_col[h] + (zt[GATE_I + h:GATE_I + h + 1, :] - zt[GATE_F + h:GATE_F + h + 1, :]),
                   -jnp.inf) for h in hs]
    inter = [b_col[h] + m_prev[h] for h in hs]
    m_t = [jnp.maximum(inter[h], jnp.max(d[h], axis=1, keepdims=True)) for h in hs]
    q = [q_ref[:, h * M_DK:(h + 1) * M_DK] for h in hs]
    k = [k_ref[:, h * M_DK:(h + 1) * M_DK] * (M_DK ** -0.5) for h in hs]
    qb = [q[h].astype(BF16) for h in hs]
    vb = [v_ref[:, h * M_DV:(h + 1) * M_DV].astype(BF16) for h in hs]
    qk = [_dot_nt(qb[h], k[h].astype(BF16)) for h in hs]
    qc = [_dot(qb[h], c_scr[h].astype(BF16)) for h in hs]
    w = [jnp.exp(d[h] - m_t[h]) * qk[h] for h in hs]
    a = [jnp.exp(inter[h] - m_t[h]) for h in hs]
    wv = [_dot(w[h].astype(BF16), vb[h]) for h in hs]
    nq = [a[h] * jnp.sum(q[h] * n_scr[h], axis=1, keepdims=True) + jnp.sum(w[h], axis=1, keepdims=True) for h in hs]
    hh = [(a[h] * qc[h] + wv[h]) / jnp.maximum(jnp.abs(nq[h]), jnp.exp(-m_t[h])) for h in hs]
    for h in hs:
        y = _rms(hh[h], mn_ref[:, h * M_DV:(h + 1) * M_DV])
        y = y * _sigmoid(mo_ref[:, h * M_DV:(h + 1) * M_DV])
        hm_ref[:, h * M_DV:(h + 1) * M_DV] = y.astype(BF16)

    m_new = [m_t[h][L - 1:L, :] for h in hs]
    b_last = [b_col[h][L - 1:L, :] for h in hs]
    kw = [k[h] * jnp.exp(b_last[h] - b_col[h] + i_col[h] - m_new[h]) for h in hs]
    upd = [_dot_tn(kw[h].astype(BF16), vb[h]) for h in hs]
    for h in hs:
        decay = jnp.exp(b_last[h] + m_prev[h] - m_new[h])
        c_scr[h] = decay * c_scr[h] + upd[h]
        n_scr[h] = decay * n_scr[h] + jnp.sum(kw[h], axis=0, keepdims=True)
        m_scr[0:1, h:h + 1] = m_new[h]

    @pl.when(c == pl.num_programs(1) - 1)
    def _():
        c_ref[...] = c_scr[...]
        n_ref[...] = n_scr[...]
        m_ref[...] = m_scr[...]


def _mlstm_prompt(proj3, gate_bias, m_norm, l):
    b, t, _ = proj3.shape
    L = M_CHUNK
    dqk = M_HEADS * M_DK
    dv = M_HEADS * M_DV
    return pl.pallas_call(
        _mlstm_prompt_kernel,
        grid=(b, t // L),
        in_specs=[
            pl.BlockSpec((None, L, dqk), lambda i, c: (i, c, 0)),
            pl.BlockSpec((None, L, dqk), lambda i, c: (i, c, 1)),
            pl.BlockSpec((None, L, dv), lambda i, c: (i, c, 1)),
            pl.BlockSpec((None, L, dv), lambda i, c: (i, c, 2)),
            pl.BlockSpec((None, L, LANES), lambda i, c: (i, c, GATE_ROW)),
            _layer_spec((1, LANES), l, lambda i, c: (0, 0)),
            _layer_spec((1, dv), l, lambda i, c: (0, 0)),
        ],
        out_specs=[
            pl.BlockSpec((None, L, dv), lambda i, c: (i, c, 0)),
            pl.BlockSpec((None, M_HEADS, M_DK, M_DV), lambda i, c: (i, 0, 0, 0)),
            pl.BlockSpec((None, M_HEADS, 1, M_DK), lambda i, c: (i, 0, 0, 0)),
            pl.BlockSpec((None, 1, LANES), lambda i, c: (i, 0, 0)),
        ],
        out_shape=[
            jax.ShapeDtypeStruct((b, t, dv), BF16),
            jax.ShapeDtypeStruct((b, M_HEADS, M_DK, M_DV), F32),
            jax.ShapeDtypeStruct((b, M_HEADS, 1, M_DK), F32),
            jax.ShapeDtypeStruct((b, 1, LANES), F32),
        ],
        scratch_shapes=[
            pltpu.VMEM((M_HEADS, M_DK, M_DV), F32),
            pltpu.VMEM((M_HEADS, 1, M_DK), F32),
            pltpu.VMEM((1, LANES), F32),
        ],
        compiler_params=_cparams("parallel", "arbitrary"),
        name="mlstm_prompt",
    )(proj3, proj3, proj3, proj3, proj3, gate_bias, m_norm)


def _token_rows(pr_ref, nb, off):
    return pr_ref[pl.ds(off, nb, stride=ROWS_PER_TOKEN), :]


def _token_rows2(pr_ref, nb, off):
    return jnp.concatenate([_token_rows(pr_ref, nb, off), _token_rows(pr_ref, nb, off + 1)], axis=1)


def _pick_row(tiles):
    rid = lax.broadcasted_iota(jnp.int32, tiles[0].shape, 0)
    out = tiles[0]
    for s in range(1, len(tiles)):
        out = jnp.where(rid == s, tiles[s], out)
    return out


def _columns(tiles):
    pad = LANES - SUBLANES * len(tiles)
    stack = jnp.concatenate(tiles + ([jnp.zeros((pad, LANES), F32)] if pad else []), axis=0)
    return stack.T


def _mlstm_sample_kernel(pr_ref, gb_ref, mn_ref, c_ref, n_ref, m_ref, acc_ref,
                         hm_ref, co_ref, no_ref, mo_ref):
    del acc_ref
    nb = m_ref.shape[0]
    gates = _token_rows(pr_ref, nb, GATE_ROW) + gb_ref[...]
    capped = _softcap(gates)
    logf = -_softplus(-capped)
    i4 = capped[:, GATE_I:GATE_I + M_HEADS]
    f4 = logf[:, GATE_F:GATE_F + M_HEADS]
    inter = f4 + m_ref[...]
    m_new = jnp.maximum(inter, i4)
    a4 = jnp.exp(inter - m_new)
    w4 = jnp.exp(i4 - m_new)
    e4 = jnp.exp(-m_new)
    mo_ref[...] = m_new

    scale = M_DK ** -0.5
    ys, n_rows = [], []
    for h in range(M_HEADS):
        a, wi, e = a4[:, h:h + 1], w4[:, h:h + 1], e4[:, h:h + 1]
        q = _token_rows(pr_ref, nb, ROW_MQ + h)
        k = _token_rows(pr_ref, nb, ROW_MK + h) * scale
        v = _token_rows2(pr_ref, nb, ROW_MV + 2 * h)
        og = _token_rows2(pr_ref, nb, ROW_MO + 2 * h)
        n_old = n_ref[:, h * M_DK:(h + 1) * M_DK]
        wk = wi * k
        n_rows.append(a * n_old + wk)
        wts = wi * jnp.sum(q * k, axis=1, keepdims=True)
        nq = a * jnp.sum(q * n_old, axis=1, keepdims=True) + wts
        wk_cols = _columns([wk])
        a_rows = jnp.broadcast_to(a, (nb, M_DV))
        qb = q.astype(BF16)
        qc = []
        for s in range(nb):
            c_old = c_ref[s, h]
            qc.append(_dot(qb, c_old.astype(BF16)))
            co_ref[s, h] = a_rows[s:s + 1, :] * c_old + wk_cols[:, s:s + 1] * v[s:s + 1, :]
        num = a * _pick_row(qc) + wts * v
        hh = num / jnp.maximum(jnp.abs(nq), e)
        ys.append(_rms(hh, mn_ref[:, h * M_DV:(h + 1) * M_DV]) * _sigmoid(og))
    hm_ref[...] = jnp.concatenate(ys, axis=1).astype(BF16)
    no_ref[...] = jnp.concatenate(n_rows, axis=1)


def _mlstm_sample(proj_rows, gate_bias, m_norm, c_all, n_all, m_all, c_acc, l):
    depth, b = m_all.shape[:2]
    nb = SAMPLE_BLOCK
    dv = M_HEADS * M_DV
    dk = M_HEADS * M_DK
    c_spec = _layer_spec((nb, M_HEADS, M_DK, M_DV), l, lambda i: (i, 0, 0, 0))
    return pl.pallas_call(
        _mlstm_sample_kernel,
        grid=(b // nb,),
        in_specs=[
            pl.BlockSpec((nb * ROWS_PER_TOKEN, LANES), lambda i: (i, 0)),
            _layer_spec((1, LANES), l, lambda i: (0, 0)),
            _layer_spec((1, dv), l, lambda i: (0, 0)),
            c_spec,
            _layer_spec((nb, dk), l, lambda i: (i, 0)),
            _layer_spec((nb, M_HEADS), l, lambda i: (i, 0)),
            pl.BlockSpec(memory_space=pl.ANY),
        ],
        out_specs=[
            pl.BlockSpec((nb, dv), lambda i: (i, 0)),
            c_spec,
            pl.BlockSpec((nb, dk), lambda i: (i, 0)),
            pl.BlockSpec((nb, M_HEADS), lambda i: (i, 0)),
        ],
        out_shape=[
            jax.ShapeDtypeStruct((b, dv), BF16),
            jax.ShapeDtypeStruct(c_all.shape, F32),
            jax.ShapeDtypeStruct((b, dk), F32),
            jax.ShapeDtypeStruct((b, M_HEADS), F32),
        ],
        input_output_aliases={6: 1},
        compiler_params=_cparams("parallel"),
        name="mlstm_sample",
    )(proj_rows, gate_bias, m_norm, c_all, n_all, m_all, c_acc)


def _gdn_gates(gates, alog):
    log_a = -jnp.exp(alog) * _softplus(gates)
    beta = _sigmoid(gates)
    return log_a, beta


def _gdn_prompt_kernel(x_ref, gz_ref, gt_ref, gb_ref, al_ref, cw_ref, gn_ref,
                       og_ref, s_ref, cv_ref, s_scr, tail_scr):
    Lb = x_ref.shape[0]
    L, P = G_CHUNK, G_PAIR
    n_pair = Lb // P
    c = pl.program_id(1)

    @pl.when(c == 0)
    def _():
        s_scr[...] = jnp.zeros_like(s_scr)
        tail_scr[...] = jnp.zeros_like(tail_scr)

    log_a, beta = _gdn_gates(gt_ref[...] + gb_ref[...], al_ref[...])
    row = lax.broadcasted_iota(jnp.int32, (Lb, Lb), 0)
    col = lax.broadcasted_iota(jnp.int32, (Lb, Lb), 1)
    shift = int(math.log2(L))
    same_chunk = jnp.right_shift(row, shift) == jnp.right_shift(col, shift)
    gcum = _dot_f32(jnp.where(same_chunk & (row >= col), 1.0, 0.0), log_a)
    gtot = _dot_f32(jnp.where(same_chunk, 1.0, 0.0), log_a)
    lane = lax.broadcasted_iota(jnp.int32, (Lb, LANES), 1)
    zt = jnp.where(lane < GATE_B, gcum, beta).T

    rp = lax.broadcasted_iota(jnp.int32, (P, P), 0)
    cp = lax.broadcasted_iota(jnp.int32, (P, P), 1)
    same = jnp.right_shift(rp, shift) == jnp.right_shift(cp, shift)
    incl = same & (rp >= cp)
    strict = same & (rp > cp)

    def conv_cols(c0):
        xc = x_ref[:, c0:c0 + LANES]
        xp = jnp.concatenate([tail_scr[:, c0:c0 + LANES], xc], axis=0)
        y = xc * cw_ref[3:4, c0:c0 + LANES]
        for j in range(CONV_W - 1):
            y = y + xp[5 + j:5 + j + Lb, :] * cw_ref[j:j + 1, c0:c0 + LANES]
        return y * _sigmoid(y)

    heads = []
    for h in range(G_HEADS):
        qc = conv_cols(h * G_DK)
        kc = conv_cols(G_HEADS * G_DK + h * G_DK)
        vc = conv_cols(2 * G_HEADS * G_DK + h * G_DV)
        q2 = qc * (lax.rsqrt(jnp.sum(qc * qc, axis=1, keepdims=True) + EPS) * (G_DK ** -0.5))
        k2 = kc * lax.rsqrt(jnp.sum(kc * kc, axis=1, keepdims=True) + EPS)
        g_col = gcum[:, GATE_A + h:GATE_A + h + 1]
        t_col = gtot[:, GATE_A + h:GATE_A + h + 1]
        b_col = beta[:, GATE_B + h:GATE_B + h + 1]
        eg = jnp.exp(g_col)
        heads.append(dict(
            qb=q2.astype(BF16), kb=k2.astype(BF16), eg=eg, b_col=b_col,
            vek=jnp.concatenate([vc, eg * k2], axis=1),
            kd=(k2 * jnp.exp(t_col - g_col)).astype(BF16),
            dS=jnp.exp(t_col),
        ))

    blocks = []
    for h in range(G_HEADS):
        hd = heads[h]
        for b in range(n_pair):
            r0 = b * P
            kb = hd["kb"][r0:r0 + P]
            g_col = gcum[r0:r0 + P, GATE_A + h:GATE_A + h + 1]
            g_row = zt[GATE_A + h:GATE_A + h + 1, r0:r0 + P]
            b_row = zt[GATE_B + h:GATE_B + h + 1, r0:r0 + P]
            dec = jnp.exp(jnp.where(incl, g_col - g_row, -jnp.inf))
            a_mat = jnp.where(strict, dec * _dot_nt(kb, kb) * b_row, 0.0)
            qkd = (_dot_nt(hd["qb"][r0:r0 + P], kb) * dec).astype(BF16)
            blocks.append(dict(h=h, r0=r0, mpow=-a_mat, qacc=-a_mat, qkd=qkd))

    for blk in blocks:
        blk["mb"] = blk["mpow"].astype(BF16)
    for _ in range(int(math.log2(L)) - 1):
        for blk in blocks:
            blk["mpow"] = _dot(blk["mb"], blk["mb"])
            blk["mb"] = blk["mpow"].astype(BF16)
        for blk in blocks:
            blk["qacc"] = blk["qacc"] + blk["mpow"] + _dot(blk["qacc"].astype(BF16), blk["mb"])

    for blk in blocks:
        hd = heads[blk["h"]]
        r0 = blk["r0"]
        vek = hd["vek"][r0:r0 + P]
        t = (vek + _dot(blk["qacc"].astype(BF16), vek.astype(BF16))) * hd["b_col"][r0:r0 + P]
        blk["uv"] = t[:, :G_DV]
        blk["kw"] = t[:, G_DV:].astype(BF16)

    states = [s_scr[h] for h in range(G_HEADS)]
    wus = [[] for _ in range(G_HEADS)]
    qss = [[] for _ in range(G_HEADS)]
    for ci in range(Lb // L):
        r0 = ci * L
        for h in range(G_HEADS):
            hd = heads[h]
            blk = blocks[h * n_pair + r0 // P]
            o0 = r0 - blk["r0"]
            lhs = jnp.concatenate([blk["kw"][o0:o0 + L], hd["qb"][r0:r0 + L]], axis=0)
            r = _dot(lhs, states[h].astype(BF16))
            wu = (blk["uv"][o0:o0 + L] - r[:L]).astype(BF16)
            qss[h].append(r[L:])
            wus[h].append(wu)
            states[h] = hd["dS"][r0:r0 + 1] * states[h] + _dot_tn(hd["kd"][r0:r0 + L], wu)
    for h in range(G_HEADS):
        s_scr[h] = states[h]

    for h in range(G_HEADS):
        hd = heads[h]
        for b in range(n_pair):
            r0 = b * P
            blk = blocks[h * n_pair + b]
            i0 = r0 // L
            wu = jnp.concatenate(wus[h][i0:i0 + P // L], axis=0)
            qs = jnp.concatenate(qss[h][i0:i0 + P // L], axis=0)
            o = hd["eg"][r0:r0 + P] * qs + _dot(blk["qkd"], wu)
            y = _rms(o, gn_ref[:, h * G_DV:(h + 1) * G_DV])
            gz = gz_ref[r0:r0 + P, h * G_DV:(h + 1) * G_DV]
            og_ref[r0:r0 + P, h * G_DV:(h + 1) * G_DV] = (y * (gz * _sigmoid(gz))).astype(BF16)

    tail_scr[...] = x_ref[Lb - SUBLANES:Lb, :]

    @pl.when(c == pl.num_programs(1) - 1)
    def _():
        s_ref[...] = s_scr[...]
        cv_ref[...] = x_ref[Lb - (CONV_W - 1):Lb, :]


def _gdn_prompt(proj3, gate_bias, alog, conv_w, g_norm, l):
    b, t, _ = proj3.shape
    Lb = G_BLOCK
    dv = G_HEADS * G_DV
    return pl.pallas_call(
        _gdn_prompt_kernel,
        grid=(b, t // Lb),
        in_specs=[
            pl.BlockSpec((None, Lb, G_CONV_CH), lambda i, c: (i, c, 1)),
            pl.BlockSpec((None, Lb, dv), lambda i, c: (i, c, 6)),
            pl.BlockSpec((None, Lb, LANES), lambda i, c: (i, c, GATE_ROW)),
            _layer_spec((1, LANES), l, lambda i, c: (0, 0)),
            _layer_spec((1, LANES), l, lambda i, c: (0, 0)),
            _layer_spec((CONV_W, G_CONV_CH), l, lambda i, c: (0, 0)),
            _layer_spec((1, dv), l, lambda i, c: (0, 0)),
        ],
        out_specs=[
            pl.BlockSpec((None, Lb, dv), lambda i, c: (i, c, 0)),
            pl.BlockSpec((None, G_HEADS, G_DK, G_DV), lambda i, c: (i, 0, 0, 0)),
            pl.BlockSpec((None, CONV_W - 1, G_CONV_CH), lambda i, c: (i, 0, 0)),
        ],
        out_shape=[
            jax.ShapeDtypeStruct((b, t, dv), BF16),
            jax.ShapeDtypeStruct((b, G_HEADS, G_DK, G_DV), F32),
            jax.ShapeDtypeStruct((b, CONV_W - 1, G_CONV_CH), F32),
        ],
        scratch_shapes=[
            pltpu.VMEM((G_HEADS, G_DK, G_DV), F32),
            pltpu.VMEM((SUBLANES, G_CONV_CH), F32),
        ],
        compiler_params=_cparams("parallel", "arbitrary"),
        name="gdn_prompt",
    )(proj3, proj3, proj3, gate_bias, alog, conv_w, g_norm)


def _gdn_sample_kernel(pr_ref, gb_ref, al_ref, cw_ref, gn_ref, s_ref, cb_ref, acc_ref,
                       og_ref, so_ref, co_ref):
    del acc_ref
    nb = cb_ref.shape[2]
    log_a, beta = _gdn_gates(_token_rows(pr_ref, nb, GATE_ROW) + gb_ref[...], al_ref[...])
    eg8 = jnp.exp(log_a)

    ys = []
    for r in range(CONV_ROWS):
        xr = _token_rows(pr_ref, nb, ROW_GX + r)
        y = xr * cw_ref[CONV_W - 1, r:r + 1, :]
        for j in range(CONV_W - 1):
            y = y + cb_ref[j, r] * cw_ref[j, r:r + 1, :]
        co_ref[0, r] = cb_ref[1, r]
        co_ref[1, r] = cb_ref[2, r]
        co_ref[2, r] = xr
        ys.append(y * _sigmoid(y))

    outs = []
    for h in range(G_HEADS):
        qc, kc, v = ys[h], ys[G_HEADS + h], ys[2 * G_HEADS + h]
        q = qc * lax.rsqrt(jnp.sum(qc * qc, axis=1, keepdims=True) + EPS) * (G_DK ** -0.5)
        k = kc * lax.rsqrt(jnp.sum(kc * kc, axis=1, keepdims=True) + EPS)
        eg = eg8[:, GATE_A + h:GATE_A + h + 1]
        bt = beta[:, GATE_B + h:GATE_B + h + 1]
        qb, kb = q.astype(BF16), k.astype(BF16)
        ks, qs = [], []
        for s in range(nb):
            sb = s_ref[s, h].astype(BF16)
            ks.append(_dot(kb, sb))
            qs.append(_dot(qb, sb))
        u = bt * (v - eg * _pick_row(ks))
        o = eg * _pick_row(qs) + jnp.sum(q * k, axis=1, keepdims=True) * u
        k_cols = _columns([k])
        eg_rows = jnp.broadcast_to(eg, (nb, G_DV))
        for s in range(nb):
            so_ref[s, h] = eg_rows[s:s + 1, :] * s_ref[s, h] + k_cols[:, s:s + 1] * u[s:s + 1, :]
        gz = _token_rows(pr_ref, nb, ROW_GZ + h)
        outs.append(_rms(o, gn_ref[:, h * G_DV:(h + 1) * G_DV]) * (gz * _sigmoid(gz)))
    og_ref[...] = jnp.concatenate(outs, axis=1).astype(BF16)


def _gdn_sample(proj_rows, gate_bias, alog, conv_w_rows, g_norm, s_all, conv_t, s_acc, l):
    depth, b = s_all.shape[:2]
    nb = SAMPLE_BLOCK
    dv = G_HEADS * G_DV
    s_spec = _layer_spec((nb, G_HEADS, G_DK, G_DV), l, lambda i: (i, 0, 0, 0))
    return pl.pallas_call(
        _gdn_sample_kernel,
        grid=(b // nb,),
        in_specs=[
            pl.BlockSpec((nb * ROWS_PER_TOKEN, LANES), lambda i: (i, 0)),
            _layer_spec((1, LANES), l, lambda i: (0, 0)),
            _layer_spec((1, LANES), l, lambda i: (0, 0)),
            _layer_spec((CONV_W, CONV_ROWS, LANES), l, lambda i: (0, 0, 0)),
            _layer_spec((1, dv), l, lambda i: (0, 0)),
            s_spec,
            _layer_spec((CONV_W - 1, CONV_ROWS, nb, LANES), l, lambda i: (0, 0, i, 0)),
            pl.BlockSpec(memory_space=pl.ANY),
        ],
        out_specs=[
            pl.BlockSpec((nb, dv), lambda i: (i, 0)),
            s_spec,
            pl.BlockSpec((CONV_W - 1, CONV_ROWS, nb, LANES), lambda i: (0, 0, i, 0)),
        ],
        out_shape=[
            jax.ShapeDtypeStruct((b, dv), BF16),
            jax.ShapeDtypeStruct(s_all.shape, F32),
            jax.ShapeDtypeStruct((CONV_W - 1, CONV_ROWS, b, LANES), F32),
        ],
        input_output_aliases={7: 1},
        compiler_params=_cparams("parallel"),
        name="gdn_sample",
    )(proj_rows, gate_bias, alog, conv_w_rows, g_norm, s_all, conv_t, s_acc)


def _arrange_w_in_kernel(wt_ref, o_ref):
    n_m = 2 * M_HEADS * M_DK + 2 * M_HEADS * M_DV
    n_g = 4 * G_HEADS * G_DK
    g0 = n_m + 2 * M_HEADS
    tk = wt_ref.shape[1]
    step = 2 * LANES
    for dst, src, n in ((0, 0, n_m), (n_m, g0, n_g)):
        for c in range(0, n, step):
            o_ref[:, dst + c:dst + c + step] = wt_ref[src + c:src + c + step, :].T.astype(BF16)
    n_gate = wt_ref.shape[0] - N_WIDE
    gate = jnp.concatenate([wt_ref[n_m:g0, :], wt_ref[g0 + n_g:, :], jnp.zeros((LANES - n_gate, tk), F32)], axis=0)
    o_ref[:, N_WIDE:N_WIDE + LANES] = gate.T.astype(BF16)
    o_ref[:, N_WIDE + LANES:] = jnp.zeros((tk, N_PROJ - N_WIDE - LANES), BF16)


def _arrange_w_in(w_in, tk=256):
    depth, rows, n_in = w_in.shape
    return pl.pallas_call(
        _arrange_w_in_kernel,
        grid=(depth, rows // tk),
        in_specs=[pl.BlockSpec((None, n_in, tk), lambda l, i: (l, 0, i))],
        out_specs=pl.BlockSpec((None, tk, N_PROJ), lambda l, i: (l, i, 0)),
        out_shape=jax.ShapeDtypeStruct((depth, rows, N_PROJ), BF16),
        compiler_params=_cparams("parallel", "parallel"),
        name="arrange_w_in",
    )(jnp.swapaxes(w_in, 1, 2))


def _lane_row(parts, depth):
    row = jnp.zeros((depth, 1, LANES), F32)
    for off, arr in parts:
        row = row.at[:, 0, off:off + arr.shape[-1]].set(arr.astype(F32))
    return row


def kernel(x_prompt, x_sample, mem_prompt, state_mlstm_C, state_mlstm_n, state_mlstm_m, state_gdn_S,
           state_gdn_conv, cache_mem_k, cache_mem_v, norm_mix, w_in, mlstm_b_i, mlstm_b_f, mlstm_norm,
           gdn_conv_w, gdn_A_log, gdn_dt_bias, gdn_norm, w_out, norm_xattn, norm_mem, xattn_wq, xattn_wk,
           xattn_wv, xattn_wo, norm_ffn, ffn_w_gate, ffn_w_up, ffn_w_down, norm_final):
    bp, seq, _ = x_prompt.shape
    bs = x_sample.shape[0]
    depth = w_in.shape[0]
    dq = X_HEADS * X_DH
    tm_p, tm_wide = 512, 1024

    w_in_b = _arrange_w_in(w_in)
    w_out_b = w_out.astype(BF16)
    wq_b, wk_b, wv_b, wo_b = (w.astype(BF16) for w in (xattn_wq, xattn_wk, xattn_wv, xattn_wo))
    wg_b, wu_b, wd_b = ffn_w_gate.astype(BF16), ffn_w_up.astype(BF16), ffn_w_down.astype(BF16)
    gate_bias = _lane_row([(GATE_I, mlstm_b_i), (GATE_F, mlstm_b_f), (GATE_A, gdn_dt_bias)], depth)
    alog = _lane_row([(GATE_A, gdn_A_log)], depth)
    g_mix, g_x, g_mem, g_ffn = (g.reshape(depth, 1, D_MODEL) for g in (norm_mix, norm_xattn, norm_mem, norm_ffn))
    g_final = norm_final.reshape(1, D_MODEL)
    m_norm = mlstm_norm.reshape(depth, 1, M_HEADS * M_DV)
    g_norm = gdn_norm.reshape(depth, 1, G_HEADS * G_DV)
    conv_w_rows = gdn_conv_w.reshape(depth, CONV_W, CONV_ROWS, LANES)
    conv_t = jnp.transpose(state_gdn_conv.reshape(depth, bs, CONV_W - 1, CONV_ROWS, LANES), (0, 2, 3, 1, 4))
    n_all = state_mlstm_n.reshape(depth, bs, M_HEADS * M_DK)
    mem_k = cache_mem_k.reshape(depth, bs, N_MEM * X_HEADS, X_DH)
    mem_v = cache_mem_v.reshape(depth, bs, N_MEM * X_HEADS, X_DH)

    mk_all, mv_all = _mem_kv(mem_prompt.reshape(bp * N_MEM, D_MODEL), g_mem, wk_b, wv_b)

    xp = x_prompt.reshape(bp * seq, D_MODEL)
    xs = x_sample.reshape(bs, D_MODEL)
    c_acc = jnp.zeros(state_mlstm_C.shape, F32)
    s_acc = jnp.zeros(state_gdn_S.shape, F32)
    outs = [[] for _ in range(8)]
    for l in range(depth):
        proj_p, proj_s = _proj(xp, xs, g_mix, w_in_b, l, tm_wide)
        proj3 = proj_p.reshape(bp, seq, N_PROJ)
        hm_p, c_p, n_p, m_p = _mlstm_prompt(proj3, gate_bias, m_norm, l)
        og_p, s_p, cv_p = _gdn_prompt(proj3, gate_bias, alog, gdn_conv_w, g_norm, l)
        for lst, val in zip(outs[:5], (c_p, n_p.reshape(bp, M_HEADS, M_DK), m_p[:, 0, :M_HEADS], s_p, cv_p)):
            lst.append(val)
        proj_rows = proj_s.reshape(bs * ROWS_PER_TOKEN, LANES)
        hm_s, c_acc, n_s, m_s = _mlstm_sample(proj_rows, gate_bias, m_norm, state_mlstm_C, n_all, state_mlstm_m,
                                              c_acc, l)
        og_s, s_acc, cv_s = _gdn_sample(proj_rows, gate_bias, alog, conv_w_rows, g_norm, state_gdn_S, conv_t,
                                        s_acc, l)
        xp, xs = _mix_out(xp, xs, hm_p.reshape(bp * seq, -1), og_p.reshape(bp * seq, -1), hm_s, og_s, w_out_b, l, tm_p)
        xp, xs = _xattn(xp, xs, g_x, wq_b, mk_all, mv_all, mem_k, mem_v, wo_b, l, tm_p, seq)
        xp, xs = _ffn(xp, xs, g_ffn, wg_b, wu_b, wd_b, g_final, l, tm_wide, l == depth - 1)
        cv_s = jnp.transpose(cv_s, (2, 0, 1, 3)).reshape(bs, CONV_W - 1, G_CONV_CH)
        for lst, val in zip(outs[5:], (n_s.reshape(bs, M_HEADS, M_DK), m_s, cv_s)):
            lst.append(val)

    y_prompt = xp.reshape(bp, seq, D_MODEL)
    y_sample = xs.reshape(bs, 1, D_MODEL)
    pc, pn, pm, ps, pconv, sn, sm, sconv = (jnp.stack(o) for o in outs)
    mem_shape = (depth, bp, N_MEM, X_HEADS, X_DH)
    return (y_prompt, y_sample, pc, pn, pm, ps, pconv, mk_all.reshape(mem_shape), mv_all.reshape(mem_shape),
            c_acc, sn, sm, s_acc, sconv)
```

```python
import functools
import math

import jax
import jax.numpy as jnp
from jax import lax
from jax.experimental import pallas as pl
from jax.experimental.pallas import tpu as pltpu

F32 = jnp.float32
BF16 = jnp.bfloat16

D_MODEL = 2048
M_HEADS = 4
M_DK = 128
M_DV = 256
G_HEADS = 8
G_DK = 128
G_DV = 128
CONV_W = 4
G_CONV_CH = G_HEADS * (2 * G_DK + G_DV)
X_HEADS = 4
X_DH = 128
N_MEM = 256
D_FF = 5632
GATE_CAP = 15.0
EPS = 1e-6

LANES = 128
SUBLANES = 8
N_WIDE = 2 * M_HEADS * M_DK + 2 * M_HEADS * M_DV + 4 * G_HEADS * G_DK
PROJ_TN = 1280
N_PROJ = -(-(N_WIDE + LANES) // PROJ_TN) * PROJ_TN
ROWS_PER_TOKEN = N_PROJ // LANES
GATE_ROW = N_WIDE // LANES
GATE_I, GATE_F, GATE_A, GATE_B = 0, 4, 8, 16
ROW_MQ, ROW_MK, ROW_MV, ROW_MO = 0, 4, 8, 16
ROW_GX, ROW_GZ = 24, 48
CONV_ROWS = G_CONV_CH // LANES

VMEM_LIMIT = 56 * 1024 * 1024

M_CHUNK = 256
G_BLOCK = 256
G_CHUNK = 64
G_PAIR = 2 * G_CHUNK
SAMPLE_BLOCK = 8

NT_DIMS = (((1,), (1,)), ((), ()))
TN_DIMS = (((0,), (0,)), ((), ()))


def _cparams(*sem):
    return pltpu.CompilerParams(dimension_semantics=sem, vmem_limit_bytes=VMEM_LIMIT)


def _dot(a, b):
    return jnp.dot(a, b, preferred_element_type=F32)


def _dot_nt(a, b):
    return lax.dot_general(a, b, NT_DIMS, preferred_element_type=F32)


def _dot_tn(a, b):
    return lax.dot_general(a, b, TN_DIMS, preferred_element_type=F32)


def _dot_f32(a, b):
    return jnp.dot(a, b, preferred_element_type=F32, precision=lax.Precision.HIGHEST)


def _rms(x, g):
    ms = jnp.mean(x * x, axis=-1, keepdims=True)
    return x * lax.rsqrt(ms + EPS) * g


def _sigmoid(x):
    return 1.0 / (1.0 + jnp.exp(-x))


def _softplus(x):
    return jnp.maximum(x, 0.0) + jnp.log1p(jnp.exp(-jnp.abs(x)))


def _softcap(x):
    return GATE_CAP * jnp.tanh(x / GATE_CAP)


def _layer_spec(shape, l, tail):
    return pl.BlockSpec((None,) + shape, lambda *g: (l,) + tail(*g))


def _sample_share(m_prompt, m_sample, tm):
    tiles = m_prompt // tm
    rs = m_sample // tiles
    assert tiles * tm == m_prompt and tiles * rs == m_sample and rs % SUBLANES == 0, (m_prompt, m_sample, tm)
    return tiles, rs


def _proj_kernel(xp_ref, xs_ref, g_ref, w_ref, op_ref, os_ref, h_scr):
    tm = xp_ref.shape[0]

    @pl.when(pl.program_id(1) == 0)
    def _():
        h_scr[:tm] = _rms(xp_ref[...], g_ref[...]).astype(BF16)
        h_scr[tm:] = _rms(xs_ref[...], g_ref[...]).astype(BF16)

    r = _dot(h_scr[...], w_ref[...])
    op_ref[...] = r[:tm]
    os_ref[...] = r[tm:]


def _proj(xp, xs, g, w, l, tm):
    mp, ms = xp.shape[0], xs.shape[0]
    tiles, rs = _sample_share(mp, ms, tm)
    return pl.pallas_call(
        _proj_kernel,
        grid=(tiles, N_PROJ // PROJ_TN),
        in_specs=[
            pl.BlockSpec((tm, D_MODEL), lambda i, j: (i, 0)),
            pl.BlockSpec((rs, D_MODEL), lambda i, j: (i, 0)),
            _layer_spec((1, D_MODEL), l, lambda i, j: (0, 0)),
            _layer_spec((D_MODEL, PROJ_TN), l, lambda i, j: (0, j)),
        ],
        out_specs=[pl.BlockSpec((tm, PROJ_TN), lambda i, j: (i, j)), pl.BlockSpec((rs, PROJ_TN), lambda i, j: (i, j))],
        out_shape=[jax.ShapeDtypeStruct((mp, N_PROJ), F32), jax.ShapeDtypeStruct((ms, N_PROJ), F32)],
        scratch_shapes=[pltpu.VMEM((tm + rs, D_MODEL), BF16)],
        compiler_params=_cparams("parallel", "arbitrary"),
        name="proj",
    )(xp, xs, g, w)


def _mem_kv_kernel(x_ref, g_ref, wk_ref, wv_ref, k_ref, v_ref):
    h = _rms(x_ref[...], g_ref[...]).astype(BF16)
    k_ref[...] = _dot(h, wk_ref[...].astype(BF16))
    v_ref[...] = _dot(h, wv_ref[...].astype(BF16))


def _mem_kv(mem, g, wk, wv):
    depth = wk.shape[0]
    m = mem.shape[0]
    dq = X_HEADS * X_DH
    tm = N_MEM
    w_spec = pl.BlockSpec((None, D_MODEL, dq), lambda l, i: (l, 0, 0))
    o_spec = pl.BlockSpec((None, tm, dq), lambda l, i: (l, i, 0))
    return pl.pallas_call(
        _mem_kv_kernel,
        grid=(depth, m // tm),
        in_specs=[
            pl.BlockSpec((tm, D_MODEL), lambda l, i: (i, 0)),
            pl.BlockSpec((None, 1, D_MODEL), lambda l, i: (l, 0, 0)),
            w_spec, w_spec,
        ],
        out_specs=[o_spec, o_spec],
        out_shape=[jax.ShapeDtypeStruct((depth, m, dq), F32)] * 2,
        compiler_params=_cparams("parallel", "parallel"),
        name="mem_kv",
    )(mem, g, wk, wv)


def _mix_out_kernel(xp_ref, xs_ref, hmp_ref, ogp_ref, hms_ref, ogs_ref, w1_ref, w2_ref, op_ref, os_ref):
    tm = xp_ref.shape[0]
    hm = jnp.concatenate([hmp_ref[...], hms_ref[...]], axis=0)
    og = jnp.concatenate([ogp_ref[...], ogs_ref[...]], axis=0)
    acc = _dot(hm, w1_ref[...].astype(BF16))
    acc = acc + _dot(og, w2_ref[...].astype(BF16))
    op_ref[...] = xp_ref[...] + acc[:tm]
    os_ref[...] = xs_ref[...] + acc[tm:]


def _mix_out(xp, xs, hmp, ogp, hms, ogs, w_out, l, tm):
    mp, ms = xp.shape[0], xs.shape[0]
    tiles, rs = _sample_share(mp, ms, tm)
    half = D_MODEL // 2
    return pl.pallas_call(
        _mix_out_kernel,
        grid=(tiles,),
        in_specs=[
            pl.BlockSpec((tm, D_MODEL), lambda i: (i, 0)),
            pl.BlockSpec((rs, D_MODEL), lambda i: (i, 0)),
            pl.BlockSpec((tm, half), lambda i: (i, 0)),
            pl.BlockSpec((tm, half), lambda i: (i, 0)),
            pl.BlockSpec((rs, half), lambda i: (i, 0)),
            pl.BlockSpec((rs, half), lambda i: (i, 0)),
            _layer_spec((half, D_MODEL), l, lambda i: (0, 0)),
            _layer_spec((half, D_MODEL), l, lambda i: (1, 0)),
        ],
        out_specs=[pl.BlockSpec((tm, D_MODEL), lambda i: (i, 0)), pl.BlockSpec((rs, D_MODEL), lambda i: (i, 0))],
        out_shape=[jax.ShapeDtypeStruct((mp, D_MODEL), F32), jax.ShapeDtypeStruct((ms, D_MODEL), F32)],
        compiler_params=_cparams("parallel"),
        name="mix_out",
    )(xp, xs, hmp, ogp, hms, ogs, w_out, w_out)


def _ffn_kernel(xp_ref, xs_ref, g_ref, wg_ref, wu_ref, wd_ref, gf_ref, op_ref, os_ref, h_scr, *, final):
    j = pl.program_id(1)
    tm = xp_ref.shape[0]

    @pl.when(j == 0)
    def _():
        for x_ref, o_ref, rows in ((xp_ref, op_ref, slice(0, tm)), (xs_ref, os_ref, slice(tm, None))):
            x = x_ref[...]
            h_scr[rows] = _rms(x, g_ref[...]).astype(BF16)
            o_ref[...] = x

    h = h_scr[...]
    a = _dot(h, wg_ref[...].astype(BF16))
    b = _dot(h, wu_ref[...].astype(BF16))
    t = (a * _sigmoid(a)) * b
    r = _dot(t.astype(BF16), wd_ref[...].astype(BF16))
    op_ref[...] += r[:tm]
    os_ref[...] += r[tm:]

    if final:
        @pl.when(j == pl.num_programs(1) - 1)
        def _():
            op_ref[...] = _rms(op_ref[...], gf_ref[...])
            os_ref[...] = _rms(os_ref[...], gf_ref[...])


def _ffn(xp, xs, g, w_gate, w_up, w_down, g_final, l, tm, final, tf=256):
    mp, ms = xp.shape[0], xs.shape[0]
    tiles, rs = _sample_share(mp, ms, tm)
    return pl.pallas_call(
        functools.partial(_ffn_kernel, final=final),
        grid=(tiles, D_FF // tf),
        in_specs=[
            pl.BlockSpec((tm, D_MODEL), lambda i, j: (i, 0)),
            pl.BlockSpec((rs, D_MODEL), lambda i, j: (i, 0)),
            _layer_spec((1, D_MODEL), l, lambda i, j: (0, 0)),
            _layer_spec((D_MODEL, tf), l, lambda i, j: (0, j)),
            _layer_spec((D_MODEL, tf), l, lambda i, j: (0, j)),
            _layer_spec((tf, D_MODEL), l, lambda i, j: (j, 0)),
            pl.BlockSpec((1, D_MODEL), lambda i, j: (0, 0)),
        ],
        out_specs=[pl.BlockSpec((tm, D_MODEL), lambda i, j: (i, 0)), pl.BlockSpec((rs, D_MODEL), lambda i, j: (i, 0))],
        out_shape=[jax.ShapeDtypeStruct((mp, D_MODEL), F32), jax.ShapeDtypeStruct((ms, D_MODEL), F32)],
        scratch_shapes=[pltpu.VMEM((tm + rs, D_MODEL), BF16)],
        compiler_params=_cparams("parallel", "arbitrary"),
        name="ffn",
    )(xp, xs, g, w_gate, w_up, w_down, g_final)


def _softmax_rows(s):
    s = s - jnp.max(s, axis=-1, keepdims=True)
    p = jnp.exp(s)
    return p / jnp.sum(p, axis=-1, keepdims=True)


def _prompt_attention(q, k_ref, v_ref):
    heads = []
    for h in range(X_HEADS):
        sl = slice(h * X_DH, (h + 1) * X_DH)
        s = _dot_nt(q[:, sl].astype(BF16), k_ref[:, sl].astype(BF16)) * (X_DH ** -0.5)
        p = _softmax_rows(s)
        heads.append(_dot(p.astype(BF16), v_ref[:, sl].astype(BF16)))
    return jnp.concatenate(heads, axis=1)


def _sample_attention(q, k_ref, v_ref):
    nb = q.shape[0]
    nk = N_MEM * X_HEADS
    qh = [q[:, h * X_DH:(h + 1) * X_DH] for h in range(X_HEADS)]
    row = lax.broadcasted_iota(jnp.int32, (SUBLANES, nk), 0)
    col = lax.broadcasted_iota(jnp.int32, (SUBLANES, nk), 1)
    own = (col & (X_HEADS - 1)) == (row & (X_HEADS - 1))
    row8 = lax.broadcasted_iota(jnp.int32, (SUBLANES, X_DH), 0)
    outs = [jnp.zeros((nb, X_DH), F32) for _ in range(X_HEADS)]
    scores = []
    for s in range(nb):
        q8 = jnp.zeros((SUBLANES, X_DH), F32)
        for h in range(X_HEADS):
            q8 = jnp.where(row8 == h, qh[h][s:s + 1, :], q8)
        scores.append(_dot_nt(q8.astype(BF16), k_ref[s].astype(BF16)) * (X_DH ** -0.5))
    probs = [_softmax_rows(jnp.where(own, sc, -jnp.inf)).astype(BF16) for sc in scores]
    o8s = [_dot(probs[s], v_ref[s].astype(BF16)) for s in range(nb)]
    for s in range(nb):
        for h in range(X_HEADS):
            outs[h] = jnp.where(row8[:nb] == s, o8s[s][h:h + 1, :], outs[h])
    return jnp.concatenate(outs, axis=1)


def _xattn_kernel(xp_ref, xs_ref, g_ref, wq_ref, kp_ref, vp_ref, ks_ref, vs_ref, wo_ref, op_ref, os_ref):
    tm = xp_ref.shape[0]
    h = jnp.concatenate([_rms(xp_ref[...], g_ref[...]), _rms(xs_ref[...], g_ref[...])], axis=0)
    q = _dot(h.astype(BF16), wq_ref[...].astype(BF16))
    o = jnp.concatenate([_prompt_attention(q[:tm], kp_ref, vp_ref), _sample_attention(q[tm:], ks_ref, vs_ref)],
                        axis=0)
    r = _dot(o.astype(BF16), wo_ref[...].astype(BF16))
    op_ref[...] = xp_ref[...] + r[:tm]
    os_ref[...] = xs_ref[...] + r[tm:]


def _xattn(xp, xs, g, wq, mk_p, mv_p, mk_s, mv_s, wo, l, tm, seq):
    mp, ms = xp.shape[0], xs.shape[0]
    tiles, rs = _sample_share(mp, ms, tm)
    dq = X_HEADS * X_DH
    per_seq = seq // tm
    kvp_spec = pl.BlockSpec((None, N_MEM, dq), lambda i: (l, i // per_seq, 0))
    kvs_spec = _layer_spec((rs, N_MEM * X_HEADS, X_DH), l, lambda i: (i, 0, 0))
    return pl.pallas_call(
        _xattn_kernel,
        grid=(tiles,),
        in_specs=[
            pl.BlockSpec((tm, D_MODEL), lambda i: (i, 0)),
            pl.BlockSpec((rs, D_MODEL), lambda i: (i, 0)),
            _layer_spec((1, D_MODEL), l, lambda i: (0, 0)),
            _layer_spec((D_MODEL, dq), l, lambda i: (0, 0)),
            kvp_spec, kvp_spec, kvs_spec, kvs_spec,
            _layer_spec((dq, D_MODEL), l, lambda i: (0, 0)),
        ],
        out_specs=[pl.BlockSpec((tm, D_MODEL), lambda i: (i, 0)), pl.BlockSpec((rs, D_MODEL), lambda i: (i, 0))],
        out_shape=[jax.ShapeDtypeStruct((mp, D_MODEL), F32), jax.ShapeDtypeStruct((ms, D_MODEL), F32)],
        compiler_params=_cparams("parallel"),
        name="xattn",
    )(xp, xs, g, wq, mk_p, mv_p, mk_s, mv_s, wo)


def _mlstm_prompt_kernel(q_ref, k_ref, v_ref, mo_ref, gt_ref, gb_ref, mn_ref,
                         hm_ref, c_ref, n_ref, m_ref, c_scr, n_scr, m_scr):
    L = q_ref.shape[0]
    c = pl.program_id(1)

    @pl.when(c == 0)
    def _():
        c_scr[...] = jnp.zeros_like(c_scr)
        n_scr[...] = jnp.zeros_like(n_scr)
        m_scr[...] = jnp.zeros_like(m_scr)

    gates = gt_ref[...] + gb_ref[...]
    capped = _softcap(gates)
    logf = -_softplus(-capped)
    row = lax.broadcasted_iota(jnp.int32, (L, L), 0)
    col = lax.broadcasted_iota(jnp.int32, (L, L), 1)
    causal = row >= col
    bcum = _dot_f32(causal.astype(F32), logf)
    lane = lax.broadcasted_iota(jnp.int32, (L, LANES), 1)
    z = jnp.where(lane < GATE_F, capped, bcum)
    zt = z.T

    hs = range(M_HEADS)
    i_col = [z[:, GATE_I + h:GATE_I + h + 1] for h in hs]
    b_col = [z[:, GATE_F + h:GATE_F + h + 1] for h in hs]
    m_prev = [m_scr[0:1, h:h + 1] for h in hs]
    d = [jnp.where(causal, b_col[h] + (zt[GATE_I + h:GATE_I + h + 1, :] - zt[GATE_F + h:GATE_F + h + 1, :]),
                   -jnp.inf) for h in hs]
    inter = [b_col[h] + m_prev[h] for h in hs]
    m_t = [jnp.maximum(inter[h], jnp.max(d[h], axis=1, keepdims=True)) for h in hs]
    q = [q_ref[:, h * M_DK:(h + 1) * M_DK] for h in hs]
    k = [k_ref[:, h * M_DK:(h + 1) * M_DK] * (M_DK ** -0.5) for h in hs]
    qb = [q[h].astype(BF16) for h in hs]
    vb = [v_ref[:, h * M_DV:(h + 1) * M_DV].astype(BF16) for h in hs]
    qk = [_dot_nt(qb[h], k[h].astype(BF16)) for h in hs]
    qc = [_dot(qb[h], c_scr[h].astype(BF16)) for h in hs]
    w = [jnp.exp(d[h] - m_t[h]) * qk[h] for h in hs]
    a = [jnp.exp(inter[h] - m_t[h]) for h in hs]
    wv = [_dot(w[h].astype(BF16), vb[h]) for h in hs]
    nq = [a[h] * jnp.sum(q[h] * n_scr[h], axis=1, keepdims=True) + jnp.sum(w[h], axis=1, keepdims=True) for h in hs]
    hh = [(a[h] * qc[h] + wv[h]) / jnp.maximum(jnp.abs(nq[h]), jnp.exp(-m_t[h])) for h in hs]
    for h in hs:
        y = _rms(hh[h], mn_ref[:, h * M_DV:(h + 1) * M_DV])
        y = y * _sigmoid(mo_ref[:, h * M_DV:(h + 1) * M_DV])
        hm_ref[:, h * M_DV:(h + 1) * M_DV] = y.astype(BF16)

    m_new = [m_t[h][L - 1:L, :] for h in hs]
    b_last = [b_col[h][L - 1:L, :] for h in hs]
    kw = [k[h] * jnp.exp(b_last[h] - b_col[h] + i_col[h] - m_new[h]) for h in hs]
    upd = [_dot_tn(kw[h].astype(BF16), vb[h]) for h in hs]
    for h in hs:
        decay = jnp.exp(b_last[h] + m_prev[h] - m_new[h])
        c_scr[h] = decay * c_scr[h] + upd[h]
        n_scr[h] = decay * n_scr[h] + jnp.sum(kw[h], axis=0, keepdims=True)
        m_scr[0:1, h:h + 1] = m_new[h]

    @pl.when(c == pl.num_programs(1) - 1)
    def _():
        c_ref[...] = c_scr[...]
        n_ref[...] = n_scr[...]
        m_ref[...] = m_scr[...]


def _mlstm_prompt(proj3, gate_bias, m_norm, l):
    b, t, _ = proj3.shape
    L = M_CHUNK
    dqk = M_HEADS * M_DK
    dv = M_HEADS * M_DV
    return pl.pallas_call(
        _mlstm_prompt_kernel,
        grid=(b, t // L),
        in_specs=[
            pl.BlockSpec((None, L, dqk), lambda i, c: (i, c, 0)),
            pl.BlockSpec((None, L, dqk), lambda i, c: (i, c, 1)),
            pl.BlockSpec((None, L, dv), lambda i, c: (i, c, 1)),
            pl.BlockSpec((None, L, dv), lambda i, c: (i, c, 2)),
            pl.BlockSpec((None, L, LANES), lambda i, c: (i, c, GATE_ROW)),
            _layer_spec((1, LANES), l, lambda i, c: (0, 0)),
            _layer_spec((1, dv), l, lambda i, c: (0, 0)),
        ],
        out_specs=[
            pl.BlockSpec((None, L, dv), lambda i, c: (i, c, 0)),
            pl.BlockSpec((None, M_HEADS, M_DK, M_DV), lambda i, c: (i, 0, 0, 0)),
            pl.BlockSpec((None, M_HEADS, 1, M_DK), lambda i, c: (i, 0, 0, 0)),
            pl.BlockSpec((None, 1, LANES), lambda i, c: (i, 0, 0)),
        ],
        out_shape=[
            jax.ShapeDtypeStruct((b, t, dv), BF16),
            jax.ShapeDtypeStruct((b, M_HEADS, M_DK, M_DV), F32),
            jax.ShapeDtypeStruct((b, M_HEADS, 1, M_DK), F32),
            jax.ShapeDtypeStruct((b, 1, LANES), F32),
        ],
        scratch_shapes=[
            pltpu.VMEM((M_HEADS, M_DK, M_DV), F32),
            pltpu.VMEM((M_HEADS, 1, M_DK), F32),
            pltpu.VMEM((1, LANES), F32),
        ],
        compiler_params=_cparams("parallel", "arbitrary"),
        name="mlstm_prompt",
    )(proj3, proj3, proj3, proj3, proj3, gate_bias, m_norm)


def _token_rows(pr_ref, nb, off):
    return pr_ref[pl.ds(off, nb, stride=ROWS_PER_TOKEN), :]


def _token_rows2(pr_ref, nb, off):
    return jnp.concatenate([_token_rows(pr_ref, nb, off), _token_rows(pr_ref, nb, off + 1)], axis=1)


def _pick_row(tiles):
    rid = lax.broadcasted_iota(jnp.int32, tiles[0].shape, 0)
    out = tiles[0]
    for s in range(1, len(tiles)):
        out = jnp.where(rid == s, tiles[s], out)
    return out


def _columns(tiles):
    pad = LANES - SUBLANES * len(tiles)
    stack = jnp.concatenate(tiles + ([jnp.zeros((pad, LANES), F32)] if pad else []), axis=0)
    return stack.T


def _mlstm_sample_kernel(pr_ref, gb_ref, mn_ref, c_ref, n_ref, m_ref, acc_ref,
                         hm_ref, co_ref, no_ref, mo_ref):
    del acc_ref
    nb = m_ref.shape[0]
    gates = _token_rows(pr_ref, nb, GATE_ROW) + gb_ref[...]
    capped = _softcap(gates)
    logf = -_softplus(-capped)
    i4 = capped[:, GATE_I:GATE_I + M_HEADS]
    f4 = logf[:, GATE_F:GATE_F + M_HEADS]
    inter = f4 + m_ref[...]
    m_new = jnp.maximum(inter, i4)
    a4 = jnp.exp(inter - m_new)
    w4 = jnp.exp(i4 - m_new)
    e4 = jnp.exp(-m_new)
    mo_ref[...] = m_new

    scale = M_DK ** -0.5
    ys, n_rows = [], []
    for h in range(M_HEADS):
        a, wi, e = a4[:, h:h + 1], w4[:, h:h + 1], e4[:, h:h + 1]
        q = _token_rows(pr_ref, nb, ROW_MQ + h)
        k = _token_rows(pr_ref, nb, ROW_MK + h) * scale
        v = _token_rows2(pr_ref, nb, ROW_MV + 2 * h)
        og = _token_rows2(pr_ref, nb, ROW_MO + 2 * h)
        n_old = n_ref[:, h * M_DK:(h + 1) * M_DK]
        wk = wi * k
        n_rows.append(a * n_old + wk)
        wts = wi * jnp.sum(q * k, axis=1, keepdims=True)
        nq = a * jnp.sum(q * n_old, axis=1, keepdims=True) + wts
        wk_cols = _columns([wk])
        a_rows = jnp.broadcast_to(a, (nb, M_DV))
        qb = q.astype(BF16)
        qc = []
        for s in range(nb):
            c_old = c_ref[s, h]
            qc.append(_dot(qb, c_old.astype(BF16)))
            co_ref[s, h] = a_rows[s:s + 1, :] * c_old + wk_cols[:, s:s + 1] * v[s:s + 1, :]
        num = a * _pick_row(qc) + wts * v
        hh = num / jnp.maximum(jnp.abs(nq), e)
        ys.append(_rms(hh, mn_ref[:, h * M_DV:(h + 1) * M_DV]) * _sigmoid(og))
    hm_ref[...] = jnp.concatenate(ys, axis=1).astype(BF16)
    no_ref[...] = jnp.concatenate(n_rows, axis=1)


def _mlstm_sample(proj_rows, gate_bias, m_norm, c_all, n_all, m_all, c_acc, l):
    depth, b = m_all.shape[:2]
    nb = SAMPLE_BLOCK
    dv = M_HEADS * M_DV
    dk = M_HEADS * M_DK
    c_spec = _layer_spec((nb, M_HEADS, M_DK, M_DV), l, lambda i: (i, 0, 0, 0))
    return pl.pallas_call(
        _mlstm_sample_kernel,
        grid=(b // nb,),
        in_specs=[
            pl.BlockSpec((nb * ROWS_PER_TOKEN, LANES), lambda i: (i, 0)),
            _layer_spec((1, LANES), l, lambda i: (0, 0)),
            _layer_spec((1, dv), l, lambda i: (0, 0)),
            c_spec,
            _layer_spec((nb, dk), l, lambda i: (i, 0)),
            _layer_spec((nb, M_HEADS), l, lambda i: (i, 0)),
            pl.BlockSpec(memory_space=pl.ANY),
        ],
        out_specs=[
            pl.BlockSpec((nb, dv), lambda i: (i, 0)),
            c_spec,
            pl.BlockSpec((nb, dk), lambda i: (i, 0)),
            pl.BlockSpec((nb, M_HEADS), lambda i: (i, 0)),
        ],
        out_shape=[
            jax.ShapeDtypeStruct((b, dv), BF16),
            jax.ShapeDtypeStruct(c_all.shape, F32),
            jax.ShapeDtypeStruct((b, dk), F32),
            jax.ShapeDtypeStruct((b, M_HEADS), F32),
        ],
        input_output_aliases={6: 1},
        compiler_params=_cparams("parallel"),
        name="mlstm_sample",
    )(proj_rows, gate_bias, m_norm, c_all, n_all, m_all, c_acc)


def _gdn_gates(gates, alog):
    log_a = -jnp.exp(alog) * _softplus(gates)
    beta = _sigmoid(gates)
    return log_a, beta


def _gdn_prompt_kernel(x_ref, gz_ref, gt_ref, gb_ref, al_ref, cw_ref, gn_ref,
                       og_ref, s_ref, cv_ref, s_scr, tail_scr):
    Lb = x_ref.shape[0]
    L, P = G_CHUNK, G_PAIR
    n_pair = Lb // P
    c = pl.program_id(1)

    @pl.when(c == 0)
    def _():
        s_scr[...] = jnp.zeros_like(s_scr)
        tail_scr[...] = jnp.zeros_like(tail_scr)

    log_a, beta = _gdn_gates(gt_ref[...] + gb_ref[...], al_ref[...])
    row = lax.broadcasted_iota(jnp.int32, (Lb, Lb), 0)
    col = lax.broadcasted_iota(jnp.int32, (Lb, Lb), 1)
    shift = int(math.log2(L))
    same_chunk = jnp.right_shift(row, shift) == jnp.right_shift(col, shift)
    gcum = _dot_f32(jnp.where(same_chunk & (row >= col), 1.0, 0.0), log_a)
    gtot = _dot_f32(jnp.where(same_chunk, 1.0, 0.0), log_a)
    lane = lax.broadcasted_iota(jnp.int32, (Lb, LANES), 1)
    zt = jnp.where(lane < GATE_B, gcum, beta).T

    rp = lax.broadcasted_iota(jnp.int32, (P, P), 0)
    cp = lax.broadcasted_iota(jnp.int32, (P, P), 1)
    same = jnp.right_shift(rp, shift) == jnp.right_shift(cp, shift)
    incl = same & (rp >= cp)
    strict = same & (rp > cp)

    def conv_cols(c0):
        xc = x_ref[:, c0:c0 + LANES]
        xp = jnp.concatenate([tail_scr[:, c0:c0 + LANES], xc], axis=0)
        y = xc * cw_ref[3:4, c0:c0 + LANES]
        for j in range(CONV_W - 1):
            y = y + xp[5 + j:5 + j + Lb, :] * cw_ref[j:j + 1, c0:c0 + LANES]
        return y * _sigmoid(y)

    heads = []
    for h in range(G_HEADS):
        qc = conv_cols(h * G_DK)
        kc = conv_cols(G_HEADS * G_DK + h * G_DK)
        vc = conv_cols(2 * G_HEADS * G_DK + h * G_DV)
        q2 = qc * (lax.rsqrt(jnp.sum(qc * qc, axis=1, keepdims=True) + EPS) * (G_DK ** -0.5))
        k2 = kc * lax.rsqrt(jnp.sum(kc * kc, axis=1, keepdims=True) + EPS)
        g_col = gcum[:, GATE_A + h:GATE_A + h + 1]
        t_col = gtot[:, GATE_A + h:GATE_A + h + 1]
        b_col = beta[:, GATE_B + h:GATE_B + h + 1]
        eg = jnp.exp(g_col)
        heads.append(dict(
            qb=q2.astype(BF16), kb=k2.astype(BF16), eg=eg, b_col=b_col,
            vek=jnp.concatenate([vc, eg * k2], axis=1),
            kd=(k2 * jnp.exp(t_col - g_col)).astype(BF16),
            dS=jnp.exp(t_col),
        ))

    blocks = []
    for h in range(G_HEADS):
        hd = heads[h]
        for b in range(n_pair):
            r0 = b * P
            kb = hd["kb"][r0:r0 + P]
            g_col = gcum[r0:r0 + P, GATE_A + h:GATE_A + h + 1]
            g_row = zt[GATE_A + h:GATE_A + h + 1, r0:r0 + P]
            b_row = zt[GATE_B + h:GATE_B + h + 1, r0:r0 + P]
            dec = jnp.exp(jnp.where(incl, g_col - g_row, -jnp.inf))
            a_mat = jnp.where(strict, dec * _dot_nt(kb, kb) * b_row, 0.0)
            qkd = (_dot_nt(hd["qb"][r0:r0 + P], kb) * dec).astype(BF16)
            blocks.append(dict(h=h, r0=r0, mpow=-a_mat, qacc=-a_mat, qkd=qkd))

    for blk in blocks:
        blk["mb"] = blk["mpow"].astype(BF16)
    for _ in range(int(math.log2(L)) - 1):
        for blk in blocks:
            blk["mpow"] = _dot(blk["mb"], blk["mb"])
            blk["mb"] = blk["mpow"].astype(BF16)
        for blk in blocks:
            blk["qacc"] = blk["qacc"] + blk["mpow"] + _dot(blk["qacc"].astype(BF16), blk["mb"])

    for blk in blocks:
        hd = heads[blk["h"]]
        r0 = blk["r0"]
        vek = hd["vek"][r0:r0 + P]
        t = (vek + _dot(blk["qacc"].astype(BF16), vek.astype(BF16))) * hd["b_col"][r0:r0 + P]
        blk["uv"] = t[:, :G_DV]
        blk["kw"] = t[:, G_DV:].astype(BF16)

    states = [s_scr[h] for h in range(G_HEADS)]
    wus = [[] for _ in range(G_HEADS)]
    qss = [[] for _ in range(G_HEADS)]
    for ci in range(Lb // L):
        r0 = ci * L
        for h in range(G_HEADS):
            hd = heads[h]
            blk = blocks[h * n_pair + r0 // P]
            o0 = r0 - blk["r0"]
            lhs = jnp.concatenate([blk["kw"][o0:o0 + L], hd["qb"][r0:r0 + L]], axis=0)
            r = _dot(lhs, states[h].astype(BF16))
            wu = (blk["uv"][o0:o0 + L] - r[:L]).astype(BF16)
            qss[h].append(r[L:])
            wus[h].append(wu)
            states[h] = hd["dS"][r0:r0 + 1] * states[h] + _dot_tn(hd["kd"][r0:r0 + L], wu)
    for h in range(G_HEADS):
        s_scr[h] = states[h]

    for h in range(G_HEADS):
        hd = heads[h]
        for b in range(n_pair):
            r0 = b * P
            blk = blocks[h * n_pair + b]
            i0 = r0 // L
            wu = jnp.concatenate(wus[h][i0:i0 + P // L], axis=0)
            qs = jnp.concatenate(qss[h][i0:i0 + P // L], axis=0)
            o = hd["eg"][r0:r0 + P] * qs + _dot(blk["qkd"], wu)
            y = _rms(o, gn_ref[:, h * G_DV:(h + 1) * G_DV])
            gz = gz_ref[r0:r0 + P, h * G_DV:(h + 1) * G_DV]
            og_ref[r0:r0 + P, h * G_DV:(h + 1) * G_DV] = (y * (gz * _sigmoid(gz))).astype(BF16)

    tail_scr[...] = x_ref[Lb - SUBLANES:Lb, :]

    @pl.when(c == pl.num_programs(1) - 1)
    def _():
        s_ref[...] = s_scr[...]
        cv_ref[...] = x_ref[Lb - (CONV_W - 1):Lb, :]


def _gdn_prompt(proj3, gate_bias, alog, conv_w, g_norm, l):
    b, t, _ = proj3.shape
    Lb = G_BLOCK
    dv = G_HEADS * G_DV
    return pl.pallas_call(
        _gdn_prompt_kernel,
        grid=(b, t // Lb),
        in_specs=[
            pl.BlockSpec((None, Lb, G_CONV_CH), lambda i, c: (i, c, 1)),
            pl.BlockSpec((None, Lb, dv), lambda i, c: (i, c, 6)),
            pl.BlockSpec((None, Lb, LANES), lambda i, c: (i, c, GATE_ROW)),
            _layer_spec((1, LANES), l, lambda i, c: (0, 0)),
            _layer_spec((1, LANES), l, lambda i, c: (0, 0)),
            _layer_spec((CONV_W, G_CONV_CH), l, lambda i, c: (0, 0)),
            _layer_spec((1, dv), l, lambda i, c: (0, 0)),
        ],
        out_specs=[
            pl.BlockSpec((None, Lb, dv), lambda i, c: (i, c, 0)),
            pl.BlockSpec((None, G_HEADS, G_DK, G_DV), lambda i, c: (i, 0, 0, 0)),
            pl.BlockSpec((None, CONV_W - 1, G_CONV_CH), lambda i, c: (i, 0, 0)),
        ],
        out_shape=[
            jax.ShapeDtypeStruct((b, t, dv), BF16),
            jax.ShapeDtypeStruct((b, G_HEADS, G_DK, G_DV), F32),
            jax.ShapeDtypeStruct((b, CONV_W - 1, G_CONV_CH), F32),
        ],
        scratch_shapes=[
            pltpu.VMEM((G_HEADS, G_DK, G_DV), F32),
            pltpu.VMEM((SUBLANES, G_CONV_CH), F32),
        ],
        compiler_params=_cparams("parallel", "arbitrary"),
        name="gdn_prompt",
    )(proj3, proj3, proj3, gate_bias, alog, conv_w, g_norm)


def _gdn_sample_kernel(pr_ref, gb_ref, al_ref, cw_ref, gn_ref, s_ref, cb_ref, acc_ref,
                       og_ref, so_ref, co_ref):
    del acc_ref
    nb = cb_ref.shape[2]
    log_a, beta = _gdn_gates(_token_rows(pr_ref, nb, GATE_ROW) + gb_ref[...], al_ref[...])
    eg8 = jnp.exp(log_a)

    ys = []
    for r in range(CONV_ROWS):
        xr = _token_rows(pr_ref, nb, ROW_GX + r)
        y = xr * cw_ref[CONV_W - 1, r:r + 1, :]
        for j in range(CONV_W - 1):
            y = y + cb_ref[j, r] * cw_ref[j, r:r + 1, :]
        co_ref[0, r] = cb_ref[1, r]
        co_ref[1, r] = cb_ref[2, r]
        co_ref[2, r] = xr
        ys.append(y * _sigmoid(y))

    outs = []
    for h in range(G_HEADS):
        qc, kc, v = ys[h], ys[G_HEADS + h], ys[2 * G_HEADS + h]
        q = qc * lax.rsqrt(jnp.sum(qc * qc, axis=1, keepdims=True) + EPS) * (G_DK ** -0.5)
        k = kc * lax.rsqrt(jnp.sum(kc * kc, axis=1, keepdims=True) + EPS)
        eg = eg8[:, GATE_A + h:GATE_A + h + 1]
        bt = beta[:, GATE_B + h:GATE_B + h + 1]
        qb, kb = q.astype(BF16), k.astype(BF16)
        ks, qs = [], []
        for s in range(nb):
            sb = s_ref[s, h].astype(BF16)
            ks.append(_dot(kb, sb))
            qs.append(_dot(qb, sb))
        u = bt * (v - eg * _pick_row(ks))
        o = eg * _pick_row(qs) + jnp.sum(q * k, axis=1, keepdims=True) * u
        k_cols = _columns([k])
        eg_rows = jnp.broadcast_to(eg, (nb, G_DV))
        for s in range(nb):
            so_ref[s, h] = eg_rows[s:s + 1, :] * s_ref[s, h] + k_cols[:, s:s + 1] * u[s:s + 1, :]
        gz = _token_rows(pr_ref, nb, ROW_GZ + h)
        outs.append(_rms(o, gn_ref[:, h * G_DV:(h + 1) * G_DV]) * (gz * _sigmoid(gz)))
    og_ref[...] = jnp.concatenate(outs, axis=1).astype(BF16)


def _gdn_sample(proj_rows, gate_bias, alog, conv_w_rows, g_norm, s_all, conv_t, s_acc, l):
    depth, b = s_all.shape[:2]
    nb = SAMPLE_BLOCK
    dv = G_HEADS * G_DV
    s_spec = _layer_spec((nb, G_HEADS, G_DK, G_DV), l, lambda i: (i, 0, 0, 0))
    return pl.pallas_call(
        _gdn_sample_kernel,
        grid=(b // nb,),
        in_specs=[
            pl.BlockSpec((nb * ROWS_PER_TOKEN, LANES), lambda i: (i, 0)),
            _layer_spec((1, LANES), l, lambda i: (0, 0)),
            _layer_spec((1, LANES), l, lambda i: (0, 0)),
            _layer_spec((CONV_W, CONV_ROWS, LANES), l, lambda i: (0, 0, 0)),
            _layer_spec((1, dv), l, lambda i: (0, 0)),
            s_spec,
            _layer_spec((CONV_W - 1, CONV_ROWS, nb, LANES), l, lambda i: (0, 0, i, 0)),
            pl.BlockSpec(memory_space=pl.ANY),
        ],
        out_specs=[
            pl.BlockSpec((nb, dv), lambda i: (i, 0)),
            s_spec,
            pl.BlockSpec((CONV_W - 1, CONV_ROWS, nb, LANES), lambda i: (0, 0, i, 0)),
        ],
        out_shape=[
            jax.ShapeDtypeStruct((b, dv), BF16),
            jax.ShapeDtypeStruct(s_all.shape, F32),
            jax.ShapeDtypeStruct((CONV_W - 1, CONV_ROWS, b, LANES), F32),
        ],
        input_output_aliases={7: 1},
        compiler_params=_cparams("parallel"),
        name="gdn_sample",
    )(proj_rows, gate_bias, alog, conv_w_rows, g_norm, s_all, conv_t, s_acc)


def _arrange_w_in_kernel(wt_ref, o_ref):
    n_m = 2 * M_HEADS * M_DK + 2 * M_HEADS * M_DV
    n_g = 4 * G_HEADS * G_DK
    g0 = n_m + 2 * M_HEADS
    tk = wt_ref.shape[1]
    step = 2 * LANES
    for dst, src, n in ((0, 0, n_m), (n_m, g0, n_g)):
        for c in range(0, n, step):
            o_ref[:, dst + c:dst + c + step] = wt_ref[src + c:src + c + step, :].T.astype(BF16)
    n_gate = wt_ref.shape[0] - N_WIDE
    gate = jnp.concatenate([wt_ref[n_m:g0, :], wt_ref[g0 + n_g:, :], jnp.zeros((LANES - n_gate, tk), F32)], axis=0)
    o_ref[:, N_WIDE:N_WIDE + LANES] = gate.T.astype(BF16)
    o_ref[:, N_WIDE + LANES:] = jnp.zeros((tk, N_PROJ - N_WIDE - LANES), BF16)


def _arrange_w_in(w_in, tk=256):
    depth, rows, n_in = w_in.shape
    return pl.pallas_call(
        _arrange_w_in_kernel,
        grid=(depth, rows // tk),
        in_specs=[pl.BlockSpec((None, n_in, tk), lambda l, i: (l, 0, i))],
        out_specs=pl.BlockSpec((None, tk, N_PROJ), lambda l, i: (l, i, 0)),
        out_shape=jax.ShapeDtypeStruct((depth, rows, N_PROJ), BF16),
        compiler_params=_cparams("parallel", "parallel"),
        name="arrange_w_in",
    )(jnp.swapaxes(w_in, 1, 2))


def _lane_row(parts, depth):
    row = jnp.zeros((depth, 1, LANES), F32)
    for off, arr in parts:
        row = row.at[:, 0, off:off + arr.shape[-1]].set(arr.astype(F32))
    return row


def kernel(x_prompt, x_sample, mem_prompt, state_mlstm_C, state_mlstm_n, state_mlstm_m, state_gdn_S,
           state_gdn_conv, cache_mem_k, cache_mem_v, norm_mix, w_in, mlstm_b_i, mlstm_b_f, mlstm_norm,
           gdn_conv_w, gdn_A_log, gdn_dt_bias, gdn_norm, w_out, norm_xattn, norm_mem, xattn_wq, xattn_wk,
           xattn_wv, xattn_wo, norm_ffn, ffn_w_gate, ffn_w_up, ffn_w_down, norm_final):
    bp, seq, _ = x_prompt.shape
    bs = x_sample.shape[0]
    depth = w_in.shape[0]
    dq = X_HEADS * X_DH
    tm_p, tm_wide = 512, 1024

    w_in_b = _arrange_w_in(w_in)
    gate_bias = _lane_row([(GATE_I, mlstm_b_i), (GATE_F, mlstm_b_f), (GATE_A, gdn_dt_bias)], depth)
    alog = _lane_row([(GATE_A, gdn_A_log)], depth)
    g_mix, g_x, g_mem, g_ffn = (g.reshape(depth, 1, D_MODEL) for g in (norm_mix, norm_xattn, norm_mem, norm_ffn))
    g_final = norm_final.reshape(1, D_MODEL)
    m_norm = mlstm_norm.reshape(depth, 1, M_HEADS * M_DV)
    g_norm = gdn_norm.reshape(depth, 1, G_HEADS * G_DV)
    conv_w_rows = gdn_conv_w.reshape(depth, CONV_W, CONV_ROWS, LANES)
    conv_t = jnp.transpose(state_gdn_conv.reshape(depth, bs, CONV_W - 1, CONV_ROWS, LANES), (0, 2, 3, 1, 4))
    n_all = state_mlstm_n.reshape(depth, bs, M_HEADS * M_DK)
    mem_k = cache_mem_k.reshape(depth, bs, N_MEM * X_HEADS, X_DH)
    mem_v = cache_mem_v.reshape(depth, bs, N_MEM * X_HEADS, X_DH)

    mk_all, mv_all = _mem_kv(mem_prompt.reshape(bp * N_MEM, D_MODEL), g_mem, xattn_wk, xattn_wv)

    xp = x_prompt.reshape(bp * seq, D_MODEL)
    xs = x_sample.reshape(bs, D_MODEL)
    c_acc = jnp.zeros(state_mlstm_C.shape, F32)
    s_acc = jnp.zeros(state_gdn_S.shape, F32)
    outs = [[] for _ in range(8)]
    for l in range(depth):
        proj_p, proj_s = _proj(xp, xs, g_mix, w_in_b, l, tm_wide)
        proj3 = proj_p.reshape(bp, seq, N_PROJ)
        hm_p, c_p, n_p, m_p = _mlstm_prompt(proj3, gate_bias, m_norm, l)
        og_p, s_p, cv_p = _gdn_prompt(proj3, gate_bias, alog, gdn_conv_w, g_norm, l)
        for lst, val in zip(outs[:5], (c_p, n_p.reshape(bp, M_HEADS, M_DK), m_p[:, 0, :M_HEADS], s_p, cv_p)):
            lst.append(val)
        proj_rows = proj_s.reshape(bs * ROWS_PER_TOKEN, LANES)
        hm_s, c_acc, n_s, m_s = _mlstm_sample(proj_rows, gate_bias, m_norm, state_mlstm_C, n_all, state_mlstm_m,
                                              c_acc, l)
        og_s, s_acc, cv_s = _gdn_sample(proj_rows, gate_bias, alog, conv_w_rows, g_norm, state_gdn_S, conv_t,
                                        s_acc, l)
        xp, xs = _mix_out(xp, xs, hm_p.reshape(bp * seq, -1), og_p.reshape(bp * seq, -1), hm_s, og_s, w_out, l, tm_p)
        xp, xs = _xattn(xp, xs, g_x, xattn_wq, mk_all, mv_all, mem_k, mem_v, xattn_wo, l, tm_p, seq)
        xp, xs = _ffn(xp, xs, g_ffn, ffn_w_gate, ffn_w_up, ffn_w_down, g_final, l, tm_wide, l == depth - 1)
        cv_s = jnp.transpose(cv_s, (2, 0, 1, 3)).reshape(bs, CONV_W - 1, G_CONV_CH)
        for lst, val in zip(outs[5:], (n_s.reshape(bs, M_HEADS, M_DK), m_s, cv_s)):
            lst.append(val)

    y_prompt = xp.reshape(bp, seq, D_MODEL)
    y_sample = xs.reshape(bs, 1, D_MODEL)
    pc, pn, pm, ps, pconv, sn, sm, sconv = (jnp.stack(o) for o in outs)
    mem_shape = (depth, bp, N_MEM, X_HEADS, X_DH)
    return (y_prompt, y_sample, pc, pn, pm, ps, pconv, mk_all.reshape(mem_shape), mv_all.reshape(mem_shape),
            c_acc, sn, sm, s_acc, sconv)
```

```python
import functools
import math

import jax
import jax.numpy as jnp
from jax import lax
from jax.experimental import pallas as pl
from jax.experimental.pallas import tpu as pltpu

F32 = jnp.float32
BF16 = jnp.bfloat16

D_MODEL = 2048
M_HEADS = 4
M_DK = 128
M_DV = 256
G_HEADS = 8
G_DK = 128
G_DV = 128
CONV_W = 4
G_CONV_CH = G_HEADS * (2 * G_DK + G_DV)
X_HEADS = 4
X_DH = 128
N_MEM = 256
D_FF = 5632
GATE_CAP = 15.0
EPS = 1e-6

LANES = 128
SUBLANES = 8
N_WIDE = 2 * M_HEADS * M_DK + 2 * M_HEADS * M_DV + 4 * G_HEADS * G_DK
PROJ_TN = 1280
N_PROJ = -(-(N_WIDE + LANES) // PROJ_TN) * PROJ_TN
ROWS_PER_TOKEN = N_PROJ // LANES
GATE_ROW = N_WIDE // LANES
GATE_I, GATE_F, GATE_A, GATE_B = 0, 4, 8, 16
ROW_MQ, ROW_MK, ROW_MV, ROW_MO = 0, 4, 8, 16
ROW_GX, ROW_GZ = 24, 48
CONV_ROWS = G_CONV_CH // LANES

VMEM_LIMIT = 56 * 1024 * 1024

M_CHUNK = 256
G_BLOCK = 256
G_CHUNK = 64
G_PAIR = 2 * G_CHUNK
SAMPLE_BLOCK = 8

NT_DIMS = (((1,), (1,)), ((), ()))
TN_DIMS = (((0,), (0,)), ((), ()))


def _cparams(*sem):
    return pltpu.CompilerParams(dimension_semantics=sem, vmem_limit_bytes=VMEM_LIMIT)


def _dot(a, b):
    return jnp.dot(a, b, preferred_element_type=F32)


def _dot_nt(a, b):
    return lax.dot_general(a, b, NT_DIMS, preferred_element_type=F32)


def _dot_tn(a, b):
    return lax.dot_general(a, b, TN_DIMS, preferred_element_type=F32)


def _dot_f32(a, b):
    return jnp.dot(a, b, preferred_element_type=F32, precision=lax.Precision.HIGHEST)


def _rms(x, g):
    ms = jnp.mean(x * x, axis=-1, keepdims=True)
    return x * lax.rsqrt(ms + EPS) * g


def _sigmoid(x):
    return 1.0 / (1.0 + jnp.exp(-x))


def _softplus(x):
    return jnp.maximum(x, 0.0) + jnp.log1p(jnp.exp(-jnp.abs(x)))


def _softcap(x):
    return GATE_CAP * jnp.tanh(x / GATE_CAP)


def _layer_spec(shape, l, tail):
    return pl.BlockSpec((None,) + shape, lambda *g: (l,) + tail(*g))


def _sample_share(m_prompt, m_sample, tm):
    tiles = m_prompt // tm
    rs = m_sample // tiles
    assert tiles * tm == m_prompt and tiles * rs == m_sample and rs % SUBLANES == 0, (m_prompt, m_sample, tm)
    return tiles, rs


def _proj_kernel(xp_ref, xs_ref, g_ref, w_ref, op_ref, os_ref, h_scr):
    tm = xp_ref.shape[0]

    @pl.when(pl.program_id(1) == 0)
    def _():
        h_scr[:tm] = _rms(xp_ref[...], g_ref[...]).astype(BF16)
        h_scr[tm:] = _rms(xs_ref[...], g_ref[...]).astype(BF16)

    r = _dot(h_scr[...], w_ref[...])
    op_ref[...] = r[:tm]
    os_ref[...] = r[tm:]


def _proj(xp, xs, g, w, l, tm):
    mp, ms = xp.shape[0], xs.shape[0]
    tiles, rs = _sample_share(mp, ms, tm)
    return pl.pallas_call(
        _proj_kernel,
        grid=(tiles, N_PROJ // PROJ_TN),
        in_specs=[
            pl.BlockSpec((tm, D_MODEL), lambda i, j: (i, 0)),
            pl.BlockSpec((rs, D_MODEL), lambda i, j: (i, 0)),
            _layer_spec((1, D_MODEL), l, lambda i, j: (0, 0)),
            _layer_spec((D_MODEL, PROJ_TN), l, lambda i, j: (0, j)),
        ],
        out_specs=[pl.BlockSpec((tm, PROJ_TN), lambda i, j: (i, j)), pl.BlockSpec((rs, PROJ_TN), lambda i, j: (i, j))],
        out_shape=[jax.ShapeDtypeStruct((mp, N_PROJ), F32), jax.ShapeDtypeStruct((ms, N_PROJ), F32)],
        scratch_shapes=[pltpu.VMEM((tm + rs, D_MODEL), BF16)],
        compiler_params=_cparams("parallel", "arbitrary"),
        name="proj",
    )(xp, xs, g, w)


def _mem_kv_kernel(x_ref, g_ref, wk_ref, wv_ref, k_ref, v_ref):
    h = _rms(x_ref[...], g_ref[...]).astype(BF16)
    k_ref[...] = _dot(h, wk_ref[...].astype(BF16))
    v_ref[...] = _dot(h, wv_ref[...].astype(BF16))


def _mem_kv(mem, g, wk, wv):
    depth = wk.shape[0]
    m = mem.shape[0]
    dq = X_HEADS * X_DH
    tm = N_MEM
    w_spec = pl.BlockSpec((None, D_MODEL, dq), lambda l, i: (l, 0, 0))
    o_spec = pl.BlockSpec((None, tm, dq), lambda l, i: (l, i, 0))
    return pl.pallas_call(
        _mem_kv_kernel,
        grid=(depth, m // tm),
        in_specs=[
            pl.BlockSpec((tm, D_MODEL), lambda l, i: (i, 0)),
            pl.BlockSpec((None, 1, D_MODEL), lambda l, i: (l, 0, 0)),
            w_spec, w_spec,
        ],
        out_specs=[o_spec, o_spec],
        out_shape=[jax.ShapeDtypeStruct((depth, m, dq), F32)] * 2,
        compiler_params=_cparams("parallel", "parallel"),
        name="mem_kv",
    )(mem, g, wk, wv)


def _mix_out_kernel(xp_ref, xs_ref, hmp_ref, ogp_ref, hms_ref, ogs_ref, w1_ref, w2_ref, op_ref, os_ref):
    tm = xp_ref.shape[0]
    hm = jnp.concatenate([hmp_ref[...], hms_ref[...]], axis=0)
    og = jnp.concatenate([ogp_ref[...], ogs_ref[...]], axis=0)
    acc = _dot(hm, w1_ref[...].astype(BF16))
    acc = acc + _dot(og, w2_ref[...].astype(BF16))
    op_ref[...] = xp_ref[...] + acc[:tm]
    os_ref[...] = xs_ref[...] + acc[tm:]


def _mix_out(xp, xs, hmp, ogp, hms, ogs, w_out, l, tm):
    mp, ms = xp.shape[0], xs.shape[0]
    tiles, rs = _sample_share(mp, ms, tm)
    half = D_MODEL // 2
    return pl.pallas_call(
        _mix_out_kernel,
        grid=(tiles,),
        in_specs=[
            pl.BlockSpec((tm, D_MODEL), lambda i: (i, 0)),
            pl.BlockSpec((rs, D_MODEL), lambda i: (i, 0)),
            pl.BlockSpec((tm, half), lambda i: (i, 0)),
            pl.BlockSpec((tm, half), lambda i: (i, 0)),
            pl.BlockSpec((rs, half), lambda i: (i, 0)),
            pl.BlockSpec((rs, half), lambda i: (i, 0)),
            _layer_spec((half, D_MODEL), l, lambda i: (0, 0)),
            _layer_spec((half, D_MODEL), l, lambda i: (1, 0)),
        ],
        out_specs=[pl.BlockSpec((tm, D_MODEL), lambda i: (i, 0)), pl.BlockSpec((rs, D_MODEL), lambda i: (i, 0))],
        out_shape=[jax.ShapeDtypeStruct((mp, D_MODEL), F32), jax.ShapeDtypeStruct((ms, D_MODEL), F32)],
        compiler_params=_cparams("parallel"),
        name="mix_out",
    )(xp, xs, hmp, ogp, hms, ogs, w_out, w_out)


def _ffn_kernel(xp_ref, xs_ref, g_ref, wg_ref, wu_ref, wd_ref, gf_ref, op_ref, os_ref, h_scr, *, final):
    j = pl.program_id(1)
    tm = xp_ref.shape[0]

    @pl.when(j == 0)
    def _():
        for x_ref, o_ref, rows in ((xp_ref, op_ref, slice(0, tm)), (xs_ref, os_ref, slice(tm, None))):
            x = x_ref[...]
            h_scr[rows] = _rms(x, g_ref[...]).astype(BF16)
            o_ref[...] = x

    h = h_scr[...]
    a = _dot(h, wg_ref[...].astype(BF16))
    b = _dot(h, wu_ref[...].astype(BF16))
    t = (a * _sigmoid(a)) * b
    r = _dot(t.astype(BF16), wd_ref[...].astype(BF16))
    op_ref[...] += r[:tm]
    os_ref[...] += r[tm:]

    if final:
        @pl.when(j == pl.num_programs(1) - 1)
        def _():
            op_ref[...] = _rms(op_ref[...], gf_ref[...])
            os_ref[...] = _rms(os_ref[...], gf_ref[...])


def _ffn(xp, xs, g, w_gate, w_up, w_down, g_final, l, tm, final, tf=256):
    mp, ms = xp.shape[0], xs.shape[0]
    tiles, rs = _sample_share(mp, ms, tm)
    return pl.pallas_call(
        functools.partial(_ffn_kernel, final=final),
        grid=(tiles, D_FF // tf),
        in_specs=[
            pl.BlockSpec((tm, D_MODEL), lambda i, j: (i, 0)),
            pl.BlockSpec((rs, D_MODEL), lambda i, j: (i, 0)),
            _layer_spec((1, D_MODEL), l, lambda i, j: (0, 0)),
            _layer_spec((D_MODEL, tf), l, lambda i, j: (0, j)),
            _layer_spec((D_MODEL, tf), l, lambda i, j: (0, j)),
            _layer_spec((tf, D_MODEL), l, lambda i, j: (j, 0)),
            pl.BlockSpec((1, D_MODEL), lambda i, j: (0, 0)),
        ],
        out_specs=[pl.BlockSpec((tm, D_MODEL), lambda i, j: (i, 0)), pl.BlockSpec((rs, D_MODEL), lambda i, j: (i, 0))],
        out_shape=[jax.ShapeDtypeStruct((mp, D_MODEL), F32), jax.ShapeDtypeStruct((ms, D_MODEL), F32)],
        scratch_shapes=[pltpu.VMEM((tm + rs, D_MODEL), BF16)],
        compiler_params=_cparams("parallel", "arbitrary"),
        name="ffn",
    )(xp, xs, g, w_gate, w_up, w_down, g_final)


def _softmax_rows(s):
    s = s - jnp.max(s, axis=-1, keepdims=True)
    p = jnp.exp(s)
    return p / jnp.sum(p, axis=-1, keepdims=True)


def _prompt_attention(q, k_ref, v_ref):
    heads = []
    for h in range(X_HEADS):
        sl = slice(h * X_DH, (h + 1) * X_DH)
        s = _dot_nt(q[:, sl].astype(BF16), k_ref[:, sl].astype(BF16)) * (X_DH ** -0.5)
        p = _softmax_rows(s)
        heads.append(_dot(p.astype(BF16), v_ref[:, sl].astype(BF16)))
    return jnp.concatenate(heads, axis=1)


def _sample_attention(q, k_ref, v_ref):
    nb = q.shape[0]
    nk = N_MEM * X_HEADS
    qh = [q[:, h * X_DH:(h + 1) * X_DH] for h in range(X_HEADS)]
    row = lax.broadcasted_iota(jnp.int32, (SUBLANES, nk), 0)
    col = lax.broadcasted_iota(jnp.int32, (SUBLANES, nk), 1)
    own = (col & (X_HEADS - 1)) == (row & (X_HEADS - 1))
    row8 = lax.broadcasted_iota(jnp.int32, (SUBLANES, X_DH), 0)
    outs = [jnp.zeros((nb, X_DH), F32) for _ in range(X_HEADS)]
    scores = []
    for s in range(nb):
        q8 = jnp.zeros((SUBLANES, X_DH), F32)
        for h in range(X_HEADS):
            q8 = jnp.where(row8 == h, qh[h][s:s + 1, :], q8)
        scores.append(_dot_nt(q8.astype(BF16), k_ref[s].astype(BF16)) * (X_DH ** -0.5))
    probs = [_softmax_rows(jnp.where(own, sc, -jnp.inf)).astype(BF16) for sc in scores]
    o8s = [_dot(probs[s], v_ref[s].astype(BF16)) for s in range(nb)]
    for s in range(nb):
        for h in range(X_HEADS):
            outs[h] = jnp.where(row8[:nb] == s, o8s[s][h:h + 1, :], outs[h])
    return jnp.concatenate(outs, axis=1)


def _xattn_kernel(xp_ref, xs_ref, g_ref, wq_ref, kp_ref, vp_ref, ks_ref, vs_ref, wo_ref, op_ref, os_ref):
    tm = xp_ref.shape[0]
    h = jnp.concatenate([_rms(xp_ref[...], g_ref[...]), _rms(xs_ref[...], g_ref[...])], axis=0)
    q = _dot(h.astype(BF16), wq_ref[...].astype(BF16))
    o = jnp.concatenate([_prompt_attention(q[:tm], kp_ref, vp_ref), _sample_attention(q[tm:], ks_ref, vs_ref)],
                        axis=0)
    r = _dot(o.astype(BF16), wo_ref[...].astype(BF16))
    op_ref[...] = xp_ref[...] + r[:tm]
    os_ref[...] = xs_ref[...] + r[tm:]


def _xattn(xp, xs, g, wq, mk_p, mv_p, mk_s, mv_s, wo, l, tm, seq):
    mp, ms = xp.shape[0], xs.shape[0]
    tiles, rs = _sample_share(mp, ms, tm)
    dq = X_HEADS * X_DH
    per_seq = seq // tm
    kvp_spec = pl.BlockSpec((None, N_MEM, dq), lambda i: (l, i // per_seq, 0))
    kvs_spec = _layer_spec((rs, N_MEM * X_HEADS, X_DH), l, lambda i: (i, 0, 0))
    return pl.pallas_call(
        _xattn_kernel,
        grid=(tiles,),
        in_specs=[
            pl.BlockSpec((tm, D_MODEL), lambda i: (i, 0)),
            pl.BlockSpec((rs, D_MODEL), lambda i: (i, 0)),
            _layer_spec((1, D_MODEL), l, lambda i: (0, 0)),
            _layer_spec((D_MODEL, dq), l, lambda i: (0, 0)),
            kvp_spec, kvp_spec, kvs_spec, kvs_spec,
            _layer_spec((dq, D_MODEL), l, lambda i: (0, 0)),
        ],
        out_specs=[pl.BlockSpec((tm, D_MODEL), lambda i: (i, 0)), pl.BlockSpec((rs, D_MODEL), lambda i: (i, 0))],
        out_shape=[jax.ShapeDtypeStruct((mp, D_MODEL), F32), jax.ShapeDtypeStruct((ms, D_MODEL), F32)],
        compiler_params=_cparams("parallel"),
        name="xattn",
    )(xp, xs, g, wq, mk_p, mv_p, mk_s, mv_s, wo)


def _mlstm_prompt_kernel(q_ref, k_ref, v_ref, mo_ref, gt_ref, gb_ref, mn_ref,
                         hm_ref, c_ref, n_ref, m_ref, c_scr, n_scr, m_scr):
    L = q_ref.shape[0]
    c = pl.program_id(1)

    @pl.when(c == 0)
    def _():
        c_scr[...] = jnp.zeros_like(c_scr)
        n_scr[...] = jnp.zeros_like(n_scr)
        m_scr[...] = jnp.zeros_like(m_scr)

    gates = gt_ref[...] + gb_ref[...]
    capped = _softcap(gates)
    logf = -_softplus(-capped)
    row = lax.broadcasted_iota(jnp.int32, (L, L), 0)
    col = lax.broadcasted_iota(jnp.int32, (L, L), 1)
    causal = row >= col
    bcum = _dot_f32(causal.astype(F32), logf)
    lane = lax.broadcasted_iota(jnp.int32, (L, LANES), 1)
    z = jnp.where(lane < GATE_F, capped, bcum)
    zt = z.T

    hs = range(M_HEADS)
    i_col = [z[:, GATE_I + h:GATE_I + h + 1] for h in hs]
    b_col = [z[:, GATE_F + h:GATE_F + h + 1] for h in hs]
    m_prev = [m_scr[0:1, h:h + 1] for h in hs]
    d = [jnp.where(causal, b_col[h] + (zt[GATE_I + h:GATE_I + h + 1, :] - zt[GATE_F + h:GATE_F + h + 1, :]),
                   -jnp.inf) for h in hs]
    inter = [b_col[h] + m_prev[h] for h in hs]
    m_t = [jnp.maximum(inter[h], jnp.max(d[h], axis=1, keepdims=True)) for h in hs]
    q = [q_ref[:, h * M_DK:(h + 1) * M_DK] for h in hs]
    k = [k_ref[:, h * M_DK:(h + 1) * M_DK] * (M_DK ** -0.5) for h in hs]
    qb = [q[h].astype(BF16) for h in hs]
    vb = [v_ref[:, h * M_DV:(h + 1) * M_DV].astype(BF16) for h in hs]
    qk = [_dot_nt(qb[h], k[h].astype(BF16)) for h in hs]
    qc = [_dot(qb[h], c_scr[h].astype(BF16)) for h in hs]
    w = [jnp.exp(d[h] - m_t[h]) * qk[h] for h in hs]
    a = [jnp.exp(inter[h] - m_t[h]) for h in hs]
    wv = [_dot(w[h].astype(BF16), vb[h]) for h in hs]
    nq = [a[h] * jnp.sum(q[h] * n_scr[h], axis=1, keepdims=True) + jnp.sum(w[h], axis=1, keepdims=True) for h in hs]
    hh = [(a[h] * qc[h] + wv[h]) / jnp.maximum(jnp.abs(nq[h]), jnp.exp(-m_t[h])) for h in hs]
    for h in hs:
        y = _rms(hh[h], mn_ref[:, h * M_DV:(h + 1) * M_DV])
        y = y * _sigmoid(mo_ref[:, h * M_DV:(h + 1) * M_DV])
        hm_ref[:, h * M_DV:(h + 1) * M_DV] = y.astype(BF16)

    m_new = [m_t[h][L - 1:L, :] for h in hs]
    b_last = [b_col[h][L - 1:L, :] for h in hs]
    kw = [k[h] * jnp.exp(b_last[h] - b_col[h] + i_col[h] - m_new[h]) for h in hs]
    upd = [_dot_tn(kw[h].astype(BF16), vb[h]) for h in hs]
    for h in hs:
        decay = jnp.exp(b_last[h] + m_prev[h] - m_new[h])
        c_scr[h] = decay * c_scr[h] + upd[h]
        n_scr[h] = decay * n_scr[h] + jnp.sum(kw[h], axis=0, keepdims=True)
        m_scr[0:1, h:h + 1] = m_new[h]

    @pl.when(c == pl.num_programs(1) - 1)
    def _():
        c_ref[...] = c_scr[...]
        n_ref[...] = n_scr[...]
        m_ref[...] = m_scr[...]


def _sample_block_index(b, nc, n_blocks):
    every = b * nc // n_blocks
    assert every * n_blocks == b * nc and nc % every == 0, (b, nc, n_blocks)
    per = nc // every
    return (lambda i, c: i * per + c // every), every


def _token_rows(pr_ref, nb, off):
    return pr_ref[pl.ds(off, nb, stride=ROWS_PER_TOKEN), :]


def _token_rows2(pr_ref, nb, off):
    return jnp.concatenate([_token_rows(pr_ref, nb, off), _token_rows(pr_ref, nb, off + 1)], axis=1)


def _pick_row(tiles):
    rid = lax.broadcasted_iota(jnp.int32, tiles[0].shape, 0)
    out = tiles[0]
    for s in range(1, len(tiles)):
        out = jnp.where(rid == s, tiles[s], out)
    return out


def _columns(tiles):
    pad = LANES - SUBLANES * len(tiles)
    stack = jnp.concatenate(tiles + ([jnp.zeros((pad, LANES), F32)] if pad else []), axis=0)
    return stack.T


def _mlstm_sample_kernel(pr_ref, gb_ref, mn_ref, c_ref, n_ref, m_ref, acc_ref,
                         hm_ref, co_ref, no_ref, mo_ref):
    del acc_ref
    nb = m_ref.shape[0]
    gates = _token_rows(pr_ref, nb, GATE_ROW) + gb_ref[...]
    capped = _softcap(gates)
    logf = -_softplus(-capped)
    i4 = capped[:, GATE_I:GATE_I + M_HEADS]
    f4 = logf[:, GATE_F:GATE_F + M_HEADS]
    inter = f4 + m_ref[...]
    m_new = jnp.maximum(inter, i4)
    a4 = jnp.exp(inter - m_new)
    w4 = jnp.exp(i4 - m_new)
    e4 = jnp.exp(-m_new)
    mo_ref[...] = m_new

    scale = M_DK ** -0.5
    ys, n_rows = [], []
    for h in range(M_HEADS):
        a, wi, e = a4[:, h:h + 1], w4[:, h:h + 1], e4[:, h:h + 1]
        q = _token_rows(pr_ref, nb, ROW_MQ + h)
        k = _token_rows(pr_ref, nb, ROW_MK + h) * scale
        v = _token_rows2(pr_ref, nb, ROW_MV + 2 * h)
        og = _token_rows2(pr_ref, nb, ROW_MO + 2 * h)
        n_old = n_ref[:, h * M_DK:(h + 1) * M_DK]
        wk = wi * k
        n_rows.append(a * n_old + wk)
        wts = wi * jnp.sum(q * k, axis=1, keepdims=True)
        nq = a * jnp.sum(q * n_old, axis=1, keepdims=True) + wts
        wk_cols = _columns([wk])
        a_rows = jnp.broadcast_to(a, (nb, M_DV))
        qb = q.astype(BF16)
        qc = []
        for s in range(nb):
            c_old = c_ref[s, h]
            qc.append(_dot(qb, c_old.astype(BF16)))
            co_ref[s, h] = a_rows[s:s + 1, :] * c_old + wk_cols[:, s:s + 1] * v[s:s + 1, :]
        num = a * _pick_row(qc) + wts * v
        hh = num / jnp.maximum(jnp.abs(nq), e)
        ys.append(_rms(hh, mn_ref[:, h * M_DV:(h + 1) * M_DV]) * _sigmoid(og))
    hm_ref[...] = jnp.concatenate(ys, axis=1).astype(BF16)
    no_ref[...] = jnp.concatenate(n_rows, axis=1)


def _mlstm_kernel(*refs, every):
    p_in, s_in, p_out, s_out, scr = refs[:7], refs[7:14], refs[14:18], refs[18:22], refs[22:]
    _mlstm_prompt_kernel(*p_in, *p_out, *scr)

    @pl.when(pl.program_id(1) % every == 0)
    def _():
        _mlstm_sample_kernel(*s_in, *s_out)


def _mlstm(proj3, proj_rows, gate_bias, m_norm, c_all, n_all, m_all, c_acc, l):
    b, t, _ = proj3.shape
    bs = m_all.shape[1]
    L = M_CHUNK
    nb = SAMPLE_BLOCK
    dqk = M_HEADS * M_DK
    dv = M_HEADS * M_DV
    blk, every = _sample_block_index(b, t // L, bs // nb)
    c_spec = _layer_spec((nb, M_HEADS, M_DK, M_DV), l, lambda i, c: (blk(i, c), 0, 0, 0))
    return pl.pallas_call(
        functools.partial(_mlstm_kernel, every=every),
        grid=(b, t // L),
        in_specs=[
            pl.BlockSpec((None, L, dqk), lambda i, c: (i, c, 0)),
            pl.BlockSpec((None, L, dqk), lambda i, c: (i, c, 1)),
            pl.BlockSpec((None, L, dv), lambda i, c: (i, c, 1)),
            pl.BlockSpec((None, L, dv), lambda i, c: (i, c, 2)),
            pl.BlockSpec((None, L, LANES), lambda i, c: (i, c, GATE_ROW)),
            _layer_spec((1, LANES), l, lambda i, c: (0, 0)),
            _layer_spec((1, dv), l, lambda i, c: (0, 0)),
            pl.BlockSpec((nb * ROWS_PER_TOKEN, LANES), lambda i, c: (blk(i, c), 0)),
            _layer_spec((1, LANES), l, lambda i, c: (0, 0)),
            _layer_spec((1, dv), l, lambda i, c: (0, 0)),
            c_spec,
            _layer_spec((nb, dqk), l, lambda i, c: (blk(i, c), 0)),
            _layer_spec((nb, M_HEADS), l, lambda i, c: (blk(i, c), 0)),
            pl.BlockSpec(memory_space=pl.ANY),
        ],
        out_specs=[
            pl.BlockSpec((None, L, dv), lambda i, c: (i, c, 0)),
            pl.BlockSpec((None, M_HEADS, M_DK, M_DV), lambda i, c: (i, 0, 0, 0)),
            pl.BlockSpec((None, M_HEADS, 1, M_DK), lambda i, c: (i, 0, 0, 0)),
            pl.BlockSpec((None, 1, LANES), lambda i, c: (i, 0, 0)),
            pl.BlockSpec((nb, dv), lambda i, c: (blk(i, c), 0)),
            c_spec,
            pl.BlockSpec((nb, dqk), lambda i, c: (blk(i, c), 0)),
            pl.BlockSpec((nb, M_HEADS), lambda i, c: (blk(i, c), 0)),
        ],
        out_shape=[
            jax.ShapeDtypeStruct((b, t, dv), BF16),
            jax.ShapeDtypeStruct((b, M_HEADS, M_DK, M_DV), F32),
            jax.ShapeDtypeStruct((b, M_HEADS, 1, M_DK), F32),
            jax.ShapeDtypeStruct((b, 1, LANES), F32),
            jax.ShapeDtypeStruct((bs, dv), BF16),
            jax.ShapeDtypeStruct(c_all.shape, F32),
            jax.ShapeDtypeStruct((bs, dqk), F32),
            jax.ShapeDtypeStruct((bs, M_HEADS), F32),
        ],
        scratch_shapes=[
            pltpu.VMEM((M_HEADS, M_DK, M_DV), F32),
            pltpu.VMEM((M_HEADS, 1, M_DK), F32),
            pltpu.VMEM((1, LANES), F32),
        ],
        input_output_aliases={13: 5},
        compiler_params=_cparams("parallel", "arbitrary"),
        name="mlstm",
    )(proj3, proj3, proj3, proj3, proj3, gate_bias, m_norm,
      proj_rows, gate_bias, m_norm, c_all, n_all, m_all, c_acc)


def _gdn_gates(gates, alog):
    log_a = -jnp.exp(alog) * _softplus(gates)
    beta = _sigmoid(gates)
    return log_a, beta


def _gdn_prompt_kernel(x_ref, gz_ref, gt_ref, gb_ref, al_ref, cw_ref, gn_ref,
                       og_ref, s_ref, cv_ref, s_scr, tail_scr):
    Lb = x_ref.shape[0]
    L, P = G_CHUNK, G_PAIR
    n_pair = Lb // P
    c = pl.program_id(1)

    @pl.when(c == 0)
    def _():
        s_scr[...] = jnp.zeros_like(s_scr)
        tail_scr[...] = jnp.zeros_like(tail_scr)

    log_a, beta = _gdn_gates(gt_ref[...] + gb_ref[...], al_ref[...])
    row = lax.broadcasted_iota(jnp.int32, (Lb, Lb), 0)
    col = lax.broadcasted_iota(jnp.int32, (Lb, Lb), 1)
    shift = int(math.log2(L))
    same_chunk = jnp.right_shift(row, shift) == jnp.right_shift(col, shift)
    gcum = _dot_f32(jnp.where(same_chunk & (row >= col), 1.0, 0.0), log_a)
    gtot = _dot_f32(jnp.where(same_chunk, 1.0, 0.0), log_a)
    lane = lax.broadcasted_iota(jnp.int32, (Lb, LANES), 1)
    zt = jnp.where(lane < GATE_B, gcum, beta).T

    rp = lax.broadcasted_iota(jnp.int32, (P, P), 0)
    cp = lax.broadcasted_iota(jnp.int32, (P, P), 1)
    same = jnp.right_shift(rp, shift) == jnp.right_shift(cp, shift)
    incl = same & (rp >= cp)
    strict = same & (rp > cp)

    def conv_cols(c0):
        xc = x_ref[:, c0:c0 + LANES]
        xp = jnp.concatenate([tail_scr[:, c0:c0 + LANES], xc], axis=0)
        y = xc * cw_ref[3:4, c0:c0 + LANES]
        for j in range(CONV_W - 1):
            y = y + xp[5 + j:5 + j + Lb, :] * cw_ref[j:j + 1, c0:c0 + LANES]
        return y * _sigmoid(y)

    heads = []
    for h in range(G_HEADS):
        qc = conv_cols(h * G_DK)
        kc = conv_cols(G_HEADS * G_DK + h * G_DK)
        vc = conv_cols(2 * G_HEADS * G_DK + h * G_DV)
        q2 = qc * (lax.rsqrt(jnp.sum(qc * qc, axis=1, keepdims=True) + EPS) * (G_DK ** -0.5))
        k2 = kc * lax.rsqrt(jnp.sum(kc * kc, axis=1, keepdims=True) + EPS)
        g_col = gcum[:, GATE_A + h:GATE_A + h + 1]
        t_col = gtot[:, GATE_A + h:GATE_A + h + 1]
        b_col = beta[:, GATE_B + h:GATE_B + h + 1]
        eg = jnp.exp(g_col)
        heads.append(dict(
            qb=q2.astype(BF16), kb=k2.astype(BF16), eg=eg, b_col=b_col,
            vek=jnp.concatenate([vc, eg * k2], axis=1),
            kd=(k2 * jnp.exp(t_col - g_col)).astype(BF16),
            dS=jnp.exp(t_col),
        ))

    blocks = []
    for h in range(G_HEADS):
        hd = heads[h]
        for b in range(n_pair):
            r0 = b * P
            kb = hd["kb"][r0:r0 + P]
            g_col = gcum[r0:r0 + P, GATE_A + h:GATE_A + h + 1]
            g_row = zt[GATE_A + h:GATE_A + h + 1, r0:r0 + P]
            b_row = zt[GATE_B + h:GATE_B + h + 1, r0:r0 + P]
            dec = jnp.exp(jnp.where(incl, g_col - g_row, -jnp.inf))
            a_mat = jnp.where(strict, dec * _dot_nt(kb, kb) * b_row, 0.0)
            qkd = (_dot_nt(hd["qb"][r0:r0 + P], kb) * dec).astype(BF16)
            blocks.append(dict(h=h, r0=r0, mpow=-a_mat, qacc=-a_mat, qkd=qkd))

    for blk in blocks:
        blk["mb"] = blk["mpow"].astype(BF16)
    for _ in range(int(math.log2(L)) - 1):
        for blk in blocks:
            blk["mpow"] = _dot(blk["mb"], blk["mb"])
            blk["mb"] = blk["mpow"].astype(BF16)
        for blk in blocks:
            blk["qacc"] = blk["qacc"] + blk["mpow"] + _dot(blk["qacc"].astype(BF16), blk["mb"])

    for blk in blocks:
        hd = heads[blk["h"]]
        r0 = blk["r0"]
        vek = hd["vek"][r0:r0 + P]
        t = (vek + _dot(blk["qacc"].astype(BF16), vek.astype(BF16))) * hd["b_col"][r0:r0 + P]
        blk["uv"] = t[:, :G_DV]
        blk["kw"] = t[:, G_DV:].astype(BF16)

    states = [s_scr[h] for h in range(G_HEADS)]
    wus = [[] for _ in range(G_HEADS)]
    qss = [[] for _ in range(G_HEADS)]
    for ci in range(Lb // L):
        r0 = ci * L
        for h in range(G_HEADS):
            hd = heads[h]
            blk = blocks[h * n_pair + r0 // P]
            o0 = r0 - blk["r0"]
            lhs = jnp.concatenate([blk["kw"][o0:o0 + L], hd["qb"][r0:r0 + L]], axis=0)
            r = _dot(lhs, states[h].astype(BF16))
            wu = (blk["uv"][o0:o0 + L] - r[:L]).astype(BF16)
            qss[h].append(r[L:])
            wus[h].append(wu)
            states[h] = hd["dS"][r0:r0 + 1] * states[h] + _dot_tn(hd["kd"][r0:r0 + L], wu)
    for h in range(G_HEADS):
        s_scr[h] = states[h]

    for h in range(G_HEADS):
        hd = heads[h]
        for b in range(n_pair):
            r0 = b * P
            blk = blocks[h * n_pair + b]
            i0 = r0 // L
            wu = jnp.concatenate(wus[h][i0:i0 + P // L], axis=0)
            qs = jnp.concatenate(qss[h][i0:i0 + P // L], axis=0)
            o = hd["eg"][r0:r0 + P] * qs + _dot(blk["qkd"], wu)
            y = _rms(o, gn_ref[:, h * G_DV:(h + 1) * G_DV])
            gz = gz_ref[r0:r0 + P, h * G_DV:(h + 1) * G_DV]
            og_ref[r0:r0 + P, h * G_DV:(h + 1) * G_DV] = (y * (gz * _sigmoid(gz))).astype(BF16)

    tail_scr[...] = x_ref[Lb - SUBLANES:Lb, :]

    @pl.when(c == pl.num_programs(1) - 1)
    def _():
        s_ref[...] = s_scr[...]
        cv_ref[...] = x_ref[Lb - (CONV_W - 1):Lb, :]


def _gdn_sample_kernel(pr_ref, gb_ref, al_ref, cw_ref, gn_ref, s_ref, cb_ref, acc_ref,
                       og_ref, so_ref, co_ref):
    del acc_ref
    nb = cb_ref.shape[2]
    log_a, beta = _gdn_gates(_token_rows(pr_ref, nb, GATE_ROW) + gb_ref[...], al_ref[...])
    eg8 = jnp.exp(log_a)

    ys = []
    for r in range(CONV_ROWS):
        xr = _token_rows(pr_ref, nb, ROW_GX + r)
        y = xr * cw_ref[CONV_W - 1, r:r + 1, :]
        for j in range(CONV_W - 1):
            y = y + cb_ref[j, r] * cw_ref[j, r:r + 1, :]
        co_ref[0, r] = cb_ref[1, r]
        co_ref[1, r] = cb_ref[2, r]
        co_ref[2, r] = xr
        ys.append(y * _sigmoid(y))

    outs = []
    for h in range(G_HEADS):
        qc, kc, v = ys[h], ys[G_HEADS + h], ys[2 * G_HEADS + h]
        q = qc * lax.rsqrt(jnp.sum(qc * qc, axis=1, keepdims=True) + EPS) * (G_DK ** -0.5)
        k = kc * lax.rsqrt(jnp.sum(kc * kc, axis=1, keepdims=True) + EPS)
        eg = eg8[:, GATE_A + h:GATE_A + h + 1]
        bt = beta[:, GATE_B + h:GATE_B + h + 1]
        qb, kb = q.astype(BF16), k.astype(BF16)
        ks, qs = [], []
        for s in range(nb):
            sb = s_ref[s, h].astype(BF16)
            ks.append(_dot(kb, sb))
            qs.append(_dot(qb, sb))
        u = bt * (v - eg * _pick_row(ks))
        o = eg * _pick_row(qs) + jnp.sum(q * k, axis=1, keepdims=True) * u
        k_cols = _columns([k])
        eg_rows = jnp.broadcast_to(eg, (nb, G_DV))
        for s in range(nb):
            so_ref[s, h] = eg_rows[s:s + 1, :] * s_ref[s, h] + k_cols[:, s:s + 1] * u[s:s + 1, :]
        gz = _token_rows(pr_ref, nb, ROW_GZ + h)
        outs.append(_rms(o, gn_ref[:, h * G_DV:(h + 1) * G_DV]) * (gz * _sigmoid(gz)))
    og_ref[...] = jnp.concatenate(outs, axis=1).astype(BF16)


def _gdn_kernel(*refs, every):
    p_in, s_in, p_out, s_out, scr = refs[:7], refs[7:15], refs[15:18], refs[18:21], refs[21:]
    _gdn_prompt_kernel(*p_in, *p_out, *scr)

    @pl.when(pl.program_id(1) % every == 0)
    def _():
        _gdn_sample_kernel(*s_in, *s_out)


def _gdn(proj3, proj_rows, gate_bias, alog, conv_w, conv_w_rows, g_norm, s_all, conv_t, s_acc, l):
    b, t, _ = proj3.shape
    bs = s_all.shape[1]
    Lb = G_BLOCK
    nb = SAMPLE_BLOCK
    dv = G_HEADS * G_DV
    blk, every = _sample_block_index(b, t // Lb, bs // nb)
    s_spec = _layer_spec((nb, G_HEADS, G_DK, G_DV), l, lambda i, c: (blk(i, c), 0, 0, 0))
    cv_shape = (CONV_W - 1, CONV_ROWS, nb, LANES)
    return pl.pallas_call(
        functools.partial(_gdn_kernel, every=every),
        grid=(b, t // Lb),
        in_specs=[
            pl.BlockSpec((None, Lb, G_CONV_CH), lambda i, c: (i, c, 1)),
            pl.BlockSpec((None, Lb, dv), lambda i, c: (i, c, 6)),
            pl.BlockSpec((None, Lb, LANES), lambda i, c: (i, c, GATE_ROW)),
            _layer_spec((1, LANES), l, lambda i, c: (0, 0)),
            _layer_spec((1, LANES), l, lambda i, c: (0, 0)),
            _layer_spec((CONV_W, G_CONV_CH), l, lambda i, c: (0, 0)),
            _layer_spec((1, dv), l, lambda i, c: (0, 0)),
            pl.BlockSpec((nb * ROWS_PER_TOKEN, LANES), lambda i, c: (blk(i, c), 0)),
            _layer_spec((1, LANES), l, lambda i, c: (0, 0)),
            _layer_spec((1, LANES), l, lambda i, c: (0, 0)),
            _layer_spec((CONV_W, CONV_ROWS, LANES), l, lambda i, c: (0, 0, 0)),
            _layer_spec((1, dv), l, lambda i, c: (0, 0)),
            s_spec,
            _layer_spec(cv_shape, l, lambda i, c: (0, 0, blk(i, c), 0)),
            pl.BlockSpec(memory_space=pl.ANY),
        ],
        out_specs=[
            pl.BlockSpec((None, Lb, dv), lambda i, c: (i, c, 0)),
            pl.BlockSpec((None, G_HEADS, G_DK, G_DV), lambda i, c: (i, 0, 0, 0)),
            pl.BlockSpec((None, CONV_W - 1, G_CONV_CH), lambda i, c: (i, 0, 0)),
            pl.BlockSpec((nb, dv), lambda i, c: (blk(i, c), 0)),
            s_spec,
            pl.BlockSpec(cv_shape, lambda i, c: (0, 0, blk(i, c), 0)),
        ],
        out_shape=[
            jax.ShapeDtypeStruct((b, t, dv), BF16),
            jax.ShapeDtypeStruct((b, G_HEADS, G_DK, G_DV), F32),
            jax.ShapeDtypeStruct((b, CONV_W - 1, G_CONV_CH), F32),
            jax.ShapeDtypeStruct((bs, dv), BF16),
            jax.ShapeDtypeStruct(s_all.shape, F32),
            jax.ShapeDtypeStruct((CONV_W - 1, CONV_ROWS, bs, LANES), F32),
        ],
        scratch_shapes=[
            pltpu.VMEM((G_HEADS, G_DK, G_DV), F32),
            pltpu.VMEM((SUBLANES, G_CONV_CH), F32),
        ],
        input_output_aliases={14: 4},
        compiler_params=_cparams("parallel", "arbitrary"),
        name="gdn",
    )(proj3, proj3, proj3, gate_bias, alog, conv_w, g_norm,
      proj_rows, gate_bias, alog, conv_w_rows, g_norm, s_all, conv_t, s_acc)


def _arrange_w_in_kernel(wt_ref, o_ref):
    n_m = 2 * M_HEADS * M_DK + 2 * M_HEADS * M_DV
    n_g = 4 * G_HEADS * G_DK
    g0 = n_m + 2 * M_HEADS
    tk = wt_ref.shape[1]
    step = 2 * LANES
    for dst, src, n in ((0, 0, n_m), (n_m, g0, n_g)):
        for c in range(0, n, step):
            o_ref[:, dst + c:dst + c + step] = wt_ref[src + c:src + c + step, :].T.astype(BF16)
    n_gate = wt_ref.shape[0] - N_WIDE
    gate = jnp.concatenate([wt_ref[n_m:g0, :], wt_ref[g0 + n_g:, :], jnp.zeros((LANES - n_gate, tk), F32)], axis=0)
    o_ref[:, N_WIDE:N_WIDE + LANES] = gate.T.astype(BF16)
    o_ref[:, N_WIDE + LANES:] = jnp.zeros((tk, N_PROJ - N_WIDE - LANES), BF16)


def _arrange_w_in(w_in, tk=256):
    depth, rows, n_in = w_in.shape
    return pl.pallas_call(
        _arrange_w_in_kernel,
        grid=(depth, rows // tk),
        in_specs=[pl.BlockSpec((None, n_in, tk), lambda l, i: (l, 0, i))],
        out_specs=pl.BlockSpec((None, tk, N_PROJ), lambda l, i: (l, i, 0)),
        out_shape=jax.ShapeDtypeStruct((depth, rows, N_PROJ), BF16),
        compiler_params=_cparams("parallel", "parallel"),
        name="arrange_w_in",
    )(jnp.swapaxes(w_in, 1, 2))


def _lane_row(parts, depth):
    row = jnp.zeros((depth, 1, LANES), F32)
    for off, arr in parts:
        row = row.at[:, 0, off:off + arr.shape[-1]].set(arr.astype(F32))
    return row


def kernel(x_prompt, x_sample, mem_prompt, state_mlstm_C, state_mlstm_n, state_mlstm_m, state_gdn_S,
           state_gdn_conv, cache_mem_k, cache_mem_v, norm_mix, w_in, mlstm_b_i, mlstm_b_f, mlstm_norm,
           gdn_conv_w, gdn_A_log, gdn_dt_bias, gdn_norm, w_out, norm_xattn, norm_mem, xattn_wq, xattn_wk,
           xattn_wv, xattn_wo, norm_ffn, ffn_w_gate, ffn_w_up, ffn_w_down, norm_final):
    bp, seq, _ = x_prompt.shape
    bs = x_sample.shape[0]
    depth = w_in.shape[0]
    dq = X_HEADS * X_DH
    tm_p, tm_wide = 512, 1024

    w_in_b = _arrange_w_in(w_in)
    gate_bias = _lane_row([(GATE_I, mlstm_b_i), (GATE_F, mlstm_b_f), (GATE_A, gdn_dt_bias)], depth)
    alog = _lane_row([(GATE_A, gdn_A_log)], depth)
    g_mix, g_x, g_mem, g_ffn = (g.reshape(depth, 1, D_MODEL) for g in (norm_mix, norm_xattn, norm_mem, norm_ffn))
    g_final = norm_final.reshape(1, D_MODEL)
    m_norm = mlstm_norm.reshape(depth, 1, M_HEADS * M_DV)
    g_norm = gdn_norm.reshape(depth, 1, G_HEADS * G_DV)
    conv_w_rows = gdn_conv_w.reshape(depth, CONV_W, CONV_ROWS, LANES)
    conv_t = jnp.transpose(state_gdn_conv.reshape(depth, bs, CONV_W - 1, CONV_ROWS, LANES), (0, 2, 3, 1, 4))
    n_all = state_mlstm_n.reshape(depth, bs, M_HEADS * M_DK)
    mem_k = cache_mem_k.reshape(depth, bs, N_MEM * X_HEADS, X_DH)
    mem_v = cache_mem_v.reshape(depth, bs, N_MEM * X_HEADS, X_DH)

    mk_all, mv_all = _mem_kv(mem_prompt.reshape(bp * N_MEM, D_MODEL), g_mem, xattn_wk, xattn_wv)

    xp = x_prompt.reshape(bp * seq, D_MODEL)
    xs = x_sample.reshape(bs, D_MODEL)
    c_acc = jnp.zeros(state_mlstm_C.shape, F32)
    s_acc = jnp.zeros(state_gdn_S.shape, F32)
    outs = [[] for _ in range(8)]
    for l in range(depth):
        proj_p, proj_s = _proj(xp, xs, g_mix, w_in_b, l, tm_wide)
        proj3 = proj_p.reshape(bp, seq, N_PROJ)
        proj_rows = proj_s.reshape(bs * ROWS_PER_TOKEN, LANES)
        hm_p, c_p, n_p, m_p, hm_s, c_acc, n_s, m_s = _mlstm(
            proj3, proj_rows, gate_bias, m_norm, state_mlstm_C, n_all, state_mlstm_m, c_acc, l)
        og_p, s_p, cv_p, og_s, s_acc, cv_s = _gdn(
            proj3, proj_rows, gate_bias, alog, gdn_conv_w, conv_w_rows, g_norm, state_gdn_S, conv_t, s_acc, l)
        for lst, val in zip(outs[:5], (c_p, n_p.reshape(bp, M_HEADS, M_DK), m_p[:, 0, :M_HEADS], s_p, cv_p)):
            lst.append(val)
        xp, xs = _mix_out(xp, xs, hm_p.reshape(bp * seq, -1), og_p.reshape(bp * seq, -1), hm_s, og_s, w_out, l, tm_p)
        xp, xs = _xattn(xp, xs, g_x, xattn_wq, mk_all, mv_all, mem_k, mem_v, xattn_wo, l, tm_p, seq)
        xp, xs = _ffn(xp, xs, g_ffn, ffn_w_gate, ffn_w_up, ffn_w_down, g_final, l, tm_wide, l == depth - 1)
        cv_s = jnp.transpose(cv_s, (2, 0, 1, 3)).reshape(bs, CONV_W - 1, G_CONV_CH)
        for lst, val in zip(outs[5:], (n_s.reshape(bs, M_HEADS, M_DK), m_s, cv_s)):
            lst.append(val)

    y_prompt = xp.reshape(bp, seq, D_MODEL)
    y_sample = xs.reshape(bs, 1, D_MODEL)
    pc, pn, pm, ps, pconv, sn, sm, sconv = (jnp.stack(o) for o in outs)
    mem_shape = (depth, bp, N_MEM, X_HEADS, X_DH)
    return (y_prompt, y_sample, pc, pn, pm, ps, pconv, mk_all.reshape(mem_shape), mv_all.reshape(mem_shape),
            c_acc, sn, sm, s_acc, sconv)
```

```python
import functools
import math

import jax
import jax.numpy as jnp
from jax import lax
from jax.experimental import pallas as pl
from jax.experimental.pallas import tpu as pltpu

F32 = jnp.float32
BF16 = jnp.bfloat16

D_MODEL = 2048
M_HEADS = 4
M_DK = 128
M_DV = 256
G_HEADS = 8
G_DK = 128
G_DV = 128
CONV_W = 4
G_CONV_CH = G_HEADS * (2 * G_DK + G_DV)
X_HEADS = 4
X_DH = 128
N_MEM = 256
D_FF = 5632
GATE_CAP = 15.0
EPS = 1e-6

LANES = 128
SUBLANES = 8
N_WIDE = 2 * M_HEADS * M_DK + 2 * M_HEADS * M_DV + 4 * G_HEADS * G_DK
PROJ_TN = 1280
N_PROJ = -(-(N_WIDE + LANES) // PROJ_TN) * PROJ_TN
ROWS_PER_TOKEN = N_PROJ // LANES
GATE_ROW = N_WIDE // LANES
GATE_I, GATE_F, GATE_A, GATE_B = 0, 4, 8, 16
ROW_MQ, ROW_MK, ROW_MV, ROW_MO = 0, 4, 8, 16
ROW_GX, ROW_GZ = 24, 48
CONV_ROWS = G_CONV_CH // LANES

VMEM_LIMIT = 56 * 1024 * 1024

M_CHUNK = 256
G_BLOCK = 256
G_CHUNK = 64
G_PAIR = 2 * G_CHUNK
SAMPLE_BLOCK = 8

NT_DIMS = (((1,), (1,)), ((), ()))
TN_DIMS = (((0,), (0,)), ((), ()))


def _cparams(*sem):
    return pltpu.CompilerParams(dimension_semantics=sem, vmem_limit_bytes=VMEM_LIMIT)


def _dot(a, b):
    return jnp.dot(a, b, preferred_element_type=F32)


def _dot_nt(a, b):
    return lax.dot_general(a, b, NT_DIMS, preferred_element_type=F32)


def _dot_tn(a, b):
    return lax.dot_general(a, b, TN_DIMS, preferred_element_type=F32)


def _dot_f32(a, b):
    return jnp.dot(a, b, preferred_element_type=F32, precision=lax.Precision.HIGHEST)


def _rms(x, g):
    ms = jnp.mean(x * x, axis=-1, keepdims=True)
    return x * lax.rsqrt(ms + EPS) * g


def _sigmoid(x):
    return 1.0 / (1.0 + jnp.exp(-x))


def _softplus(x):
    return jnp.maximum(x, 0.0) + jnp.log1p(jnp.exp(-jnp.abs(x)))


def _softcap(x):
    return GATE_CAP * jnp.tanh(x / GATE_CAP)


def _layer_spec(shape, l, tail):
    return pl.BlockSpec((None,) + shape, lambda *g: (l,) + tail(*g))


def _sample_share(m_prompt, m_sample, tm):
    tiles = m_prompt // tm
    rs = m_sample // tiles
    assert tiles * tm == m_prompt and tiles * rs == m_sample and rs % SUBLANES == 0, (m_prompt, m_sample, tm)
    return tiles, rs


def _proj_kernel(xp_ref, xs_ref, g_ref, w_ref, op_ref, os_ref, h_scr):
    tm = xp_ref.shape[0]

    @pl.when(pl.program_id(1) == 0)
    def _():
        h_scr[:tm] = _rms(xp_ref[...], g_ref[...]).astype(BF16)
        h_scr[tm:] = _rms(xs_ref[...], g_ref[...]).astype(BF16)

    r = _dot(h_scr[...], w_ref[...])
    op_ref[...] = r[:tm]
    os_ref[...] = r[tm:]


def _proj(xp, xs, g, w, l, tm):
    mp, ms = xp.shape[0], xs.shape[0]
    tiles, rs = _sample_share(mp, ms, tm)
    return pl.pallas_call(
        _proj_kernel,
        grid=(tiles, N_PROJ // PROJ_TN),
        in_specs=[
            pl.BlockSpec((tm, D_MODEL), lambda i, j: (i, 0)),
            pl.BlockSpec((rs, D_MODEL), lambda i, j: (i, 0)),
            _layer_spec((1, D_MODEL), l, lambda i, j: (0, 0)),
            _layer_spec((D_MODEL, PROJ_TN), l, lambda i, j: (0, j)),
        ],
        out_specs=[pl.BlockSpec((tm, PROJ_TN), lambda i, j: (i, j)), pl.BlockSpec((rs, PROJ_TN), lambda i, j: (i, j))],
        out_shape=[jax.ShapeDtypeStruct((mp, N_PROJ), F32), jax.ShapeDtypeStruct((ms, N_PROJ), F32)],
        scratch_shapes=[pltpu.VMEM((tm + rs, D_MODEL), BF16)],
        compiler_params=_cparams("parallel", "arbitrary"),
        name="proj",
    )(xp, xs, g, w)


def _mem_kv_kernel(x_ref, g_ref, wk_ref, wv_ref, k_ref, v_ref):
    h = _rms(x_ref[...], g_ref[...]).astype(BF16)
    k_ref[...] = _dot(h, wk_ref[...].astype(BF16))
    v_ref[...] = _dot(h, wv_ref[...].astype(BF16))


def _mem_kv(mem, g, wk, wv):
    depth = wk.shape[0]
    m = mem.shape[0]
    dq = X_HEADS * X_DH
    tm = N_MEM
    w_spec = pl.BlockSpec((None, D_MODEL, dq), lambda l, i: (l, 0, 0))
    o_spec = pl.BlockSpec((None, tm, dq), lambda l, i: (l, i, 0))
    return pl.pallas_call(
        _mem_kv_kernel,
        grid=(depth, m // tm),
        in_specs=[
            pl.BlockSpec((tm, D_MODEL), lambda l, i: (i, 0)),
            pl.BlockSpec((None, 1, D_MODEL), lambda l, i: (l, 0, 0)),
            w_spec, w_spec,
        ],
        out_specs=[o_spec, o_spec],
        out_shape=[jax.ShapeDtypeStruct((depth, m, dq), F32)] * 2,
        compiler_params=_cparams("parallel", "parallel"),
        name="mem_kv",
    )(mem, g, wk, wv)


def _mix_out_kernel(xp_ref, xs_ref, hmp_ref, ogp_ref, hms_ref, ogs_ref, w1_ref, w2_ref, op_ref, os_ref):
    tm = xp_ref.shape[0]
    hm = jnp.concatenate([hmp_ref[...], hms_ref[...]], axis=0)
    og = jnp.concatenate([ogp_ref[...], ogs_ref[...]], axis=0)
    acc = _dot(hm, w1_ref[...].astype(BF16))
    acc = acc + _dot(og, w2_ref[...].astype(BF16))
    op_ref[...] = xp_ref[...] + acc[:tm]
    os_ref[...] = xs_ref[...] + acc[tm:]


def _mix_out(xp, xs, hmp, ogp, hms, ogs, w_out, l, tm):
    mp, ms = xp.shape[0], xs.shape[0]
    tiles, rs = _sample_share(mp, ms, tm)
    half = D_MODEL // 2
    return pl.pallas_call(
        _mix_out_kernel,
        grid=(tiles,),
        in_specs=[
            pl.BlockSpec((tm, D_MODEL), lambda i: (i, 0)),
            pl.BlockSpec((rs, D_MODEL), lambda i: (i, 0)),
            pl.BlockSpec((tm, half), lambda i: (i, 0)),
            pl.BlockSpec((tm, half), lambda i: (i, 0)),
            pl.BlockSpec((rs, half), lambda i: (i, 0)),
            pl.BlockSpec((rs, half), lambda i: (i, 0)),
            _layer_spec((half, D_MODEL), l, lambda i: (0, 0)),
            _layer_spec((half, D_MODEL), l, lambda i: (1, 0)),
        ],
        out_specs=[pl.BlockSpec((tm, D_MODEL), lambda i: (i, 0)), pl.BlockSpec((rs, D_MODEL), lambda i: (i, 0))],
        out_shape=[jax.ShapeDtypeStruct((mp, D_MODEL), F32), jax.ShapeDtypeStruct((ms, D_MODEL), F32)],
        compiler_params=_cparams("parallel"),
        name="mix_out",
    )(xp, xs, hmp, ogp, hms, ogs, w_out, w_out)


def _ffn_kernel(xp_ref, xs_ref, g_ref, wg_ref, wu_ref, wd_ref, gf_ref, op_ref, os_ref, h_scr, *, final):
    j = pl.program_id(1)
    tm = xp_ref.shape[0]

    @pl.when(j == 0)
    def _():
        for x_ref, o_ref, rows in ((xp_ref, op_ref, slice(0, tm)), (xs_ref, os_ref, slice(tm, None))):
            x = x_ref[...]
            h_scr[rows] = _rms(x, g_ref[...]).astype(BF16)
            o_ref[...] = x

    h = h_scr[...]
    a = _dot(h, wg_ref[...].astype(BF16))
    b = _dot(h, wu_ref[...].astype(BF16))
    t = (a * _sigmoid(a)) * b
    r = _dot(t.astype(BF16), wd_ref[...].astype(BF16))
    op_ref[...] += r[:tm]
    os_ref[...] += r[tm:]

    if final:
        @pl.when(j == pl.num_programs(1) - 1)
        def _():
            op_ref[...] = _rms(op_ref[...], gf_ref[...])
            os_ref[...] = _rms(os_ref[...], gf_ref[...])


def _ffn(xp, xs, g, w_gate, w_up, w_down, g_final, l, tm, final, tf=512):
    mp, ms = xp.shape[0], xs.shape[0]
    tiles, rs = _sample_share(mp, ms, tm)
    return pl.pallas_call(
        functools.partial(_ffn_kernel, final=final),
        grid=(tiles, D_FF // tf),
        in_specs=[
            pl.BlockSpec((tm, D_MODEL), lambda i, j: (i, 0), pipeline_mode=pl.Buffered(1)),
            pl.BlockSpec((rs, D_MODEL), lambda i, j: (i, 0)),
            _layer_spec((1, D_MODEL), l, lambda i, j: (0, 0)),
            _layer_spec((D_MODEL, tf), l, lambda i, j: (0, j)),
            _layer_spec((D_MODEL, tf), l, lambda i, j: (0, j)),
            _layer_spec((tf, D_MODEL), l, lambda i, j: (j, 0)),
            pl.BlockSpec((1, D_MODEL), lambda i, j: (0, 0)),
        ],
        out_specs=[pl.BlockSpec((tm, D_MODEL), lambda i, j: (i, 0), pipeline_mode=pl.Buffered(1)),
                   pl.BlockSpec((rs, D_MODEL), lambda i, j: (i, 0))],
        out_shape=[jax.ShapeDtypeStruct((mp, D_MODEL), F32), jax.ShapeDtypeStruct((ms, D_MODEL), F32)],
        scratch_shapes=[pltpu.VMEM((tm + rs, D_MODEL), BF16)],
        compiler_params=_cparams("parallel", "arbitrary"),
        name="ffn",
    )(xp, xs, g, w_gate, w_up, w_down, g_final)


def _softmax_rows(s):
    s = s - jnp.max(s, axis=-1, keepdims=True)
    p = jnp.exp(s)
    return p / jnp.sum(p, axis=-1, keepdims=True)


def _prompt_attention(q, k_ref, v_ref):
    heads = []
    for h in range(X_HEADS):
        sl = slice(h * X_DH, (h + 1) * X_DH)
        s = _dot_nt(q[:, sl].astype(BF16), k_ref[:, sl].astype(BF16)) * (X_DH ** -0.5)
        p = _softmax_rows(s)
        heads.append(_dot(p.astype(BF16), v_ref[:, sl].astype(BF16)))
    return jnp.concatenate(heads, axis=1)


def _sample_attention(q, k_ref, v_ref):
    nb = q.shape[0]
    nk = N_MEM * X_HEADS
    qh = [q[:, h * X_DH:(h + 1) * X_DH] for h in range(X_HEADS)]
    row = lax.broadcasted_iota(jnp.int32, (SUBLANES, nk), 0)
    col = lax.broadcasted_iota(jnp.int32, (SUBLANES, nk), 1)
    own = (col & (X_HEADS - 1)) == (row & (X_HEADS - 1))
    row8 = lax.broadcasted_iota(jnp.int32, (SUBLANES, X_DH), 0)
    outs = [jnp.zeros((nb, X_DH), F32) for _ in range(X_HEADS)]
    scores = []
    for s in range(nb):
        q8 = jnp.zeros((SUBLANES, X_DH), F32)
        for h in range(X_HEADS):
            q8 = jnp.where(row8 == h, qh[h][s:s + 1, :], q8)
        scores.append(_dot_nt(q8.astype(BF16), k_ref[s].astype(BF16)) * (X_DH ** -0.5))
    probs = [_softmax_rows(jnp.where(own, sc, -jnp.inf)).astype(BF16) for sc in scores]
    o8s = [_dot(probs[s], v_ref[s].astype(BF16)) for s in range(nb)]
    for s in range(nb):
        for h in range(X_HEADS):
            outs[h] = jnp.where(row8[:nb] == s, o8s[s][h:h + 1, :], outs[h])
    return jnp.concatenate(outs, axis=1)


def _xattn_kernel(xp_ref, xs_ref, g_ref, wq_ref, kp_ref, vp_ref, ks_ref, vs_ref, wo_ref, op_ref, os_ref):
    tm = xp_ref.shape[0]
    h = jnp.concatenate([_rms(xp_ref[...], g_ref[...]), _rms(xs_ref[...], g_ref[...])], axis=0)
    q = _dot(h.astype(BF16), wq_ref[...].astype(BF16))
    o = jnp.concatenate([_prompt_attention(q[:tm], kp_ref, vp_ref), _sample_attention(q[tm:], ks_ref, vs_ref)],
                        axis=0)
    r = _dot(o.astype(BF16), wo_ref[...].astype(BF16))
    op_ref[...] = xp_ref[...] + r[:tm]
    os_ref[...] = xs_ref[...] + r[tm:]


def _xattn(xp, xs, g, wq, mk_p, mv_p, mk_s, mv_s, wo, l, tm, seq):
    mp, ms = xp.shape[0], xs.shape[0]
    tiles, rs = _sample_share(mp, ms, tm)
    dq = X_HEADS * X_DH
    per_seq = seq // tm
    kvp_spec = pl.BlockSpec((None, N_MEM, dq), lambda i: (l, i // per_seq, 0))
    kvs_spec = _layer_spec((rs, N_MEM * X_HEADS, X_DH), l, lambda i: (i, 0, 0))
    return pl.pallas_call(
        _xattn_kernel,
        grid=(tiles,),
        in_specs=[
            pl.BlockSpec((tm, D_MODEL), lambda i: (i, 0)),
            pl.BlockSpec((rs, D_MODEL), lambda i: (i, 0)),
            _layer_spec((1, D_MODEL), l, lambda i: (0, 0)),
            _layer_spec((D_MODEL, dq), l, lambda i: (0, 0)),
            kvp_spec, kvp_spec, kvs_spec, kvs_spec,
            _layer_spec((dq, D_MODEL), l, lambda i: (0, 0)),
        ],
        out_specs=[pl.BlockSpec((tm, D_MODEL), lambda i: (i, 0)), pl.BlockSpec((rs, D_MODEL), lambda i: (i, 0))],
        out_shape=[jax.ShapeDtypeStruct((mp, D_MODEL), F32), jax.ShapeDtypeStruct((ms, D_MODEL), F32)],
        compiler_params=_cparams("parallel"),
        name="xattn",
    )(xp, xs, g, wq, mk_p, mv_p, mk_s, mv_s, wo)


def _mlstm_prompt_kernel(q_ref, k_ref, v_ref, mo_ref, gt_ref, gb_ref, mn_ref,
                         hm_ref, c_ref, n_ref, m_ref, c_scr, n_scr, m_scr):
    L = q_ref.shape[0]
    c = pl.program_id(1)

    @pl.when(c == 0)
    def _():
        c_scr[...] = jnp.zeros_like(c_scr)
        n_scr[...] = jnp.zeros_like(n_scr)
        m_scr[...] = jnp.zeros_like(m_scr)

    gates = gt_ref[...] + gb_ref[...]
    capped = _softcap(gates)
    logf = -_softplus(-capped)
    row = lax.broadcasted_iota(jnp.int32, (L, L), 0)
    col = lax.broadcasted_iota(jnp.int32, (L, L), 1)
    causal = row >= col
    bcum = _dot_f32(causal.astype(F32), logf)
    lane = lax.broadcasted_iota(jnp.int32, (L, LANES), 1)
    z = jnp.where(lane < GATE_F, capped, bcum)
    zt = z.T

    hs = range(M_HEADS)
    i_col = [z[:, GATE_I + h:GATE_I + h + 1] for h in hs]
    b_col = [z[:, GATE_F + h:GATE_F + h + 1] for h in hs]
    m_prev = [m_scr[0:1, h:h + 1] for h in hs]
    d = [jnp.where(causal, b_col[h] + (zt[GATE_I + h:GATE_I + h + 1, :] - zt[GATE_F + h:GATE_F + h + 1, :]),
                   -jnp.inf) for h in hs]
    inter = [b_col[h] + m_prev[h] for h in hs]
    m_t = [jnp.maximum(inter[h], jnp.max(d[h], axis=1, keepdims=True)) for h in hs]
    q = [q_ref[:, h * M_DK:(h + 1) * M_DK] for h in hs]
    k = [k_ref[:, h * M_DK:(h + 1) * M_DK] * (M_DK ** -0.5) for h in hs]
    qb = [q[h].astype(BF16) for h in hs]
    vb = [v_ref[:, h * M_DV:(h + 1) * M_DV].astype(BF16) for h in hs]
    qk = [_dot_nt(qb[h], k[h].astype(BF16)) for h in hs]
    qc = [_dot(qb[h], c_scr[h].astype(BF16)) for h in hs]
    w = [jnp.exp(d[h] - m_t[h]) * qk[h] for h in hs]
    a = [jnp.exp(inter[h] - m_t[h]) for h in hs]
    wv = [_dot(w[h].astype(BF16), vb[h]) for h in hs]
    nq = [a[h] * jnp.sum(q[h] * n_scr[h], axis=1, keepdims=True) + jnp.sum(w[h], axis=1, keepdims=True) for h in hs]
    hh = [(a[h] * qc[h] + wv[h]) / jnp.maximum(jnp.abs(nq[h]), jnp.exp(-m_t[h])) for h in hs]
    for h in hs:
        y = _rms(hh[h], mn_ref[:, h * M_DV:(h + 1) * M_DV])
        y = y * _sigmoid(mo_ref[:, h * M_DV:(h + 1) * M_DV])
        hm_ref[:, h * M_DV:(h + 1) * M_DV] = y.astype(BF16)

    m_new = [m_t[h][L - 1:L, :] for h in hs]
    b_last = [b_col[h][L - 1:L, :] for h in hs]
    kw = [k[h] * jnp.exp(b_last[h] - b_col[h] + i_col[h] - m_new[h]) for h in hs]
    upd = [_dot_tn(kw[h].astype(BF16), vb[h]) for h in hs]
    for h in hs:
        decay = jnp.exp(b_last[h] + m_prev[h] - m_new[h])
        c_scr[h] = decay * c_scr[h] + upd[h]
        n_scr[h] = decay * n_scr[h] + jnp.sum(kw[h], axis=0, keepdims=True)
        m_scr[0:1, h:h + 1] = m_new[h]

    @pl.when(c == pl.num_programs(1) - 1)
    def _():
        c_ref[...] = c_scr[...]
        n_ref[...] = n_scr[...]
        m_ref[...] = m_scr[...]


def _sample_block_index(b, nc, n_blocks):
    every = b * nc // n_blocks
    assert every * n_blocks == b * nc and nc % every == 0, (b, nc, n_blocks)
    per = nc // every
    return (lambda i, c: i * per + c // every), every


def _token_rows(pr_ref, nb, off):
    return pr_ref[pl.ds(off, nb, stride=ROWS_PER_TOKEN), :]


def _token_rows2(pr_ref, nb, off):
    return jnp.concatenate([_token_rows(pr_ref, nb, off), _token_rows(pr_ref, nb, off + 1)], axis=1)


def _pick_row(tiles):
    rid = lax.broadcasted_iota(jnp.int32, tiles[0].shape, 0)
    out = tiles[0]
    for s in range(1, len(tiles)):
        out = jnp.where(rid == s, tiles[s], out)
    return out


def _columns(tiles):
    pad = LANES - SUBLANES * len(tiles)
    stack = jnp.concatenate(tiles + ([jnp.zeros((pad, LANES), F32)] if pad else []), axis=0)
    return stack.T


def _mlstm_sample_kernel(pr_ref, gb_ref, mn_ref, c_ref, n_ref, m_ref, acc_ref,
                         hm_ref, co_ref, no_ref, mo_ref):
    del acc_ref
    nb = m_ref.shape[0]
    gates = _token_rows(pr_ref, nb, GATE_ROW) + gb_ref[...]
    capped = _softcap(gates)
    logf = -_softplus(-capped)
    i4 = capped[:, GATE_I:GATE_I + M_HEADS]
    f4 = logf[:, GATE_F:GATE_F + M_HEADS]
    inter = f4 + m_ref[...]
    m_new = jnp.maximum(inter, i4)
    a4 = jnp.exp(inter - m_new)
    w4 = jnp.exp(i4 - m_new)
    e4 = jnp.exp(-m_new)
    mo_ref[...] = m_new

    scale = M_DK ** -0.5
    ys, n_rows = [], []
    for h in range(M_HEADS):
        a, wi, e = a4[:, h:h + 1], w4[:, h:h + 1], e4[:, h:h + 1]
        q = _token_rows(pr_ref, nb, ROW_MQ + h)
        k = _token_rows(pr_ref, nb, ROW_MK + h) * scale
        v = _token_rows2(pr_ref, nb, ROW_MV + 2 * h)
        og = _token_rows2(pr_ref, nb, ROW_MO + 2 * h)
        n_old = n_ref[:, h * M_DK:(h + 1) * M_DK]
        wk = wi * k
        n_rows.append(a * n_old + wk)
        wts = wi * jnp.sum(q * k, axis=1, keepdims=True)
        nq = a * jnp.sum(q * n_old, axis=1, keepdims=True) + wts
        wk_cols = _columns([wk])
        a_rows = jnp.broadcast_to(a, (nb, M_DV))
        qb = q.astype(BF16)
        qc = []
        for s in range(nb):
            c_old = c_ref[s, h]
            qc.append(_dot(qb, c_old.astype(BF16)))
            co_ref[s, h] = a_rows[s:s + 1, :] * c_old + wk_cols[:, s:s + 1] * v[s:s + 1, :]
        num = a * _pick_row(qc) + wts * v
        hh = num / jnp.maximum(jnp.abs(nq), e)
        ys.append(_rms(hh, mn_ref[:, h * M_DV:(h + 1) * M_DV]) * _sigmoid(og))
    hm_ref[...] = jnp.concatenate(ys, axis=1).astype(BF16)
    no_ref[...] = jnp.concatenate(n_rows, axis=1)


def _mlstm_kernel(*refs, every):
    p_in, s_in, p_out, s_out, scr = refs[:7], refs[7:14], refs[14:18], refs[18:22], refs[22:]
    _mlstm_prompt_kernel(*p_in, *p_out, *scr)

    @pl.when(pl.program_id(1) % every == 0)
    def _():
        _mlstm_sample_kernel(*s_in, *s_out)


def _mlstm(proj3, proj_rows, gate_bias, m_norm, c_all, n_all, m_all, c_acc, l):
    b, t, _ = proj3.shape
    bs = m_all.shape[1]
    L = M_CHUNK
    nb = SAMPLE_BLOCK
    dqk = M_HEADS * M_DK
    dv = M_HEADS * M_DV
    blk, every = _sample_block_index(b, t // L, bs // nb)
    c_spec = _layer_spec((nb, M_HEADS, M_DK, M_DV), l, lambda i, c: (blk(i, c), 0, 0, 0))
    return pl.pallas_call(
        functools.partial(_mlstm_kernel, every=every),
        grid=(b, t // L),
        in_specs=[
            pl.BlockSpec((None, L, dqk), lambda i, c: (i, c, 0)),
            pl.BlockSpec((None, L, dqk), lambda i, c: (i, c, 1)),
            pl.BlockSpec((None, L, dv), lambda i, c: (i, c, 1)),
            pl.BlockSpec((None, L, dv), lambda i, c: (i, c, 2)),
            pl.BlockSpec((None, L, LANES), lambda i, c: (i, c, GATE_ROW)),
            _layer_spec((1, LANES), l, lambda i, c: (0, 0)),
            _layer_spec((1, dv), l, lambda i, c: (0, 0)),
            pl.BlockSpec((nb * ROWS_PER_TOKEN, LANES), lambda i, c: (blk(i, c), 0)),
            _layer_spec((1, LANES), l, lambda i, c: (0, 0)),
            _layer_spec((1, dv), l, lambda i, c: (0, 0)),
            c_spec,
            _layer_spec((nb, dqk), l, lambda i, c: (blk(i, c), 0)),
            _layer_spec((nb, M_HEADS), l, lambda i, c: (blk(i, c), 0)),
            pl.BlockSpec(memory_space=pl.ANY),
        ],
        out_specs=[
            pl.BlockSpec((None, L, dv), lambda i, c: (i, c, 0)),
            pl.BlockSpec((None, M_HEADS, M_DK, M_DV), lambda i, c: (i, 0, 0, 0)),
            pl.BlockSpec((None, M_HEADS, 1, M_DK), lambda i, c: (i, 0, 0, 0)),
            pl.BlockSpec((None, 1, LANES), lambda i, c: (i, 0, 0)),
            pl.BlockSpec((nb, dv), lambda i, c: (blk(i, c), 0)),
            c_spec,
            pl.BlockSpec((nb, dqk), lambda i, c: (blk(i, c), 0)),
            pl.BlockSpec((nb, M_HEADS), lambda i, c: (blk(i, c), 0)),
        ],
        out_shape=[
            jax.ShapeDtypeStruct((b, t, dv), BF16),
            jax.ShapeDtypeStruct((b, M_HEADS, M_DK, M_DV), F32),
            jax.ShapeDtypeStruct((b, M_HEADS, 1, M_DK), F32),
            jax.ShapeDtypeStruct((b, 1, LANES), F32),
            jax.ShapeDtypeStruct((bs, dv), BF16),
            jax.ShapeDtypeStruct(c_all.shape, F32),
            jax.ShapeDtypeStruct((bs, dqk), F32),
            jax.ShapeDtypeStruct((bs, M_HEADS), F32),
        ],
        scratch_shapes=[
            pltpu.VMEM((M_HEADS, M_DK, M_DV), F32),
            pltpu.VMEM((M_HEADS, 1, M_DK), F32),
            pltpu.VMEM((1, LANES), F32),
        ],
        input_output_aliases={13: 5},
        compiler_params=_cparams("parallel", "arbitrary"),
        name="mlstm",
    )(proj3, proj3, proj3, proj3, proj3, gate_bias, m_norm,
      proj_rows, gate_bias, m_norm, c_all, n_all, m_all, c_acc)


def _gdn_gates(gates, alog):
    log_a = -jnp.exp(alog) * _softplus(gates)
    beta = _sigmoid(gates)
    return log_a, beta


def _gdn_prompt_kernel(x_ref, gz_ref, gt_ref, gb_ref, al_ref, cw_ref, gn_ref,
                       og_ref, s_ref, cv_ref, s_scr, tail_scr):
    Lb = x_ref.shape[0]
    L, P = G_CHUNK, G_PAIR
    n_pair = Lb // P
    c = pl.program_id(1)

    @pl.when(c == 0)
    def _():
        s_scr[...] = jnp.zeros_like(s_scr)
        tail_scr[...] = jnp.zeros_like(tail_scr)

    log_a, beta = _gdn_gates(gt_ref[...] + gb_ref[...], al_ref[...])
    row = lax.broadcasted_iota(jnp.int32, (Lb, Lb), 0)
    col = lax.broadcasted_iota(jnp.int32, (Lb, Lb), 1)
    shift = int(math.log2(L))
    same_chunk = jnp.right_shift(row, shift) == jnp.right_shift(col, shift)
    gcum = _dot_f32(jnp.where(same_chunk & (row >= col), 1.0, 0.0), log_a)
    gtot = _dot_f32(jnp.where(same_chunk, 1.0, 0.0), log_a)
    lane = lax.broadcasted_iota(jnp.int32, (Lb, LANES), 1)
    zt = jnp.where(lane < GATE_B, gcum, beta).T

    rp = lax.broadcasted_iota(jnp.int32, (P, P), 0)
    cp = lax.broadcasted_iota(jnp.int32, (P, P), 1)
    same = jnp.right_shift(rp, shift) == jnp.right_shift(cp, shift)
    incl = same & (rp >= cp)
    strict = same & (rp > cp)

    def conv_cols(c0):
        xc = x_ref[:, c0:c0 + LANES]
        xp = jnp.concatenate([tail_scr[:, c0:c0 + LANES], xc], axis=0)
        y = xc * cw_ref[3:4, c0:c0 + LANES]
        for j in range(CONV_W - 1):
            y = y + xp[5 + j:5 + j + Lb, :] * cw_ref[j:j + 1, c0:c0 + LANES]
        return y * _sigmoid(y)

    heads = []
    for h in range(G_HEADS):
        qc = conv_cols(h * G_DK)
        kc = conv_cols(G_HEADS * G_DK + h * G_DK)
        vc = conv_cols(2 * G_HEADS * G_DK + h * G_DV)
        q2 = qc * (lax.rsqrt(jnp.sum(qc * qc, axis=1, keepdims=True) + EPS) * (G_DK ** -0.5))
        k2 = kc * lax.rsqrt(jnp.sum(kc * kc, axis=1, keepdims=True) + EPS)
        g_col = gcum[:, GATE_A + h:GATE_A + h + 1]
        t_col = gtot[:, GATE_A + h:GATE_A + h + 1]
        b_col = beta[:, GATE_B + h:GATE_B + h + 1]
        eg = jnp.exp(g_col)
        heads.append(dict(
            qb=q2.astype(BF16), kb=k2.astype(BF16), eg=eg, b_col=b_col,
            vek=jnp.concatenate([vc, eg * k2], axis=1),
            kd=(k2 * jnp.exp(t_col - g_col)).astype(BF16),
            dS=jnp.exp(t_col),
        ))

    blocks = []
    for h in range(G_HEADS):
        hd = heads[h]
        for b in range(n_pair):
            r0 = b * P
            kb = hd["kb"][r0:r0 + P]
            g_col = gcum[r0:r0 + P, GATE_A + h:GATE_A + h + 1]
            g_row = zt[GATE_A + h:GATE_A + h + 1, r0:r0 + P]
            b_row = zt[GATE_B + h:GATE_B + h + 1, r0:r0 + P]
            dec = jnp.exp(jnp.where(incl, g_col - g_row, -jnp.inf))
            a_mat = jnp.where(strict, dec * _dot_nt(kb, kb) * b_row, 0.0)
            qkd = (_dot_nt(hd["qb"][r0:r0 + P], kb) * dec).astype(BF16)
            blocks.append(dict(h=h, r0=r0, mpow=-a_mat, qacc=-a_mat, qkd=qkd))

    for blk in blocks:
        blk["mb"] = blk["mpow"].astype(BF16)
    for _ in range(int(math.log2(L)) - 1):
        for blk in blocks:
            blk["mpow"] = _dot(blk["mb"], blk["mb"])
            blk["mb"] = blk["mpow"].astype(BF16)
        for blk in blocks:
            blk["qacc"] = blk["qacc"] + blk["mpow"] + _dot(blk["qacc"].astype(BF16), blk["mb"])

    for blk in blocks:
        hd = heads[blk["h"]]
        r0 = blk["r0"]
        vek = hd["vek"][r0:r0 + P]
        t = (vek + _dot(blk["qacc"].astype(BF16), vek.astype(BF16))) * hd["b_col"][r0:r0 + P]
        blk["uv"] = t[:, :G_DV]
        blk["kw"] = t[:, G_DV:].astype(BF16)

    states = [s_scr[h] for h in range(G_HEADS)]
    wus = [[] for _ in range(G_HEADS)]
    qss = [[] for _ in range(G_HEADS)]
    for ci in range(Lb // L):
        r0 = ci * L
        for h in range(G_HEADS):
            hd = heads[h]
            blk = blocks[h * n_pair + r0 // P]
            o0 = r0 - blk["r0"]
            lhs = jnp.concatenate([blk["kw"][o0:o0 + L], hd["qb"][r0:r0 + L]], axis=0)
            r = _dot(lhs, states[h].astype(BF16))
            wu = (blk["uv"][o0:o0 + L] - r[:L]).astype(BF16)
            qss[h].append(r[L:])
            wus[h].append(wu)
            states[h] = hd["dS"][r0:r0 + 1] * states[h] + _dot_tn(hd["kd"][r0:r0 + L], wu)
    for h in range(G_HEADS):
        s_scr[h] = states[h]

    for h in range(G_HEADS):
        hd = heads[h]
        for b in range(n_pair):
            r0 = b * P
            blk = blocks[h * n_pair + b]
            i0 = r0 // L
            wu = jnp.concatenate(wus[h][i0:i0 + P // L], axis=0)
            qs = jnp.concatenate(qss[h][i0:i0 + P // L], axis=0)
            o = hd["eg"][r0:r0 + P] * qs + _dot(blk["qkd"], wu)
            y = _rms(o, gn_ref[:, h * G_DV:(h + 1) * G_DV])
            gz = gz_ref[r0:r0 + P, h * G_DV:(h + 1) * G_DV]
            og_ref[r0:r0 + P, h * G_DV:(h + 1) * G_DV] = (y * (gz * _sigmoid(gz))).astype(BF16)

    tail_scr[...] = x_ref[Lb - SUBLANES:Lb, :]

    @pl.when(c == pl.num_programs(1) - 1)
    def _():
        s_ref[...] = s_scr[...]
        cv_ref[...] = x_ref[Lb - (CONV_W - 1):Lb, :]


def _gdn_sample_kernel(pr_ref, gb_ref, al_ref, cw_ref, gn_ref, s_ref, cb_ref, acc_ref,
                       og_ref, so_ref, co_ref):
    del acc_ref
    nb = cb_ref.shape[2]
    log_a, beta = _gdn_gates(_token_rows(pr_ref, nb, GATE_ROW) + gb_ref[...], al_ref[...])
    eg8 = jnp.exp(log_a)

    ys = []
    for r in range(CONV_ROWS):
        xr = _token_rows(pr_ref, nb, ROW_GX + r)
        y = xr * cw_ref[CONV_W - 1, r:r + 1, :]
        for j in range(CONV_W - 1):
            y = y + cb_ref[j, r] * cw_ref[j, r:r + 1, :]
        co_ref[0, r] = cb_ref[1, r]
        co_ref[1, r] = cb_ref[2, r]
        co_ref[2, r] = xr
        ys.append(y * _sigmoid(y))

    outs = []
    for h in range(G_HEADS):
        qc, kc, v = ys[h], ys[G_HEADS + h], ys[2 * G_HEADS + h]
        q = qc * lax.rsqrt(jnp.sum(qc * qc, axis=1, keepdims=True) + EPS) * (G_DK ** -0.5)
        k = kc * lax.rsqrt(jnp.sum(kc * kc, axis=1, keepdims=True) + EPS)
        eg = eg8[:, GATE_A + h:GATE_A + h + 1]
        bt = beta[:, GATE_B + h:GATE_B + h + 1]
        qb, kb = q.astype(BF16), k.astype(BF16)
        ks, qs = [], []
        for s in range(nb):
            sb = s_ref[s, h].astype(BF16)
            ks.append(_dot(kb, sb))
            qs.append(_dot(qb, sb))
        u = bt * (v - eg * _pick_row(ks))
        o = eg * _pick_row(qs) + jnp.sum(q * k, axis=1, keepdims=True) * u
        k_cols = _columns([k])
        eg_rows = jnp.broadcast_to(eg, (nb, G_DV))
        for s in range(nb):
            so_ref[s, h] = eg_rows[s:s + 1, :] * s_ref[s, h] + k_cols[:, s:s + 1] * u[s:s + 1, :]
        gz = _token_rows(pr_ref, nb, ROW_GZ + h)
        outs.append(_rms(o, gn_ref[:, h * G_DV:(h + 1) * G_DV]) * (gz * _sigmoid(gz)))
    og_ref[...] = jnp.concatenate(outs, axis=1).astype(BF16)


def _gdn_kernel(*refs, every):
    p_in, s_in, p_out, s_out, scr = refs[:7], refs[7:15], refs[15:18], refs[18:21], refs[21:]
    _gdn_prompt_kernel(*p_in, *p_out, *scr)

    @pl.when(pl.program_id(1) % every == 0)
    def _():
        _gdn_sample_kernel(*s_in, *s_out)


def _gdn(proj3, proj_rows, gate_bias, alog, conv_w, conv_w_rows, g_norm, s_all, conv_t, s_acc, l):
    b, t, _ = proj3.shape
    bs = s_all.shape[1]
    Lb = G_BLOCK
    nb = SAMPLE_BLOCK
    dv = G_HEADS * G_DV
    blk, every = _sample_block_index(b, t // Lb, bs // nb)
    s_spec = _layer_spec((nb, G_HEADS, G_DK, G_DV), l, lambda i, c: (blk(i, c), 0, 0, 0))
    cv_shape = (CONV_W - 1, CONV_ROWS, nb, LANES)
    return pl.pallas_call(
        functools.partial(_gdn_kernel, every=every),
        grid=(b, t // Lb),
        in_specs=[
            pl.BlockSpec((None, Lb, G_CONV_CH), lambda i, c: (i, c, 1)),
            pl.BlockSpec((None, Lb, dv), lambda i, c: (i, c, 6)),
            pl.BlockSpec((None, Lb, LANES), lambda i, c: (i, c, GATE_ROW)),
            _layer_spec((1, LANES), l, lambda i, c: (0, 0)),
            _layer_spec((1, LANES), l, lambda i, c: (0, 0)),
            _layer_spec((CONV_W, G_CONV_CH), l, lambda i, c: (0, 0)),
            _layer_spec((1, dv), l, lambda i, c: (0, 0)),
            pl.BlockSpec((nb * ROWS_PER_TOKEN, LANES), lambda i, c: (blk(i, c), 0)),
            _layer_spec((1, LANES), l, lambda i, c: (0, 0)),
            _layer_spec((1, LANES), l, lambda i, c: (0, 0)),
            _layer_spec((CONV_W, CONV_ROWS, LANES), l, lambda i, c: (0, 0, 0)),
            _layer_spec((1, dv), l, lambda i, c: (0, 0)),
            s_spec,
            _layer_spec(cv_shape, l, lambda i, c: (0, 0, blk(i, c), 0)),
            pl.BlockSpec(memory_space=pl.ANY),
        ],
        out_specs=[
            pl.BlockSpec((None, Lb, dv), lambda i, c: (i, c, 0)),
            pl.BlockSpec((None, G_HEADS, G_DK, G_DV), lambda i, c: (i, 0, 0, 0)),
            pl.BlockSpec((None, CONV_W - 1, G_CONV_CH), lambda i, c: (i, 0, 0)),
            pl.BlockSpec((nb, dv), lambda i, c: (blk(i, c), 0)),
            s_spec,
            pl.BlockSpec(cv_shape, lambda i, c: (0, 0, blk(i, c), 0)),
        ],
        out_shape=[
            jax.ShapeDtypeStruct((b, t, dv), BF16),
            jax.ShapeDtypeStruct((b, G_HEADS, G_DK, G_DV), F32),
            jax.ShapeDtypeStruct((b, CONV_W - 1, G_CONV_CH), F32),
            jax.ShapeDtypeStruct((bs, dv), BF16),
            jax.ShapeDtypeStruct(s_all.shape, F32),
            jax.ShapeDtypeStruct((CONV_W - 1, CONV_ROWS, bs, LANES), F32),
        ],
        scratch_shapes=[
            pltpu.VMEM((G_HEADS, G_DK, G_DV), F32),
            pltpu.VMEM((SUBLANES, G_CONV_CH), F32),
        ],
        input_output_aliases={14: 4},
        compiler_params=_cparams("parallel", "arbitrary"),
        name="gdn",
    )(proj3, proj3, proj3, gate_bias, alog, conv_w, g_norm,
      proj_rows, gate_bias, alog, conv_w_rows, g_norm, s_all, conv_t, s_acc)


def _arrange_w_in_kernel(wt_ref, o_ref):
    n_m = 2 * M_HEADS * M_DK + 2 * M_HEADS * M_DV
    n_g = 4 * G_HEADS * G_DK
    g0 = n_m + 2 * M_HEADS
    tk = wt_ref.shape[1]
    step = 2 * LANES
    for dst, src, n in ((0, 0, n_m), (n_m, g0, n_g)):
        for c in range(0, n, step):
            o_ref[:, dst + c:dst + c + step] = wt_ref[src + c:src + c + step, :].T.astype(BF16)
    n_gate = wt_ref.shape[0] - N_WIDE
    gate = jnp.concatenate([wt_ref[n_m:g0, :], wt_ref[g0 + n_g:, :], jnp.zeros((LANES - n_gate, tk), F32)], axis=0)
    o_ref[:, N_WIDE:N_WIDE + LANES] = gate.T.astype(BF16)
    o_ref[:, N_WIDE + LANES:] = jnp.zeros((tk, N_PROJ - N_WIDE - LANES), BF16)


def _arrange_w_in(w_in, tk=256):
    depth, rows, n_in = w_in.shape
    return pl.pallas_call(
        _arrange_w_in_kernel,
        grid=(depth, rows // tk),
        in_specs=[pl.BlockSpec((None, n_in, tk), lambda l, i: (l, 0, i))],
        out_specs=pl.BlockSpec((None, tk, N_PROJ), lambda l, i: (l, i, 0)),
        out_shape=jax.ShapeDtypeStruct((depth, rows, N_PROJ), BF16),
        compiler_params=_cparams("parallel", "parallel"),
        name="arrange_w_in",
    )(jnp.swapaxes(w_in, 1, 2))


def _lane_row(parts, depth):
    row = jnp.zeros((depth, 1, LANES), F32)
    for off, arr in parts:
        row = row.at[:, 0, off:off + arr.shape[-1]].set(arr.astype(F32))
    return row


def kernel(x_prompt, x_sample, mem_prompt, state_mlstm_C, state_mlstm_n, state_mlstm_m, state_gdn_S,
           state_gdn_conv, cache_mem_k, cache_mem_v, norm_mix, w_in, mlstm_b_i, mlstm_b_f, mlstm_norm,
           gdn_conv_w, gdn_A_log, gdn_dt_bias, gdn_norm, w_out, norm_xattn, norm_mem, xattn_wq, xattn_wk,
           xattn_wv, xattn_wo, norm_ffn, ffn_w_gate, ffn_w_up, ffn_w_down, norm_final):
    bp, seq, _ = x_prompt.shape
    bs = x_sample.shape[0]
    depth = w_in.shape[0]
    dq = X_HEADS * X_DH
    tm_p, tm_wide = 512, 1024

    w_in_b = _arrange_w_in(w_in)
    gate_bias = _lane_row([(GATE_I, mlstm_b_i), (GATE_F, mlstm_b_f), (GATE_A, gdn_dt_bias)], depth)
    alog = _lane_row([(GATE_A, gdn_A_log)], depth)
    g_mix, g_x, g_mem, g_ffn = (g.reshape(depth, 1, D_MODEL) for g in (norm_mix, norm_xattn, norm_mem, norm_ffn))
    g_final = norm_final.reshape(1, D_MODEL)
    m_norm = mlstm_norm.reshape(depth, 1, M_HEADS * M_DV)
    g_norm = gdn_norm.reshape(depth, 1, G_HEADS * G_DV)
    conv_w_rows = gdn_conv_w.reshape(depth, CONV_W, CONV_ROWS, LANES)
    conv_t = jnp.transpose(state_gdn_conv.reshape(depth, bs, CONV_W - 1, CONV_ROWS, LANES), (0, 2, 3, 1, 4))
    n_all = state_mlstm_n.reshape(depth, bs, M_HEADS * M_DK)
    mem_k = cache_mem_k.reshape(depth, bs, N_MEM * X_HEADS, X_DH)
    mem_v = cache_mem_v.reshape(depth, bs, N_MEM * X_HEADS, X_DH)

    mk_all, mv_all = _mem_kv(mem_prompt.reshape(bp * N_MEM, D_MODEL), g_mem, xattn_wk, xattn_wv)

    xp = x_prompt.reshape(bp * seq, D_MODEL)
    xs = x_sample.reshape(bs, D_MODEL)
    c_acc = jnp.zeros(state_mlstm_C.shape, F32)
    s_acc = jnp.zeros(state_gdn_S.shape, F32)
    outs = [[] for _ in range(8)]
    for l in range(depth):
        proj_p, proj_s = _proj(xp, xs, g_mix, w_in_b, l, tm_wide)
        proj3 = proj_p.reshape(bp, seq, N_PROJ)
        proj_rows = proj_s.reshape(bs * ROWS_PER_TOKEN, LANES)
        hm_p, c_p, n_p, m_p, hm_s, c_acc, n_s, m_s = _mlstm(
            proj3, proj_rows, gate_bias, m_norm, state_mlstm_C, n_all, state_mlstm_m, c_acc, l)
        og_p, s_p, cv_p, og_s, s_acc, cv_s = _gdn(
            proj3, proj_rows, gate_bias, alog, gdn_conv_w, conv_w_rows, g_norm, state_gdn_S, conv_t, s_acc, l)
        for lst, val in zip(outs[:5], (c_p, n_p.reshape(bp, M_HEADS, M_DK), m_p[:, 0, :M_HEADS], s_p, cv_p)):
            lst.append(val)
        xp, xs = _mix_out(xp, xs, hm_p.reshape(bp * seq, -1), og_p.reshape(bp * seq, -1), hm_s, og_s, w_out, l, tm_p)
        xp, xs = _xattn(xp, xs, g_x, xattn_wq, mk_all, mv_all, mem_k, mem_v, xattn_wo, l, tm_p, seq)
        xp, xs = _ffn(xp, xs, g_ffn, ffn_w_gate, ffn_w_up, ffn_w_down, g_final, l, tm_wide, l == depth - 1)
        cv_s = jnp.transpose(cv_s, (2, 0, 1, 3)).reshape(bs, CONV_W - 1, G_CONV_CH)
        for lst, val in zip(outs[5:], (n_s.reshape(bs, M_HEADS, M_DK), m_s, cv_s)):
            lst.append(val)

    y_prompt = xp.reshape(bp, seq, D_MODEL)
    y_sample = xs.reshape(bs, 1, D_MODEL)
    pc, pn, pm, ps, pconv, sn, sm, sconv = (jnp.stack(o) for o in outs)
    mem_shape = (depth, bp, N_MEM, X_HEADS, X_DH)
    return (y_prompt, y_sample, pc, pn, pm, ps, pconv, mk_all.reshape(mem_shape), mv_all.reshape(mem_shape),
            c_acc, sn, sm, s_acc, sconv)
```

```python
import functools
import math

import jax
import jax.numpy as jnp
from jax import lax
from jax.experimental import pallas as pl
from jax.experimental.pallas import tpu as pltpu

F32 = jnp.float32
BF16 = jnp.bfloat16

D_MODEL = 2048
M_HEADS = 4
M_DK = 128
M_DV = 256
G_HEADS = 8
G_DK = 128
G_DV = 128
CONV_W = 4
G_CONV_CH = G_HEADS * (2 * G_DK + G_DV)
X_HEADS = 4
X_DH = 128
N_MEM = 256
D_FF = 5632
GATE_CAP = 15.0
EPS = 1e-6

LANES = 128
SUBLANES = 8
N_WIDE = 2 * M_HEADS * M_DK + 2 * M_HEADS * M_DV + 4 * G_HEADS * G_DK
PROJ_TN = 1536
N_PROJ = -(-(N_WIDE + LANES) // PROJ_TN) * PROJ_TN
ROWS_PER_TOKEN = N_PROJ // LANES
GATE_ROW = N_WIDE // LANES
GATE_I, GATE_F, GATE_A, GATE_B = 0, 4, 8, 16
ROW_MQ, ROW_MK, ROW_MV, ROW_MO = 0, 4, 8, 16
ROW_GX, ROW_GZ = 24, 48
CONV_ROWS = G_CONV_CH // LANES

VMEM_LIMIT = 56 * 1024 * 1024

M_CHUNK = 256
G_BLOCK = 256
G_CHUNK = 64
G_PAIR = 2 * G_CHUNK
SAMPLE_BLOCK = 8

NT_DIMS = (((1,), (1,)), ((), ()))
TN_DIMS = (((0,), (0,)), ((), ()))


def _cparams(*sem):
    return pltpu.CompilerParams(dimension_semantics=sem, vmem_limit_bytes=VMEM_LIMIT)


def _dot(a, b):
    return jnp.dot(a, b, preferred_element_type=F32)


def _dot_nt(a, b):
    return lax.dot_general(a, b, NT_DIMS, preferred_element_type=F32)


def _dot_tn(a, b):
    return lax.dot_general(a, b, TN_DIMS, preferred_element_type=F32)


def _dot_f32(a, b):
    return jnp.dot(a, b, preferred_element_type=F32, precision=lax.Precision.HIGHEST)


def _rms(x, g):
    ms = jnp.mean(x * x, axis=-1, keepdims=True)
    return x * lax.rsqrt(ms + EPS) * g


def _sigmoid(x):
    return 1.0 / (1.0 + jnp.exp(-x))


def _softplus(x):
    return jnp.maximum(x, 0.0) + jnp.log1p(jnp.exp(-jnp.abs(x)))


def _softcap(x):
    return GATE_CAP * jnp.tanh(x / GATE_CAP)


def _layer_spec(shape, l, tail):
    return pl.BlockSpec((None,) + shape, lambda *g: (l,) + tail(*g))


def _sample_share(m_prompt, m_sample, tm):
    tiles = m_prompt // tm
    rs = m_sample // tiles
    assert tiles * tm == m_prompt and tiles * rs == m_sample and rs % SUBLANES == 0, (m_prompt, m_sample, tm)
    return tiles, rs


def _proj_kernel(xp_ref, xs_ref, g_ref, w_ref, op_ref, os_ref, h_scr):
    tm = xp_ref.shape[0]

    @pl.when(pl.program_id(1) == 0)
    def _():
        h_scr[:tm] = _rms(xp_ref[...], g_ref[...]).astype(BF16)
        h_scr[tm:] = _rms(xs_ref[...], g_ref[...]).astype(BF16)

    r = _dot(h_scr[...], w_ref[...])
    op_ref[...] = r[:tm]
    os_ref[...] = r[tm:]


def _proj(xp, xs, g, w, l, tm):
    mp, ms = xp.shape[0], xs.shape[0]
    tiles, rs = _sample_share(mp, ms, tm)
    return pl.pallas_call(
        _proj_kernel,
        grid=(tiles, N_PROJ // PROJ_TN),
        in_specs=[
            pl.BlockSpec((tm, D_MODEL), lambda i, j: (i, 0)),
            pl.BlockSpec((rs, D_MODEL), lambda i, j: (i, 0)),
            _layer_spec((1, D_MODEL), l, lambda i, j: (0, 0)),
            _layer_spec((D_MODEL, PROJ_TN), l, lambda i, j: (0, j)),
        ],
        out_specs=[pl.BlockSpec((tm, PROJ_TN), lambda i, j: (i, j)), pl.BlockSpec((rs, PROJ_TN), lambda i, j: (i, j))],
        out_shape=[jax.ShapeDtypeStruct((mp, N_PROJ), F32), jax.ShapeDtypeStruct((ms, N_PROJ), F32)],
        scratch_shapes=[pltpu.VMEM((tm + rs, D_MODEL), BF16)],
        compiler_params=_cparams("parallel", "arbitrary"),
        name="proj",
    )(xp, xs, g, w)


def _mem_kv_kernel(x_ref, g_ref, wk_ref, wv_ref, k_ref, v_ref):
    h = _rms(x_ref[...], g_ref[...]).astype(BF16)
    k_ref[...] = _dot(h, wk_ref[...].astype(BF16))
    v_ref[...] = _dot(h, wv_ref[...].astype(BF16))


def _mem_kv(mem, g, wk, wv):
    depth = wk.shape[0]
    m = mem.shape[0]
    dq = X_HEADS * X_DH
    tm = N_MEM
    w_spec = pl.BlockSpec((None, D_MODEL, dq), lambda l, i: (l, 0, 0))
    o_spec = pl.BlockSpec((None, tm, dq), lambda l, i: (l, i, 0))
    return pl.pallas_call(
        _mem_kv_kernel,
        grid=(depth, m // tm),
        in_specs=[
            pl.BlockSpec((tm, D_MODEL), lambda l, i: (i, 0)),
            pl.BlockSpec((None, 1, D_MODEL), lambda l, i: (l, 0, 0)),
            w_spec, w_spec,
        ],
        out_specs=[o_spec, o_spec],
        out_shape=[jax.ShapeDtypeStruct((depth, m, dq), F32)] * 2,
        compiler_params=_cparams("parallel", "parallel"),
        name="mem_kv",
    )(mem, g, wk, wv)


def _mix_out_kernel(xp_ref, xs_ref, hmp_ref, ogp_ref, hms_ref, ogs_ref, w1_ref, w2_ref, op_ref, os_ref):
    tm = xp_ref.shape[0]
    hm = jnp.concatenate([hmp_ref[...], hms_ref[...]], axis=0)
    og = jnp.concatenate([ogp_ref[...], ogs_ref[...]], axis=0)
    acc = _dot(hm, w1_ref[...].astype(BF16))
    acc = acc + _dot(og, w2_ref[...].astype(BF16))
    op_ref[...] = xp_ref[...] + acc[:tm]
    os_ref[...] = xs_ref[...] + acc[tm:]


def _mix_out(xp, xs, hmp, ogp, hms, ogs, w_out, l, tm):
    mp, ms = xp.shape[0], xs.shape[0]
    tiles, rs = _sample_share(mp, ms, tm)
    half = D_MODEL // 2
    return pl.pallas_call(
        _mix_out_kernel,
        grid=(tiles,),
        in_specs=[
            pl.BlockSpec((tm, D_MODEL), lambda i: (i, 0)),
            pl.BlockSpec((rs, D_MODEL), lambda i: (i, 0)),
            pl.BlockSpec((tm, half), lambda i: (i, 0)),
            pl.BlockSpec((tm, half), lambda i: (i, 0)),
            pl.BlockSpec((rs, half), lambda i: (i, 0)),
            pl.BlockSpec((rs, half), lambda i: (i, 0)),
            _layer_spec((half, D_MODEL), l, lambda i: (0, 0)),
            _layer_spec((half, D_MODEL), l, lambda i: (1, 0)),
        ],
        out_specs=[pl.BlockSpec((tm, D_MODEL), lambda i: (i, 0)), pl.BlockSpec((rs, D_MODEL), lambda i: (i, 0))],
        out_shape=[jax.ShapeDtypeStruct((mp, D_MODEL), F32), jax.ShapeDtypeStruct((ms, D_MODEL), F32)],
        compiler_params=_cparams("parallel"),
        name="mix_out",
    )(xp, xs, hmp, ogp, hms, ogs, w_out, w_out)


def _ffn_kernel(xp_ref, xs_ref, g_ref, wg_ref, wu_ref, wd_ref, gf_ref, op_ref, os_ref, h_scr, *, final):
    j = pl.program_id(1)
    tm = xp_ref.shape[0]

    @pl.when(j == 0)
    def _():
        for x_ref, o_ref, rows in ((xp_ref, op_ref, slice(0, tm)), (xs_ref, os_ref, slice(tm, None))):
            x = x_ref[...]
            h_scr[rows] = _rms(x, g_ref[...]).astype(BF16)
            o_ref[...] = x

    h = h_scr[...]
    a = _dot(h, wg_ref[...].astype(BF16))
    b = _dot(h, wu_ref[...].astype(BF16))
    t = (a * _sigmoid(a)) * b
    r = _dot(t.astype(BF16), wd_ref[...].astype(BF16))
    op_ref[...] += r[:tm]
    os_ref[...] += r[tm:]

    if final:
        @pl.when(j == pl.num_programs(1) - 1)
        def _():
            op_ref[...] = _rms(op_ref[...], gf_ref[...])
            os_ref[...] = _rms(os_ref[...], gf_ref[...])


def _ffn(xp, xs, g, w_gate, w_up, w_down, g_final, l, tm, final, tf=256):
    mp, ms = xp.shape[0], xs.shape[0]
    tiles, rs = _sample_share(mp, ms, tm)
    return pl.pallas_call(
        functools.partial(_ffn_kernel, final=final),
        grid=(tiles, D_FF // tf),
        in_specs=[
            pl.BlockSpec((tm, D_MODEL), lambda i, j: (i, 0)),
            pl.BlockSpec((rs, D_MODEL), lambda i, j: (i, 0)),
            _layer_spec((1, D_MODEL), l, lambda i, j: (0, 0)),
            _layer_spec((D_MODEL, tf), l, lambda i, j: (0, j)),
            _layer_spec((D_MODEL, tf), l, lambda i, j: (0, j)),
            _layer_spec((tf, D_MODEL), l, lambda i, j: (j, 0)),
            pl.BlockSpec((1, D_MODEL), lambda i, j: (0, 0)),
        ],
        out_specs=[pl.BlockSpec((tm, D_MODEL), lambda i, j: (i, 0)), pl.BlockSpec((rs, D_MODEL), lambda i, j: (i, 0))],
        out_shape=[jax.ShapeDtypeStruct((mp, D_MODEL), F32), jax.ShapeDtypeStruct((ms, D_MODEL), F32)],
        scratch_shapes=[pltpu.VMEM((tm + rs, D_MODEL), BF16)],
        compiler_params=_cparams("parallel", "arbitrary"),
        name="ffn",
    )(xp, xs, g, w_gate, w_up, w_down, g_final)


def _softmax_rows(s):
    s = s - jnp.max(s, axis=-1, keepdims=True)
    p = jnp.exp(s)
    return p / jnp.sum(p, axis=-1, keepdims=True)


def _prompt_attention(q, k_ref, v_ref):
    heads = []
    for h in range(X_HEADS):
        sl = slice(h * X_DH, (h + 1) * X_DH)
        s = _dot_nt(q[:, sl].astype(BF16), k_ref[:, sl].astype(BF16)) * (X_DH ** -0.5)
        p = _softmax_rows(s)
        heads.append(_dot(p.astype(BF16), v_ref[:, sl].astype(BF16)))
    return jnp.concatenate(heads, axis=1)


def _sample_attention(q, k_ref, v_ref):
    nb = q.shape[0]
    nk = N_MEM * X_HEADS
    qh = [q[:, h * X_DH:(h + 1) * X_DH] for h in range(X_HEADS)]
    row = lax.broadcasted_iota(jnp.int32, (SUBLANES, nk), 0)
    col = lax.broadcasted_iota(jnp.int32, (SUBLANES, nk), 1)
    own = (col & (X_HEADS - 1)) == (row & (X_HEADS - 1))
    row8 = lax.broadcasted_iota(jnp.int32, (SUBLANES, X_DH), 0)
    outs = [jnp.zeros((nb, X_DH), F32) for _ in range(X_HEADS)]
    scores = []
    for s in range(nb):
        q8 = jnp.zeros((SUBLANES, X_DH), F32)
        for h in range(X_HEADS):
            q8 = jnp.where(row8 == h, qh[h][s:s + 1, :], q8)
        scores.append(_dot_nt(q8.astype(BF16), k_ref[s].astype(BF16)) * (X_DH ** -0.5))
    probs = [_softmax_rows(jnp.where(own, sc, -jnp.inf)).astype(BF16) for sc in scores]
    o8s = [_dot(probs[s], v_ref[s].astype(BF16)) for s in range(nb)]
    for s in range(nb):
        for h in range(X_HEADS):
            outs[h] = jnp.where(row8[:nb] == s, o8s[s][h:h + 1, :], outs[h])
    return jnp.concatenate(outs, axis=1)


def _xattn_kernel(xp_ref, xs_ref, g_ref, wq_ref, kp_ref, vp_ref, ks_ref, vs_ref, wo_ref, op_ref, os_ref):
    tm = xp_ref.shape[0]
    h = jnp.concatenate([_rms(xp_ref[...], g_ref[...]), _rms(xs_ref[...], g_ref[...])], axis=0)
    q = _dot(h.astype(BF16), wq_ref[...].astype(BF16))
    o = jnp.concatenate([_prompt_attention(q[:tm], kp_ref, vp_ref), _sample_attention(q[tm:], ks_ref, vs_ref)],
                        axis=0)
    r = _dot(o.astype(BF16), wo_ref[...].astype(BF16))
    op_ref[...] = xp_ref[...] + r[:tm]
    os_ref[...] = xs_ref[...] + r[tm:]


def _xattn(xp, xs, g, wq, mk_p, mv_p, mk_s, mv_s, wo, l, tm, seq):
    mp, ms = xp.shape[0], xs.shape[0]
    tiles, rs = _sample_share(mp, ms, tm)
    dq = X_HEADS * X_DH
    per_seq = seq // tm
    kvp_spec = pl.BlockSpec((None, N_MEM, dq), lambda i: (l, i // per_seq, 0))
    kvs_spec = _layer_spec((rs, N_MEM * X_HEADS, X_DH), l, lambda i: (i, 0, 0))
    return pl.pallas_call(
        _xattn_kernel,
        grid=(tiles,),
        in_specs=[
            pl.BlockSpec((tm, D_MODEL), lambda i: (i, 0)),
            pl.BlockSpec((rs, D_MODEL), lambda i: (i, 0)),
            _layer_spec((1, D_MODEL), l, lambda i: (0, 0)),
            _layer_spec((D_MODEL, dq), l, lambda i: (0, 0)),
            kvp_spec, kvp_spec, kvs_spec, kvs_spec,
            _layer_spec((dq, D_MODEL), l, lambda i: (0, 0)),
        ],
        out_specs=[pl.BlockSpec((tm, D_MODEL), lambda i: (i, 0)), pl.BlockSpec((rs, D_MODEL), lambda i: (i, 0))],
        out_shape=[jax.ShapeDtypeStruct((mp, D_MODEL), F32), jax.ShapeDtypeStruct((ms, D_MODEL), F32)],
        compiler_params=_cparams("parallel"),
        name="xattn",
    )(xp, xs, g, wq, mk_p, mv_p, mk_s, mv_s, wo)


def _mlstm_prompt_kernel(q_ref, k_ref, v_ref, mo_ref, gt_ref, gb_ref, mn_ref,
                         hm_ref, c_ref, n_ref, m_ref, c_scr, n_scr, m_scr):
    L = q_ref.shape[0]
    c = pl.program_id(1)

    @pl.when(c == 0)
    def _():
        c_scr[...] = jnp.zeros_like(c_scr)
        n_scr[...] = jnp.zeros_like(n_scr)
        m_scr[...] = jnp.zeros_like(m_scr)

    gates = gt_ref[...] + gb_ref[...]
    capped = _softcap(gates)
    logf = -_softplus(-capped)
    row = lax.broadcasted_iota(jnp.int32, (L, L), 0)
    col = lax.broadcasted_iota(jnp.int32, (L, L), 1)
    causal = row >= col
    bcum = _dot_f32(causal.astype(F32), logf)
    lane = lax.broadcasted_iota(jnp.int32, (L, LANES), 1)
    z = jnp.where(lane < GATE_F, capped, bcum)
    zt = z.T

    hs = range(M_HEADS)
    i_col = [z[:, GATE_I + h:GATE_I + h + 1] for h in hs]
    b_col = [z[:, GATE_F + h:GATE_F + h + 1] for h in hs]
    m_prev = [m_scr[0:1, h:h + 1] for h in hs]
    d = [jnp.where(causal, b_col[h] + (zt[GATE_I + h:GATE_I + h + 1, :] - zt[GATE_F + h:GATE_F + h + 1, :]),
                   -jnp.inf) for h in hs]
    inter = [b_col[h] + m_prev[h] for h in hs]
    m_t = [jnp.maximum(inter[h], jnp.max(d[h], axis=1, keepdims=True)) for h in hs]
    q = [q_ref[:, h * M_DK:(h + 1) * M_DK] for h in hs]
    k = [k_ref[:, h * M_DK:(h + 1) * M_DK] * (M_DK ** -0.5) for h in hs]
    qb = [q[h].astype(BF16) for h in hs]
    vb = [v_ref[:, h * M_DV:(h + 1) * M_DV].astype(BF16) for h in hs]
    qk = [_dot_nt(qb[h], k[h].astype(BF16)) for h in hs]
    qc = [_dot(qb[h], c_scr[h].astype(BF16)) for h in hs]
    w = [jnp.exp(d[h] - m_t[h]) * qk[h] for h in hs]
    a = [jnp.exp(inter[h] - m_t[h]) for h in hs]
    wv = [_dot(w[h].astype(BF16), vb[h]) for h in hs]
    nq = [a[h] * jnp.sum(q[h] * n_scr[h], axis=1, keepdims=True) + jnp.sum(w[h], axis=1, keepdims=True) for h in hs]
    hh = [(a[h] * qc[h] + wv[h]) / jnp.maximum(jnp.abs(nq[h]), jnp.exp(-m_t[h])) for h in hs]
    for h in hs:
        y = _rms(hh[h], mn_ref[:, h * M_DV:(h + 1) * M_DV])
        y = y * _sigmoid(mo_ref[:, h * M_DV:(h + 1) * M_DV])
        hm_ref[:, h * M_DV:(h + 1) * M_DV] = y.astype(BF16)

    m_new = [m_t[h][L - 1:L, :] for h in hs]
    b_last = [b_col[h][L - 1:L, :] for h in hs]
    kw = [k[h] * jnp.exp(b_last[h] - b_col[h] + i_col[h] - m_new[h]) for h in hs]
    upd = [_dot_tn(kw[h].astype(BF16), vb[h]) for h in hs]
    for h in hs:
        decay = jnp.exp(b_last[h] + m_prev[h] - m_new[h])
        c_scr[h] = decay * c_scr[h] + upd[h]
        n_scr[h] = decay * n_scr[h] + jnp.sum(kw[h], axis=0, keepdims=True)
        m_scr[0:1, h:h + 1] = m_new[h]

    @pl.when(c == pl.num_programs(1) - 1)
    def _():
        c_ref[...] = c_scr[...]
        n_ref[...] = n_scr[...]
        m_ref[...] = m_scr[...]


def _sample_block_index(b, nc, n_blocks):
    every = b * nc // n_blocks
    assert every * n_blocks == b * nc and nc % every == 0, (b, nc, n_blocks)
    per = nc // every
    return (lambda i, c: i * per + c // every), every


def _token_rows(pr_ref, nb, off):
    return pr_ref[pl.ds(off, nb, stride=ROWS_PER_TOKEN), :]


def _token_rows2(pr_ref, nb, off):
    return jnp.concatenate([_token_rows(pr_ref, nb, off), _token_rows(pr_ref, nb, off + 1)], axis=1)


def _pick_row(tiles):
    rid = lax.broadcasted_iota(jnp.int32, tiles[0].shape, 0)
    out = tiles[0]
    for s in range(1, len(tiles)):
        out = jnp.where(rid == s, tiles[s], out)
    return out


def _columns(tiles):
    pad = LANES - SUBLANES * len(tiles)
    stack = jnp.concatenate(tiles + ([jnp.zeros((pad, LANES), F32)] if pad else []), axis=0)
    return stack.T


def _mlstm_sample_kernel(pr_ref, gb_ref, mn_ref, c_ref, n_ref, m_ref, acc_ref,
                         hm_ref, co_ref, no_ref, mo_ref):
    del acc_ref
    nb = m_ref.shape[0]
    gates = _token_rows(pr_ref, nb, GATE_ROW) + gb_ref[...]
    capped = _softcap(gates)
    logf = -_softplus(-capped)
    i4 = capped[:, GATE_I:GATE_I + M_HEADS]
    f4 = logf[:, GATE_F:GATE_F + M_HEADS]
    inter = f4 + m_ref[...]
    m_new = jnp.maximum(inter, i4)
    a4 = jnp.exp(inter - m_new)
    w4 = jnp.exp(i4 - m_new)
    e4 = jnp.exp(-m_new)
    mo_ref[...] = m_new

    scale = M_DK ** -0.5
    ys, n_rows = [], []
    for h in range(M_HEADS):
        a, wi, e = a4[:, h:h + 1], w4[:, h:h + 1], e4[:, h:h + 1]
        q = _token_rows(pr_ref, nb, ROW_MQ + h)
        k = _token_rows(pr_ref, nb, ROW_MK + h) * scale
        v = _token_rows2(pr_ref, nb, ROW_MV + 2 * h)
        og = _token_rows2(pr_ref, nb, ROW_MO + 2 * h)
        n_old = n_ref[:, h * M_DK:(h + 1) * M_DK]
        wk = wi * k
        n_rows.append(a * n_old + wk)
        wts = wi * jnp.sum(q * k, axis=1, keepdims=True)
        nq = a * jnp.sum(q * n_old, axis=1, keepdims=True) + wts
        wk_cols = _columns([wk])
        a_rows = jnp.broadcast_to(a, (nb, M_DV))
        qb = q.astype(BF16)
        qc = []
        for s in range(nb):
            c_old = c_ref[s, h]
            qc.append(_dot(qb, c_old.astype(BF16)))
            co_ref[s, h] = a_rows[s:s + 1, :] * c_old + wk_cols[:, s:s + 1] * v[s:s + 1, :]
        num = a * _pick_row(qc) + wts * v
        hh = num / jnp.maximum(jnp.abs(nq), e)
        ys.append(_rms(hh, mn_ref[:, h * M_DV:(h + 1) * M_DV]) * _sigmoid(og))
    hm_ref[...] = jnp.concatenate(ys, axis=1).astype(BF16)
    no_ref[...] = jnp.concatenate(n_rows, axis=1)


def _mlstm_kernel(*refs, every):
    p_in, s_in, p_out, s_out, scr = refs[:7], refs[7:14], refs[14:18], refs[18:22], refs[22:]
    _mlstm_prompt_kernel(*p_in, *p_out, *scr)

    @pl.when(pl.program_id(1) % every == 0)
    def _():
        _mlstm_sample_kernel(*s_in, *s_out)


def _mlstm(proj3, proj_rows, gate_bias, m_norm, c_all, n_all, m_all, c_acc, l):
    b, t, _ = proj3.shape
    bs = m_all.shape[1]
    L = M_CHUNK
    nb = SAMPLE_BLOCK
    dqk = M_HEADS * M_DK
    dv = M_HEADS * M_DV
    blk, every = _sample_block_index(b, t // L, bs // nb)
    c_spec = _layer_spec((nb, M_HEADS, M_DK, M_DV), l, lambda i, c: (blk(i, c), 0, 0, 0))
    return pl.pallas_call(
        functools.partial(_mlstm_kernel, every=every),
        grid=(b, t // L),
        in_specs=[
            pl.BlockSpec((None, L, dqk), lambda i, c: (i, c, 0)),
            pl.BlockSpec((None, L, dqk), lambda i, c: (i, c, 1)),
            pl.BlockSpec((None, L, dv), lambda i, c: (i, c, 1)),
            pl.BlockSpec((None, L, dv), lambda i, c: (i, c, 2)),
            pl.BlockSpec((None, L, LANES), lambda i, c: (i, c, GATE_ROW)),
            _layer_spec((1, LANES), l, lambda i, c: (0, 0)),
            _layer_spec((1, dv), l, lambda i, c: (0, 0)),
            pl.BlockSpec((nb * ROWS_PER_TOKEN, LANES), lambda i, c: (blk(i, c), 0)),
            _layer_spec((1, LANES), l, lambda i, c: (0, 0)),
            _layer_spec((1, dv), l, lambda i, c: (0, 0)),
            c_spec,
            _layer_spec((nb, dqk), l, lambda i, c: (blk(i, c), 0)),
            _layer_spec((nb, M_HEADS), l, lambda i, c: (blk(i, c), 0)),
            pl.BlockSpec(memory_space=pl.ANY),
        ],
        out_specs=[
            pl.BlockSpec((None, L, dv), lambda i, c: (i, c, 0)),
            pl.BlockSpec((None, M_HEADS, M_DK, M_DV), lambda i, c: (i, 0, 0, 0)),
            pl.BlockSpec((None, M_HEADS, 1, M_DK), lambda i, c: (i, 0, 0, 0)),
            pl.BlockSpec((None, 1, LANES), lambda i, c: (i, 0, 0)),
            pl.BlockSpec((nb, dv), lambda i, c: (blk(i, c), 0)),
            c_spec,
            pl.BlockSpec((nb, dqk), lambda i, c: (blk(i, c), 0)),
            pl.BlockSpec((nb, M_HEADS), lambda i, c: (blk(i, c), 0)),
        ],
        out_shape=[
            jax.ShapeDtypeStruct((b, t, dv), BF16),
            jax.ShapeDtypeStruct((b, M_HEADS, M_DK, M_DV), F32),
            jax.ShapeDtypeStruct((b, M_HEADS, 1, M_DK), F32),
            jax.ShapeDtypeStruct((b, 1, LANES), F32),
            jax.ShapeDtypeStruct((bs, dv), BF16),
            jax.ShapeDtypeStruct(c_all.shape, F32),
            jax.ShapeDtypeStruct((bs, dqk), F32),
            jax.ShapeDtypeStruct((bs, M_HEADS), F32),
        ],
        scratch_shapes=[
            pltpu.VMEM((M_HEADS, M_DK, M_DV), F32),
            pltpu.VMEM((M_HEADS, 1, M_DK), F32),
            pltpu.VMEM((1, LANES), F32),
        ],
        input_output_aliases={13: 5},
        compiler_params=_cparams("parallel", "arbitrary"),
        name="mlstm",
    )(proj3, proj3, proj3, proj3, proj3, gate_bias, m_norm,
      proj_rows, gate_bias, m_norm, c_all, n_all, m_all, c_acc)


def _gdn_gates(gates, alog):
    log_a = -jnp.exp(alog) * _softplus(gates)
    beta = _sigmoid(gates)
    return log_a, beta


def _gdn_prompt_kernel(x_ref, gz_ref, gt_ref, gb_ref, al_ref, cw_ref, gn_ref,
                       og_ref, s_ref, cv_ref, s_scr, tail_scr):
    Lb = x_ref.shape[0]
    L, P = G_CHUNK, G_PAIR
    n_pair = Lb // P
    c = pl.program_id(1)

    @pl.when(c == 0)
    def _():
        s_scr[...] = jnp.zeros_like(s_scr)
        tail_scr[...] = jnp.zeros_like(tail_scr)

    log_a, beta = _gdn_gates(gt_ref[...] + gb_ref[...], al_ref[...])
    row = lax.broadcasted_iota(jnp.int32, (Lb, Lb), 0)
    col = lax.broadcasted_iota(jnp.int32, (Lb, Lb), 1)
    shift = int(math.log2(L))
    same_chunk = jnp.right_shift(row, shift) == jnp.right_shift(col, shift)
    gcum = _dot_f32(jnp.where(same_chunk & (row >= col), 1.0, 0.0), log_a)
    gtot = _dot_f32(jnp.where(same_chunk, 1.0, 0.0), log_a)
    lane = lax.broadcasted_iota(jnp.int32, (Lb, LANES), 1)
    zt = jnp.where(lane < GATE_B, gcum, beta).T

    rp = lax.broadcasted_iota(jnp.int32, (P, P), 0)
    cp = lax.broadcasted_iota(jnp.int32, (P, P), 1)
    same = jnp.right_shift(rp, shift) == jnp.right_shift(cp, shift)
    incl = same & (rp >= cp)
    strict = same & (rp > cp)

    def conv_cols(c0):
        cols = slice(c0, c0 + LANES)
        w = [cw_ref[j:j + 1, cols] for j in range(CONV_W)]
        xp = jnp.concatenate([tail_scr[:, cols], x_ref[:, cols]], axis=0)
        x1 = pltpu.roll(xp, 1, axis=0)
        y = (xp * w[3] + x1 * w[2]) + pltpu.roll(xp * w[1] + x1 * w[0], 2, axis=0)
        y = y[SUBLANES:]
        return y * _sigmoid(y)

    heads = []
    for h in range(G_HEADS):
        qc = conv_cols(h * G_DK)
        kc = conv_cols(G_HEADS * G_DK + h * G_DK)
        vc = conv_cols(2 * G_HEADS * G_DK + h * G_DV)
        q2 = qc * (lax.rsqrt(jnp.sum(qc * qc, axis=1, keepdims=True) + EPS) * (G_DK ** -0.5))
        k2 = kc * lax.rsqrt(jnp.sum(kc * kc, axis=1, keepdims=True) + EPS)
        g_col = gcum[:, GATE_A + h:GATE_A + h + 1]
        t_col = gtot[:, GATE_A + h:GATE_A + h + 1]
        b_col = beta[:, GATE_B + h:GATE_B + h + 1]
        eg = jnp.exp(g_col)
        heads.append(dict(
            qb=q2.astype(BF16), kb=k2.astype(BF16), eg=eg, b_col=b_col,
            vek=jnp.concatenate([vc, eg * k2], axis=1),
            kd=(k2 * jnp.exp(t_col - g_col)).astype(BF16),
            dS=jnp.exp(t_col),
        ))

    blocks = []
    for h in range(G_HEADS):
        hd = heads[h]
        for b in range(n_pair):
            r0 = b * P
            kb = hd["kb"][r0:r0 + P]
            g_col = gcum[r0:r0 + P, GATE_A + h:GATE_A + h + 1]
            g_row = zt[GATE_A + h:GATE_A + h + 1, r0:r0 + P]
            b_row = zt[GATE_B + h:GATE_B + h + 1, r0:r0 + P]
            dec = jnp.exp(jnp.where(incl, g_col - g_row, -jnp.inf))
            a_mat = jnp.where(strict, dec * _dot_nt(kb, kb) * b_row, 0.0)
            qkd = (_dot_nt(hd["qb"][r0:r0 + P], kb) * dec).astype(BF16)
            blocks.append(dict(h=h, r0=r0, mpow=-a_mat, qacc=-a_mat, qkd=qkd))

    for blk in blocks:
        blk["mb"] = blk["mpow"].astype(BF16)
    for _ in range(int(math.log2(L)) - 1):
        for blk in blocks:
            blk["mpow"] = _dot(blk["mb"], blk["mb"])
            blk["mb"] = blk["mpow"].astype(BF16)
        for blk in blocks:
            blk["qacc"] = blk["qacc"] + blk["mpow"] + _dot(blk["qacc"].astype(BF16), blk["mb"])

    for blk in blocks:
        hd = heads[blk["h"]]
        r0 = blk["r0"]
        vek = hd["vek"][r0:r0 + P]
        t = (vek + _dot(blk["qacc"].astype(BF16), vek.astype(BF16))) * hd["b_col"][r0:r0 + P]
        blk["uv"] = t[:, :G_DV]
        blk["kw"] = t[:, G_DV:].astype(BF16)

    states = [s_scr[h] for h in range(G_HEADS)]
    wus = [[] for _ in range(G_HEADS)]
    qss = [[] for _ in range(G_HEADS)]
    for ci in range(Lb // L):
        r0 = ci * L
        for h in range(G_HEADS):
            hd = heads[h]
            blk = blocks[h * n_pair + r0 // P]
            o0 = r0 - blk["r0"]
            lhs = jnp.concatenate([blk["kw"][o0:o0 + L], hd["qb"][r0:r0 + L]], axis=0)
            r = _dot(lhs, states[h].astype(BF16))
            wu = (blk["uv"][o0:o0 + L] - r[:L]).astype(BF16)
            qss[h].append(r[L:])
            wus[h].append(wu)
            states[h] = hd["dS"][r0:r0 + 1] * states[h] + _dot_tn(hd["kd"][r0:r0 + L], wu)
    for h in range(G_HEADS):
        s_scr[h] = states[h]

    for h in range(G_HEADS):
        hd = heads[h]
        for b in range(n_pair):
            r0 = b * P
            blk = blocks[h * n_pair + b]
            i0 = r0 // L
            wu = jnp.concatenate(wus[h][i0:i0 + P // L], axis=0)
            qs = jnp.concatenate(qss[h][i0:i0 + P // L], axis=0)
            o = hd["eg"][r0:r0 + P] * qs + _dot(blk["qkd"], wu)
            y = _rms(o, gn_ref[:, h * G_DV:(h + 1) * G_DV])
            gz = gz_ref[r0:r0 + P, h * G_DV:(h + 1) * G_DV]
            og_ref[r0:r0 + P, h * G_DV:(h + 1) * G_DV] = (y * (gz * _sigmoid(gz))).astype(BF16)

    tail_scr[...] = x_ref[Lb - SUBLANES:Lb, :]

    @pl.when(c == pl.num_programs(1) - 1)
    def _():
        s_ref[...] = s_scr[...]
        cv_ref[...] = x_ref[Lb - (CONV_W - 1):Lb, :]


def _gdn_sample_kernel(pr_ref, gb_ref, al_ref, cw_ref, gn_ref, s_ref, cb_ref, acc_ref,
                       og_ref, so_ref, co_ref):
    del acc_ref
    nb = cb_ref.shape[2]
    log_a, beta = _gdn_gates(_token_rows(pr_ref, nb, GATE_ROW) + gb_ref[...], al_ref[...])
    eg8 = jnp.exp(log_a)

    ys = []
    for r in range(CONV_ROWS):
        xr = _token_rows(pr_ref, nb, ROW_GX + r)
        y = xr * cw_ref[CONV_W - 1, r:r + 1, :]
        for j in range(CONV_W - 1):
            y = y + cb_ref[j, r] * cw_ref[j, r:r + 1, :]
        co_ref[0, r] = cb_ref[1, r]
        co_ref[1, r] = cb_ref[2, r]
        co_ref[2, r] = xr
        ys.append(y * _sigmoid(y))

    outs = []
    for h in range(G_HEADS):
        qc, kc, v = ys[h], ys[G_HEADS + h], ys[2 * G_HEADS + h]
        q = qc * lax.rsqrt(jnp.sum(qc * qc, axis=1, keepdims=True) + EPS) * (G_DK ** -0.5)
        k = kc * lax.rsqrt(jnp.sum(kc * kc, axis=1, keepdims=True) + EPS)
        eg = eg8[:, GATE_A + h:GATE_A + h + 1]
        bt = beta[:, GATE_B + h:GATE_B + h + 1]
        qb, kb = q.astype(BF16), k.astype(BF16)
        ks, qs = [], []
        for s in range(nb):
            sb = s_ref[s, h].astype(BF16)
            ks.append(_dot(kb, sb))
            qs.append(_dot(qb, sb))
        u = bt * (v - eg * _pick_row(ks))
        o = eg * _pick_row(qs) + jnp.sum(q * k, axis=1, keepdims=True) * u
        k_cols = _columns([k])
        eg_rows = jnp.broadcast_to(eg, (nb, G_DV))
        for s in range(nb):
            so_ref[s, h] = eg_rows[s:s + 1, :] * s_ref[s, h] + k_cols[:, s:s + 1] * u[s:s + 1, :]
        gz = _token_rows(pr_ref, nb, ROW_GZ + h)
        outs.append(_rms(o, gn_ref[:, h * G_DV:(h + 1) * G_DV]) * (gz * _sigmoid(gz)))
    og_ref[...] = jnp.concatenate(outs, axis=1).astype(BF16)


def _gdn_kernel(*refs, every):
    p_in, s_in, p_out, s_out, scr = refs[:7], refs[7:15], refs[15:18], refs[18:21], refs[21:]
    _gdn_prompt_kernel(*p_in, *p_out, *scr)

    @pl.when(pl.program_id(1) % every == 0)
    def _():
        _gdn_sample_kernel(*s_in, *s_out)


def _gdn(proj3, proj_rows, gate_bias, alog, conv_w, conv_w_rows, g_norm, s_all, conv_t, s_acc, l):
    b, t, _ = proj3.shape
    bs = s_all.shape[1]
    Lb = G_BLOCK
    nb = SAMPLE_BLOCK
    dv = G_HEADS * G_DV
    blk, every = _sample_block_index(b, t // Lb, bs // nb)
    s_spec = _layer_spec((nb, G_HEADS, G_DK, G_DV), l, lambda i, c: (blk(i, c), 0, 0, 0))
    cv_shape = (CONV_W - 1, CONV_ROWS, nb, LANES)
    return pl.pallas_call(
        functools.partial(_gdn_kernel, every=every),
        grid=(b, t // Lb),
        in_specs=[
            pl.BlockSpec((None, Lb, G_CONV_CH), lambda i, c: (i, c, 1)),
            pl.BlockSpec((None, Lb, dv), lambda i, c: (i, c, 6)),
            pl.BlockSpec((None, Lb, LANES), lambda i, c: (i, c, GATE_ROW)),
            _layer_spec((1, LANES), l, lambda i, c: (0, 0)),
            _layer_spec((1, LANES), l, lambda i, c: (0, 0)),
            _layer_spec((CONV_W, G_CONV_CH), l, lambda i, c: (0, 0)),
            _layer_spec((1, dv), l, lambda i, c: (0, 0)),
            pl.BlockSpec((nb * ROWS_PER_TOKEN, LANES), lambda i, c: (blk(i, c), 0)),
            _layer_spec((1, LANES), l, lambda i, c: (0, 0)),
            _layer_spec((1, LANES), l, lambda i, c: (0, 0)),
            _layer_spec((CONV_W, CONV_ROWS, LANES), l, lambda i, c: (0, 0, 0)),
            _layer_spec((1, dv), l, lambda i, c: (0, 0)),
            s_spec,
            _layer_spec(cv_shape, l, lambda i, c: (0, 0, blk(i, c), 0)),
            pl.BlockSpec(memory_space=pl.ANY),
        ],
        out_specs=[
            pl.BlockSpec((None, Lb, dv), lambda i, c: (i, c, 0)),
            pl.BlockSpec((None, G_HEADS, G_DK, G_DV), lambda i, c: (i, 0, 0, 0)),
            pl.BlockSpec((None, CONV_W - 1, G_CONV_CH), lambda i, c: (i, 0, 0)),
            pl.BlockSpec((nb, dv), lambda i, c: (blk(i, c), 0)),
            s_spec,
            pl.BlockSpec(cv_shape, lambda i, c: (0, 0, blk(i, c), 0)),
        ],
        out_shape=[
            jax.ShapeDtypeStruct((b, t, dv), BF16),
            jax.ShapeDtypeStruct((b, G_HEADS, G_DK, G_DV), F32),
            jax.ShapeDtypeStruct((b, CONV_W - 1, G_CONV_CH), F32),
            jax.ShapeDtypeStruct((bs, dv), BF16),
            jax.ShapeDtypeStruct(s_all.shape, F32),
            jax.ShapeDtypeStruct((CONV_W - 1, CONV_ROWS, bs, LANES), F32),
        ],
        scratch_shapes=[
            pltpu.VMEM((G_HEADS, G_DK, G_DV), F32),
            pltpu.VMEM((SUBLANES, G_CONV_CH), F32),
        ],
        input_output_aliases={14: 4},
        compiler_params=_cparams("parallel", "arbitrary"),
        name="gdn",
    )(proj3, proj3, proj3, gate_bias, alog, conv_w, g_norm,
      proj_rows, gate_bias, alog, conv_w_rows, g_norm, s_all, conv_t, s_acc)


def _arrange_w_in_kernel(wt_ref, o_ref):
    n_m = 2 * M_HEADS * M_DK + 2 * M_HEADS * M_DV
    n_g = 4 * G_HEADS * G_DK
    g0 = n_m + 2 * M_HEADS
    tk = wt_ref.shape[1]
    step = 2 * LANES
    for dst, src, n in ((0, 0, n_m), (n_m, g0, n_g)):
        for c in range(0, n, step):
            o_ref[:, dst + c:dst + c + step] = wt_ref[src + c:src + c + step, :].T.astype(BF16)
    n_gate = wt_ref.shape[0] - N_WIDE
    gate = jnp.concatenate([wt_ref[n_m:g0, :], wt_ref[g0 + n_g:, :], jnp.zeros((LANES - n_gate, tk), F32)], axis=0)
    o_ref[:, N_WIDE:N_WIDE + LANES] = gate.T.astype(BF16)
    o_ref[:, N_WIDE + LANES:] = jnp.zeros((tk, N_PROJ - N_WIDE - LANES), BF16)


def _arrange_w_in(w_in, tk=256):
    depth, rows, n_in = w_in.shape
    return pl.pallas_call(
        _arrange_w_in_kernel,
        grid=(depth, rows // tk),
        in_specs=[pl.BlockSpec((None, n_in, tk), lambda l, i: (l, 0, i))],
        out_specs=pl.BlockSpec((None, tk, N_PROJ), lambda l, i: (l, i, 0)),
        out_shape=jax.ShapeDtypeStruct((depth, rows, N_PROJ), BF16),
        compiler_params=_cparams("parallel", "parallel"),
        name="arrange_w_in",
    )(jnp.swapaxes(w_in, 1, 2))


def _lane_row(parts, depth):
    row = jnp.zeros((depth, 1, LANES), F32)
    for off, arr in parts:
        row = row.at[:, 0, off:off + arr.shape[-1]].set(arr.astype(F32))
    return row


def kernel(x_prompt, x_sample, mem_prompt, state_mlstm_C, state_mlstm_n, state_mlstm_m, state_gdn_S,
           state_gdn_conv, cache_mem_k, cache_mem_v, norm_mix, w_in, mlstm_b_i, mlstm_b_f, mlstm_norm,
           gdn_conv_w, gdn_A_log, gdn_dt_bias, gdn_norm, w_out, norm_xattn, norm_mem, xattn_wq, xattn_wk,
           xattn_wv, xattn_wo, norm_ffn, ffn_w_gate, ffn_w_up, ffn_w_down, norm_final):
    bp, seq, _ = x_prompt.shape
    bs = x_sample.shape[0]
    depth = w_in.shape[0]
    dq = X_HEADS * X_DH
    tm_p, tm_wide = 512, 1024

    w_in_b = _arrange_w_in(w_in)
    gate_bias = _lane_row([(GATE_I, mlstm_b_i), (GATE_F, mlstm_b_f), (GATE_A, gdn_dt_bias)], depth)
    alog = _lane_row([(GATE_A, gdn_A_log)], depth)
    g_mix, g_x, g_mem, g_ffn = (g.reshape(depth, 1, D_MODEL) for g in (norm_mix, norm_xattn, norm_mem, norm_ffn))
    g_final = norm_final.reshape(1, D_MODEL)
    m_norm = mlstm_norm.reshape(depth, 1, M_HEADS * M_DV)
    g_norm = gdn_norm.reshape(depth, 1, G_HEADS * G_DV)
    conv_w_rows = gdn_conv_w.reshape(depth, CONV_W, CONV_ROWS, LANES)
    conv_t = jnp.transpose(state_gdn_conv.reshape(depth, bs, CONV_W - 1, CONV_ROWS, LANES), (0, 2, 3, 1, 4))
    n_all = state_mlstm_n.reshape(depth, bs, M_HEADS * M_DK)
    mem_k = cache_mem_k.reshape(depth, bs, N_MEM * X_HEADS, X_DH)
    mem_v = cache_mem_v.reshape(depth, bs, N_MEM * X_HEADS, X_DH)

    mk_all, mv_all = _mem_kv(mem_prompt.reshape(bp * N_MEM, D_MODEL), g_mem, xattn_wk, xattn_wv)

    xp = x_prompt.reshape(bp * seq, D_MODEL)
    xs = x_sample.reshape(bs, D_MODEL)
    c_acc = jnp.zeros(state_mlstm_C.shape, F32)
    s_acc = jnp.zeros(state_gdn_S.shape, F32)
    outs = [[] for _ in range(8)]
    for l in range(depth):
        proj_p, proj_s = _proj(xp, xs, g_mix, w_in_b, l, tm_wide)
        proj3 = proj_p.reshape(bp, seq, N_PROJ)
        proj_rows = proj_s.reshape(bs * ROWS_PER_TOKEN, LANES)
        hm_p, c_p, n_p, m_p, hm_s, c_acc, n_s, m_s = _mlstm(
            proj3, proj_rows, gate_bias, m_norm, state_mlstm_C, n_all, state_mlstm_m, c_acc, l)
        og_p, s_p, cv_p, og_s, s_acc, cv_s = _gdn(
            proj3, proj_rows, gate_bias, alog, gdn_conv_w, conv_w_rows, g_norm, state_gdn_S, conv_t, s_acc, l)
        for lst, val in zip(outs[:5], (c_p, n_p.reshape(bp, M_HEADS, M_DK), m_p[:, 0, :M_HEADS], s_p, cv_p)):
            lst.append(val)
        xp, xs = _mix_out(xp, xs, hm_p.reshape(bp * seq, -1), og_p.reshape(bp * seq, -1), hm_s, og_s, w_out, l, tm_p)
        xp, xs = _xattn(xp, xs, g_x, xattn_wq, mk_all, mv_all, mem_k, mem_v, xattn_wo, l, tm_p, seq)
        xp, xs = _ffn(xp, xs, g_ffn, ffn_w_gate, ffn_w_up, ffn_w_down, g_final, l, tm_wide, l == depth - 1)
        cv_s = jnp.transpose(cv_s, (2, 0, 1, 3)).reshape(bs, CONV_W - 1, G_CONV_CH)
        for lst, val in zip(outs[5:], (n_s.reshape(bs, M_HEADS, M_DK), m_s, cv_s)):
            lst.append(val)

    y_prompt = xp.reshape(bp, seq, D_MODEL)
    y_sample = xs.reshape(bs, 1, D_MODEL)
    pc, pn, pm, ps, pconv, sn, sm, sconv = (jnp.stack(o) for o in outs)
    mem_shape = (depth, bp, N_MEM, X_HEADS, X_DH)
    return (y_prompt, y_sample, pc, pn, pm, ps, pconv, mk_all.reshape(mem_shape), mv_all.reshape(mem_shape),
            c_acc, sn, sm, s_acc, sconv)
```

```python
import functools
import math

import jax
import jax.numpy as jnp
from jax import lax
from jax.experimental import pallas as pl
from jax.experimental.pallas import tpu as pltpu

F32 = jnp.float32
BF16 = jnp.bfloat16

D_MODEL = 2048
M_HEADS = 4
M_DK = 128
M_DV = 256
G_HEADS = 8
G_DK = 128
G_DV = 128
CONV_W = 4
G_CONV_CH = G_HEADS * (2 * G_DK + G_DV)
X_HEADS = 4
X_DH = 128
N_MEM = 256
D_FF = 5632
GATE_CAP = 15.0
EPS = 1e-6

LANES = 128
SUBLANES = 8
N_WIDE = 2 * M_HEADS * M_DK + 2 * M_HEADS * M_DV + 4 * G_HEADS * G_DK
PROJ_TN = 1536
N_PROJ = -(-(N_WIDE + LANES) // PROJ_TN) * PROJ_TN
ROWS_PER_TOKEN = N_PROJ // LANES
GATE_ROW = N_WIDE // LANES
GATE_I, GATE_F, GATE_A, GATE_B = 0, 4, 8, 16
ROW_MQ, ROW_MK, ROW_MV, ROW_MO = 0, 4, 8, 16
ROW_GX, ROW_GZ = 24, 48
CONV_ROWS = G_CONV_CH // LANES

VMEM_LIMIT = 56 * 1024 * 1024

M_CHUNK = 256
G_BLOCK = 256
G_CHUNK = 64
G_PAIR = 2 * G_CHUNK
SAMPLE_BLOCK = 8

NT_DIMS = (((1,), (1,)), ((), ()))
TN_DIMS = (((0,), (0,)), ((), ()))


def _cparams(*sem):
    return pltpu.CompilerParams(dimension_semantics=sem, vmem_limit_bytes=VMEM_LIMIT)


def _dot(a, b):
    return jnp.dot(a, b, preferred_element_type=F32)


def _dot_nt(a, b):
    return lax.dot_general(a, b, NT_DIMS, preferred_element_type=F32)


def _dot_tn(a, b):
    return lax.dot_general(a, b, TN_DIMS, preferred_element_type=F32)


def _dot_f32(a, b):
    return jnp.dot(a, b, preferred_element_type=F32, precision=lax.Precision.HIGHEST)


def _rms(x, g):
    ms = jnp.mean(x * x, axis=-1, keepdims=True)
    return x * lax.rsqrt(ms + EPS) * g


def _sigmoid(x):
    return 1.0 / (1.0 + jnp.exp(-x))


def _softplus(x):
    return jnp.maximum(x, 0.0) + jnp.log1p(jnp.exp(-jnp.abs(x)))


def _softcap(x):
    return GATE_CAP * jnp.tanh(x / GATE_CAP)


def _layer_spec(shape, l, tail):
    return pl.BlockSpec((None,) + shape, lambda *g: (l,) + tail(*g))


def _sample_share(m_prompt, m_sample, tm):
    tiles = m_prompt // tm
    rs = m_sample // tiles
    assert tiles * tm == m_prompt and tiles * rs == m_sample and rs % SUBLANES == 0, (m_prompt, m_sample, tm)
    return tiles, rs


def _proj_kernel(xp_ref, xs_ref, g_ref, w_ref, op_ref, os_ref, h_scr):
    tm = xp_ref.shape[0]

    @pl.when(pl.program_id(1) == 0)
    def _():
        h_scr[:tm] = _rms(xp_ref[...], g_ref[...]).astype(BF16)
        h_scr[tm:] = _rms(xs_ref[...], g_ref[...]).astype(BF16)

    r = _dot(h_scr[...], w_ref[...])
    op_ref[...] = r[:tm]
    os_ref[...] = r[tm:]


def _proj(xp, xs, g, w, l, tm):
    mp, ms = xp.shape[0], xs.shape[0]
    tiles, rs = _sample_share(mp, ms, tm)
    return pl.pallas_call(
        _proj_kernel,
        grid=(tiles, N_PROJ // PROJ_TN),
        in_specs=[
            pl.BlockSpec((tm, D_MODEL), lambda i, j: (i, 0)),
            pl.BlockSpec((rs, D_MODEL), lambda i, j: (i, 0)),
            _layer_spec((1, D_MODEL), l, lambda i, j: (0, 0)),
            _layer_spec((D_MODEL, PROJ_TN), l, lambda i, j: (0, j)),
        ],
        out_specs=[pl.BlockSpec((tm, PROJ_TN), lambda i, j: (i, j)), pl.BlockSpec((rs, PROJ_TN), lambda i, j: (i, j))],
        out_shape=[jax.ShapeDtypeStruct((mp, N_PROJ), F32), jax.ShapeDtypeStruct((ms, N_PROJ), F32)],
        scratch_shapes=[pltpu.VMEM((tm + rs, D_MODEL), BF16)],
        compiler_params=_cparams("parallel", "arbitrary"),
        name="proj",
    )(xp, xs, g, w)


def _mem_kv_kernel(x_ref, g_ref, wk_ref, wv_ref, k_ref, v_ref):
    h = _rms(x_ref[...], g_ref[...]).astype(BF16)
    k_ref[...] = _dot(h, wk_ref[...].astype(BF16))
    v_ref[...] = _dot(h, wv_ref[...].astype(BF16))


def _mem_kv(mem, g, wk, wv):
    depth = wk.shape[0]
    m = mem.shape[0]
    dq = X_HEADS * X_DH
    tm = N_MEM
    w_spec = pl.BlockSpec((None, D_MODEL, dq), lambda l, i: (l, 0, 0))
    o_spec = pl.BlockSpec((None, tm, dq), lambda l, i: (l, i, 0))
    return pl.pallas_call(
        _mem_kv_kernel,
        grid=(depth, m // tm),
        in_specs=[
            pl.BlockSpec((tm, D_MODEL), lambda l, i: (i, 0)),
            pl.BlockSpec((None, 1, D_MODEL), lambda l, i: (l, 0, 0)),
            w_spec, w_spec,
        ],
        out_specs=[o_spec, o_spec],
        out_shape=[jax.ShapeDtypeStruct((depth, m, dq), F32)] * 2,
        compiler_params=_cparams("parallel", "parallel"),
        name="mem_kv",
    )(mem, g, wk, wv)


def _mix_out_kernel(xp_ref, xs_ref, hmp_ref, ogp_ref, hms_ref, ogs_ref, w1_ref, w2_ref, op_ref, os_ref):
    tm = xp_ref.shape[0]
    hm = jnp.concatenate([hmp_ref[...], hms_ref[...]], axis=0)
    og = jnp.concatenate([ogp_ref[...], ogs_ref[...]], axis=0)
    acc = _dot(hm, w1_ref[...].astype(BF16))
    acc = acc + _dot(og, w2_ref[...].astype(BF16))
    op_ref[...] = xp_ref[...] + acc[:tm]
    os_ref[...] = xs_ref[...] + acc[tm:]


def _mix_out(xp, xs, hmp, ogp, hms, ogs, w_out, l, tm):
    mp, ms = xp.shape[0], xs.shape[0]
    tiles, rs = _sample_share(mp, ms, tm)
    half = D_MODEL // 2
    return pl.pallas_call(
        _mix_out_kernel,
        grid=(tiles,),
        in_specs=[
            pl.BlockSpec((tm, D_MODEL), lambda i: (i, 0)),
            pl.BlockSpec((rs, D_MODEL), lambda i: (i, 0)),
            pl.BlockSpec((tm, half), lambda i: (i, 0)),
            pl.BlockSpec((tm, half), lambda i: (i, 0)),
            pl.BlockSpec((rs, half), lambda i: (i, 0)),
            pl.BlockSpec((rs, half), lambda i: (i, 0)),
            _layer_spec((half, D_MODEL), l, lambda i: (0, 0)),
            _layer_spec((half, D_MODEL), l, lambda i: (1, 0)),
        ],
        out_specs=[pl.BlockSpec((tm, D_MODEL), lambda i: (i, 0)), pl.BlockSpec((rs, D_MODEL), lambda i: (i, 0))],
        out_shape=[jax.ShapeDtypeStruct((mp, D_MODEL), F32), jax.ShapeDtypeStruct((ms, D_MODEL), F32)],
        compiler_params=_cparams("parallel"),
        name="mix_out",
    )(xp, xs, hmp, ogp, hms, ogs, w_out, w_out)


def _ffn_kernel(xp_ref, xs_ref, g_ref, wg_ref, wu_ref, wd_ref, gf_ref, op_ref, os_ref, h_scr, *, final):
    j = pl.program_id(1)
    tm = xp_ref.shape[0]

    @pl.when(j == 0)
    def _():
        for x_ref, o_ref, rows in ((xp_ref, op_ref, slice(0, tm)), (xs_ref, os_ref, slice(tm, None))):
            x = x_ref[...]
            h_scr[rows] = _rms(x, g_ref[...]).astype(BF16)
            o_ref[...] = x

    h = h_scr[...]
    a = _dot(h, wg_ref[...].astype(BF16))
    b = _dot(h, wu_ref[...].astype(BF16))
    t = (a * _sigmoid(a)) * b
    r = _dot(t.astype(BF16), wd_ref[...].astype(BF16))
    op_ref[...] += r[:tm]
    os_ref[...] += r[tm:]

    if final:
        @pl.when(j == pl.num_programs(1) - 1)
        def _():
            op_ref[...] = _rms(op_ref[...], gf_ref[...])
            os_ref[...] = _rms(os_ref[...], gf_ref[...])


def _ffn(xp, xs, g, w_gate, w_up, w_down, g_final, l, tm, final, tf=256):
    mp, ms = xp.shape[0], xs.shape[0]
    tiles, rs = _sample_share(mp, ms, tm)
    return pl.pallas_call(
        functools.partial(_ffn_kernel, final=final),
        grid=(tiles, D_FF // tf),
        in_specs=[
            pl.BlockSpec((tm, D_MODEL), lambda i, j: (i, 0)),
            pl.BlockSpec((rs, D_MODEL), lambda i, j: (i, 0)),
            _layer_spec((1, D_MODEL), l, lambda i, j: (0, 0)),
            _layer_spec((D_MODEL, tf), l, lambda i, j: (0, j)),
            _layer_spec((D_MODEL, tf), l, lambda i, j: (0, j)),
            _layer_spec((tf, D_MODEL), l, lambda i, j: (j, 0)),
            pl.BlockSpec((1, D_MODEL), lambda i, j: (0, 0)),
        ],
        out_specs=[pl.BlockSpec((tm, D_MODEL), lambda i, j: (i, 0)), pl.BlockSpec((rs, D_MODEL), lambda i, j: (i, 0))],
        out_shape=[jax.ShapeDtypeStruct((mp, D_MODEL), F32), jax.ShapeDtypeStruct((ms, D_MODEL), F32)],
        scratch_shapes=[pltpu.VMEM((tm + rs, D_MODEL), BF16)],
        compiler_params=_cparams("parallel", "arbitrary"),
        name="ffn",
    )(xp, xs, g, w_gate, w_up, w_down, g_final)


def _softmax_rows(s):
    s = s - jnp.max(s, axis=-1, keepdims=True)
    p = jnp.exp(s)
    return p / jnp.sum(p, axis=-1, keepdims=True)


def _prompt_attention(q, k_ref, v_ref):
    sls = [slice(h * X_DH, (h + 1) * X_DH) for h in range(X_HEADS)]
    s = [_dot_nt(q[:, sl].astype(BF16), k_ref[:, sl].astype(BF16)) * (X_DH ** -0.5) for sl in sls]
    p = [_softmax_rows(x).astype(BF16) for x in s]
    return jnp.concatenate([_dot(p[h], v_ref[:, sls[h]].astype(BF16)) for h in range(X_HEADS)], axis=1)


def _sample_attention(q, k_ref, v_ref):
    nb = q.shape[0]
    nk = N_MEM * X_HEADS
    qh = [q[:, h * X_DH:(h + 1) * X_DH] for h in range(X_HEADS)]
    row = lax.broadcasted_iota(jnp.int32, (SUBLANES, nk), 0)
    col = lax.broadcasted_iota(jnp.int32, (SUBLANES, nk), 1)
    own = (col & (X_HEADS - 1)) == (row & (X_HEADS - 1))
    row8 = lax.broadcasted_iota(jnp.int32, (SUBLANES, X_DH), 0)
    outs = [jnp.zeros((nb, X_DH), F32) for _ in range(X_HEADS)]
    scores = []
    for s in range(nb):
        q8 = jnp.zeros((SUBLANES, X_DH), F32)
        for h in range(X_HEADS):
            q8 = jnp.where(row8 == h, qh[h][s:s + 1, :], q8)
        scores.append(_dot_nt(q8.astype(BF16), k_ref[s].astype(BF16)) * (X_DH ** -0.5))
    probs = [_softmax_rows(jnp.where(own, sc, -jnp.inf)).astype(BF16) for sc in scores]
    o8s = [_dot(probs[s], v_ref[s].astype(BF16)) for s in range(nb)]
    for s in range(nb):
        for h in range(X_HEADS):
            outs[h] = jnp.where(row8[:nb] == s, o8s[s][h:h + 1, :], outs[h])
    return jnp.concatenate(outs, axis=1)


def _xattn_kernel(xp_ref, xs_ref, g_ref, wq_ref, kp_ref, vp_ref, ks_ref, vs_ref, wo_ref, op_ref, os_ref):
    tm = xp_ref.shape[0]
    h = jnp.concatenate([_rms(xp_ref[...], g_ref[...]), _rms(xs_ref[...], g_ref[...])], axis=0)
    q = _dot(h.astype(BF16), wq_ref[...].astype(BF16))
    o = jnp.concatenate([_prompt_attention(q[:tm], kp_ref, vp_ref), _sample_attention(q[tm:], ks_ref, vs_ref)],
                        axis=0)
    r = _dot(o.astype(BF16), wo_ref[...].astype(BF16))
    op_ref[...] = xp_ref[...] + r[:tm]
    os_ref[...] = xs_ref[...] + r[tm:]


def _xattn(xp, xs, g, wq, mk_p, mv_p, mk_s, mv_s, wo, l, tm, seq):
    mp, ms = xp.shape[0], xs.shape[0]
    tiles, rs = _sample_share(mp, ms, tm)
    dq = X_HEADS * X_DH
    per_seq = seq // tm
    kvp_spec = pl.BlockSpec((None, N_MEM, dq), lambda i: (l, i // per_seq, 0))
    kvs_spec = _layer_spec((rs, N_MEM * X_HEADS, X_DH), l, lambda i: (i, 0, 0))
    return pl.pallas_call(
        _xattn_kernel,
        grid=(tiles,),
        in_specs=[
            pl.BlockSpec((tm, D_MODEL), lambda i: (i, 0)),
            pl.BlockSpec((rs, D_MODEL), lambda i: (i, 0)),
            _layer_spec((1, D_MODEL), l, lambda i: (0, 0)),
            _layer_spec((D_MODEL, dq), l, lambda i: (0, 0)),
            kvp_spec, kvp_spec, kvs_spec, kvs_spec,
            _layer_spec((dq, D_MODEL), l, lambda i: (0, 0)),
        ],
        out_specs=[pl.BlockSpec((tm, D_MODEL), lambda i: (i, 0)), pl.BlockSpec((rs, D_MODEL), lambda i: (i, 0))],
        out_shape=[jax.ShapeDtypeStruct((mp, D_MODEL), F32), jax.ShapeDtypeStruct((ms, D_MODEL), F32)],
        compiler_params=_cparams("parallel"),
        name="xattn",
    )(xp, xs, g, wq, mk_p, mv_p, mk_s, mv_s, wo)


def _mlstm_prompt_kernel(q_ref, k_ref, v_ref, mo_ref, gt_ref, gb_ref, mn_ref,
                         hm_ref, c_ref, n_ref, m_ref, c_scr, n_scr, m_scr):
    L = q_ref.shape[0]
    c = pl.program_id(1)

    @pl.when(c == 0)
    def _():
        c_scr[...] = jnp.zeros_like(c_scr)
        n_scr[...] = jnp.zeros_like(n_scr)
        m_scr[...] = jnp.zeros_like(m_scr)

    gates = gt_ref[...] + gb_ref[...]
    capped = _softcap(gates)
    logf = -_softplus(-capped)
    row = lax.broadcasted_iota(jnp.int32, (L, L), 0)
    col = lax.broadcasted_iota(jnp.int32, (L, L), 1)
    causal = row >= col
    bcum = _dot_f32(causal.astype(F32), logf)
    lane = lax.broadcasted_iota(jnp.int32, (L, LANES), 1)
    z = jnp.where(lane < GATE_F, capped, bcum)
    zt = z.T

    hs = range(M_HEADS)
    i_col = [z[:, GATE_I + h:GATE_I + h + 1] for h in hs]
    b_col = [z[:, GATE_F + h:GATE_F + h + 1] for h in hs]
    m_prev = [m_scr[0:1, h:h + 1] for h in hs]
    d = [jnp.where(causal, b_col[h] + (zt[GATE_I + h:GATE_I + h + 1, :] - zt[GATE_F + h:GATE_F + h + 1, :]),
                   -jnp.inf) for h in hs]
    inter = [b_col[h] + m_prev[h] for h in hs]
    m_t = [jnp.maximum(inter[h], jnp.max(d[h], axis=1, keepdims=True)) for h in hs]
    q = [q_ref[:, h * M_DK:(h + 1) * M_DK] for h in hs]
    k = [k_ref[:, h * M_DK:(h + 1) * M_DK] * (M_DK ** -0.5) for h in hs]
    qb = [q[h].astype(BF16) for h in hs]
    vb = [v_ref[:, h * M_DV:(h + 1) * M_DV].astype(BF16) for h in hs]
    qk = [_dot_nt(qb[h], k[h].astype(BF16)) for h in hs]
    qc = [_dot(qb[h], c_scr[h].astype(BF16)) for h in hs]
    w = [jnp.exp(d[h] - m_t[h]) * qk[h] for h in hs]
    a = [jnp.exp(inter[h] - m_t[h]) for h in hs]
    wv = [_dot(w[h].astype(BF16), vb[h]) for h in hs]
    nq = [a[h] * jnp.sum(q[h] * n_scr[h], axis=1, keepdims=True) + jnp.sum(w[h], axis=1, keepdims=True) for h in hs]
    hh = [(a[h] * qc[h] + wv[h]) / jnp.maximum(jnp.abs(nq[h]), jnp.exp(-m_t[h])) for h in hs]
    for h in hs:
        y = _rms(hh[h], mn_ref[:, h * M_DV:(h + 1) * M_DV])
        y = y * _sigmoid(mo_ref[:, h * M_DV:(h + 1) * M_DV])
        hm_ref[:, h * M_DV:(h + 1) * M_DV] = y.astype(BF16)

    m_new = [m_t[h][L - 1:L, :] for h in hs]
    b_last = [b_col[h][L - 1:L, :] for h in hs]
    kw = [k[h] * jnp.exp(b_last[h] - b_col[h] + i_col[h] - m_new[h]) for h in hs]
    upd = [_dot_tn(kw[h].astype(BF16), vb[h]) for h in hs]
    for h in hs:
        decay = jnp.exp(b_last[h] + m_prev[h] - m_new[h])
        c_scr[h] = decay * c_scr[h] + upd[h]
        n_scr[h] = decay * n_scr[h] + jnp.sum(kw[h], axis=0, keepdims=True)
        m_scr[0:1, h:h + 1] = m_new[h]

    @pl.when(c == pl.num_programs(1) - 1)
    def _():
        c_ref[...] = c_scr[...]
        n_ref[...] = n_scr[...]
        m_ref[...] = m_scr[...]


def _sample_block_index(b, nc, n_blocks):
    every = b * nc // n_blocks
    assert every * n_blocks == b * nc and nc % every == 0, (b, nc, n_blocks)
    per = nc // every
    return (lambda i, c: i * per + c // every), every


def _token_rows(pr_ref, nb, off):
    return pr_ref[pl.ds(off, nb, stride=ROWS_PER_TOKEN), :]


def _token_rows2(pr_ref, nb, off):
    return jnp.concatenate([_token_rows(pr_ref, nb, off), _token_rows(pr_ref, nb, off + 1)], axis=1)


def _pick_row(tiles):
    rid = lax.broadcasted_iota(jnp.int32, tiles[0].shape, 0)
    out = tiles[0]
    for s in range(1, len(tiles)):
        out = jnp.where(rid == s, tiles[s], out)
    return out


def _columns(tiles):
    pad = LANES - SUBLANES * len(tiles)
    stack = jnp.concatenate(tiles + ([jnp.zeros((pad, LANES), F32)] if pad else []), axis=0)
    return stack.T


def _mlstm_sample_kernel(pr_ref, gb_ref, mn_ref, c_ref, n_ref, m_ref, acc_ref,
                         hm_ref, co_ref, no_ref, mo_ref):
    del acc_ref
    nb = m_ref.shape[0]
    gates = _token_rows(pr_ref, nb, GATE_ROW) + gb_ref[...]
    capped = _softcap(gates)
    logf = -_softplus(-capped)
    i4 = capped[:, GATE_I:GATE_I + M_HEADS]
    f4 = logf[:, GATE_F:GATE_F + M_HEADS]
    inter = f4 + m_ref[...]
    m_new = jnp.maximum(inter, i4)
    a4 = jnp.exp(inter - m_new)
    w4 = jnp.exp(i4 - m_new)
    e4 = jnp.exp(-m_new)
    mo_ref[...] = m_new

    scale = M_DK ** -0.5
    ys, n_rows = [], []
    for h in range(M_HEADS):
        a, wi, e = a4[:, h:h + 1], w4[:, h:h + 1], e4[:, h:h + 1]
        q = _token_rows(pr_ref, nb, ROW_MQ + h)
        k = _token_rows(pr_ref, nb, ROW_MK + h) * scale
        v = _token_rows2(pr_ref, nb, ROW_MV + 2 * h)
        og = _token_rows2(pr_ref, nb, ROW_MO + 2 * h)
        n_old = n_ref[:, h * M_DK:(h + 1) * M_DK]
        wk = wi * k
        n_rows.append(a * n_old + wk)
        wts = wi * jnp.sum(q * k, axis=1, keepdims=True)
        nq = a * jnp.sum(q * n_old, axis=1, keepdims=True) + wts
        wk_cols = _columns([wk])
        a_rows = jnp.broadcast_to(a, (nb, M_DV))
        qb = q.astype(BF16)
        qc = []
        for s in range(nb):
            c_old = c_ref[s, h]
            qc.append(_dot(qb, c_old.astype(BF16)))
            co_ref[s, h] = a_rows[s:s + 1, :] * c_old + wk_cols[:, s:s + 1] * v[s:s + 1, :]
        num = a * _pick_row(qc) + wts * v
        hh = num / jnp.maximum(jnp.abs(nq), e)
        ys.append(_rms(hh, mn_ref[:, h * M_DV:(h + 1) * M_DV]) * _sigmoid(og))
    hm_ref[...] = jnp.concatenate(ys, axis=1).astype(BF16)
    no_ref[...] = jnp.concatenate(n_rows, axis=1)


def _mlstm_kernel(*refs, every):
    p_in, s_in, p_out, s_out, scr = refs[:7], refs[7:14], refs[14:18], refs[18:22], refs[22:]
    _mlstm_prompt_kernel(*p_in, *p_out, *scr)

    @pl.when(pl.program_id(1) % every == 0)
    def _():
        _mlstm_sample_kernel(*s_in, *s_out)


def _mlstm(proj3, proj_rows, gate_bias, m_norm, c_all, n_all, m_all, c_acc, l):
    b, t, _ = proj3.shape
    bs = m_all.shape[1]
    L = M_CHUNK
    nb = SAMPLE_BLOCK
    dqk = M_HEADS * M_DK
    dv = M_HEADS * M_DV
    blk, every = _sample_block_index(b, t // L, bs // nb)
    c_spec = _layer_spec((nb, M_HEADS, M_DK, M_DV), l, lambda i, c: (blk(i, c), 0, 0, 0))
    return pl.pallas_call(
        functools.partial(_mlstm_kernel, every=every),
        grid=(b, t // L),
        in_specs=[
            pl.BlockSpec((None, L, dqk), lambda i, c: (i, c, 0)),
            pl.BlockSpec((None, L, dqk), lambda i, c: (i, c, 1)),
            pl.BlockSpec((None, L, dv), lambda i, c: (i, c, 1)),
            pl.BlockSpec((None, L, dv), lambda i, c: (i, c, 2)),
            pl.BlockSpec((None, L, LANES), lambda i, c: (i, c, GATE_ROW)),
            _layer_spec((1, LANES), l, lambda i, c: (0, 0)),
            _layer_spec((1, dv), l, lambda i, c: (0, 0)),
            pl.BlockSpec((nb * ROWS_PER_TOKEN, LANES), lambda i, c: (blk(i, c), 0)),
            _layer_spec((1, LANES), l, lambda i, c: (0, 0)),
            _layer_spec((1, dv), l, lambda i, c: (0, 0)),
            c_spec,
            _layer_spec((nb, dqk), l, lambda i, c: (blk(i, c), 0)),
            _layer_spec((nb, M_HEADS), l, lambda i, c: (blk(i, c), 0)),
            pl.BlockSpec(memory_space=pl.ANY),
        ],
        out_specs=[
            pl.BlockSpec((None, L, dv), lambda i, c: (i, c, 0)),
            pl.BlockSpec((None, M_HEADS, M_DK, M_DV), lambda i, c: (i, 0, 0, 0)),
            pl.BlockSpec((None, M_HEADS, 1, M_DK), lambda i, c: (i, 0, 0, 0)),
            pl.BlockSpec((None, 1, LANES), lambda i, c: (i, 0, 0)),
            pl.BlockSpec((nb, dv), lambda i, c: (blk(i, c), 0)),
            c_spec,
            pl.BlockSpec((nb, dqk), lambda i, c: (blk(i, c), 0)),
            pl.BlockSpec((nb, M_HEADS), lambda i, c: (blk(i, c), 0)),
        ],
        out_shape=[
            jax.ShapeDtypeStruct((b, t, dv), BF16),
            jax.ShapeDtypeStruct((b, M_HEADS, M_DK, M_DV), F32),
            jax.ShapeDtypeStruct((b, M_HEADS, 1, M_DK), F32),
            jax.ShapeDtypeStruct((b, 1, LANES), F32),
            jax.ShapeDtypeStruct((bs, dv), BF16),
            jax.ShapeDtypeStruct(c_all.shape, F32),
            jax.ShapeDtypeStruct((bs, dqk), F32),
            jax.ShapeDtypeStruct((bs, M_HEADS), F32),
        ],
        scratch_shapes=[
            pltpu.VMEM((M_HEADS, M_DK, M_DV), F32),
            pltpu.VMEM((M_HEADS, 1, M_DK), F32),
            pltpu.VMEM((1, LANES), F32),
        ],
        input_output_aliases={13: 5},
        compiler_params=_cparams("parallel", "arbitrary"),
        name="mlstm",
    )(proj3, proj3, proj3, proj3, proj3, gate_bias, m_norm,
      proj_rows, gate_bias, m_norm, c_all, n_all, m_all, c_acc)


def _gdn_gates(gates, alog):
    log_a = -jnp.exp(alog) * _softplus(gates)
    beta = _sigmoid(gates)
    return log_a, beta


def _gdn_prompt_kernel(x_ref, gz_ref, gt_ref, gb_ref, al_ref, cw_ref, gn_ref,
                       og_ref, s_ref, cv_ref, s_scr, tail_scr):
    Lb = x_ref.shape[0]
    L, P = G_CHUNK, G_PAIR
    n_pair = Lb // P
    c = pl.program_id(1)

    @pl.when(c == 0)
    def _():
        s_scr[...] = jnp.zeros_like(s_scr)
        tail_scr[...] = jnp.zeros_like(tail_scr)

    log_a, beta = _gdn_gates(gt_ref[...] + gb_ref[...], al_ref[...])
    row = lax.broadcasted_iota(jnp.int32, (Lb, Lb), 0)
    col = lax.broadcasted_iota(jnp.int32, (Lb, Lb), 1)
    shift = int(math.log2(L))
    same_chunk = jnp.right_shift(row, shift) == jnp.right_shift(col, shift)
    gcum = _dot_f32(jnp.where(same_chunk & (row >= col), 1.0, 0.0), log_a)
    gtot = _dot_f32(jnp.where(same_chunk, 1.0, 0.0), log_a)
    lane = lax.broadcasted_iota(jnp.int32, (Lb, LANES), 1)
    zt = jnp.where(lane < GATE_B, gcum, beta).T

    rp = lax.broadcasted_iota(jnp.int32, (P, P), 0)
    cp = lax.broadcasted_iota(jnp.int32, (P, P), 1)
    same = jnp.right_shift(rp, shift) == jnp.right_shift(cp, shift)
    incl = same & (rp >= cp)
    strict = same & (rp > cp)

    def conv_cols(c0):
        cols = slice(c0, c0 + LANES)
        w = [cw_ref[j:j + 1, cols] for j in range(CONV_W)]
        xp = jnp.concatenate([tail_scr[:, cols], x_ref[:, cols]], axis=0)
        x1 = pltpu.roll(xp, 1, axis=0)
        y = (xp * w[3] + x1 * w[2]) + pltpu.roll(xp * w[1] + x1 * w[0], 2, axis=0)
        y = y[SUBLANES:]
        return y * _sigmoid(y)

    heads = []
    for h in range(G_HEADS):
        qc = conv_cols(h * G_DK)
        kc = conv_cols(G_HEADS * G_DK + h * G_DK)
        vc = conv_cols(2 * G_HEADS * G_DK + h * G_DV)
        q2 = qc * (lax.rsqrt(jnp.sum(qc * qc, axis=1, keepdims=True) + EPS) * (G_DK ** -0.5))
        k2 = kc * lax.rsqrt(jnp.sum(kc * kc, axis=1, keepdims=True) + EPS)
        g_col = gcum[:, GATE_A + h:GATE_A + h + 1]
        t_col = gtot[:, GATE_A + h:GATE_A + h + 1]
        b_col = beta[:, GATE_B + h:GATE_B + h + 1]
        eg = jnp.exp(g_col)
        heads.append(dict(
            qb=q2.astype(BF16), kb=k2.astype(BF16), eg=eg, b_col=b_col,
            vek=jnp.concatenate([vc, eg * k2], axis=1),
            kd=(k2 * jnp.exp(t_col - g_col)).astype(BF16),
            dS=jnp.exp(t_col),
        ))

    blocks = []
    for h in range(G_HEADS):
        hd = heads[h]
        for b in range(n_pair):
            r0 = b * P
            kb = hd["kb"][r0:r0 + P]
            g_col = gcum[r0:r0 + P, GATE_A + h:GATE_A + h + 1]
            g_row = zt[GATE_A + h:GATE_A + h + 1, r0:r0 + P]
            b_row = zt[GATE_B + h:GATE_B + h + 1, r0:r0 + P]
            dec = jnp.exp(jnp.where(incl, g_col - g_row, -jnp.inf))
            a_mat = jnp.where(strict, dec * _dot_nt(kb, kb) * b_row, 0.0)
            qkd = (_dot_nt(hd["qb"][r0:r0 + P], kb) * dec).astype(BF16)
            blocks.append(dict(h=h, r0=r0, mpow=-a_mat, qacc=-a_mat, qkd=qkd))

    for blk in blocks:
        blk["mb"] = blk["mpow"].astype(BF16)
    for _ in range(int(math.log2(L)) - 1):
        for blk in blocks:
            blk["mpow"] = _dot(blk["mb"], blk["mb"])
            blk["mb"] = blk["mpow"].astype(BF16)
        for blk in blocks:
            blk["qacc"] = blk["qacc"] + blk["mpow"] + _dot(blk["qacc"].astype(BF16), blk["mb"])

    for blk in blocks:
        hd = heads[blk["h"]]
        r0 = blk["r0"]
        vek = hd["vek"][r0:r0 + P]
        t = (vek + _dot(blk["qacc"].astype(BF16), vek.astype(BF16))) * hd["b_col"][r0:r0 + P]
        blk["uv"] = t[:, :G_DV]
        blk["kw"] = t[:, G_DV:].astype(BF16)

    states = [s_scr[h] for h in range(G_HEADS)]
    wus = [[] for _ in range(G_HEADS)]
    qss = [[] for _ in range(G_HEADS)]
    for ci in range(Lb // L):
        r0 = ci * L
        for h in range(G_HEADS):
            hd = heads[h]
            blk = blocks[h * n_pair + r0 // P]
            o0 = r0 - blk["r0"]
            lhs = jnp.concatenate([blk["kw"][o0:o0 + L], hd["qb"][r0:r0 + L]], axis=0)
            r = _dot(lhs, states[h].astype(BF16))
            wu = (blk["uv"][o0:o0 + L] - r[:L]).astype(BF16)
            qss[h].append(r[L:])
            wus[h].append(wu)
            states[h] = hd["dS"][r0:r0 + 1] * states[h] + _dot_tn(hd["kd"][r0:r0 + L], wu)
    for h in range(G_HEADS):
        s_scr[h] = states[h]

    for h in range(G_HEADS):
        hd = heads[h]
        for b in range(n_pair):
            r0 = b * P
            blk = blocks[h * n_pair + b]
            i0 = r0 // L
            wu = jnp.concatenate(wus[h][i0:i0 + P // L], axis=0)
            qs = jnp.concatenate(qss[h][i0:i0 + P // L], axis=0)
            o = hd["eg"][r0:r0 + P] * qs + _dot(blk["qkd"], wu)
            y = _rms(o, gn_ref[:, h * G_DV:(h + 1) * G_DV])
            gz = gz_ref[r0:r0 + P, h * G_DV:(h + 1) * G_DV]
            og_ref[r0:r0 + P, h * G_DV:(h + 1) * G_DV] = (y * (gz * _sigmoid(gz))).astype(BF16)

    tail_scr[...] = x_ref[Lb - SUBLANES:Lb, :]

    @pl.when(c == pl.num_programs(1) - 1)
    def _():
        s_ref[...] = s_scr[...]
        cv_ref[...] = x_ref[Lb - (CONV_W - 1):Lb, :]


def _gdn_sample_kernel(pr_ref, gb_ref, al_ref, cw_ref, gn_ref, s_ref, cb_ref, acc_ref,
                       og_ref, so_ref, co_ref):
    del acc_ref
    nb = cb_ref.shape[2]
    log_a, beta = _gdn_gates(_token_rows(pr_ref, nb, GATE_ROW) + gb_ref[...], al_ref[...])
    eg8 = jnp.exp(log_a)

    ys = []
    for r in range(CONV_ROWS):
        xr = _token_rows(pr_ref, nb, ROW_GX + r)
        y = xr * cw_ref[CONV_W - 1, r:r + 1, :]
        for j in range(CONV_W - 1):
            y = y + cb_ref[j, r] * cw_ref[j, r:r + 1, :]
        co_ref[0, r] = cb_ref[1, r]
        co_ref[1, r] = cb_ref[2, r]
        co_ref[2, r] = xr
        ys.append(y * _sigmoid(y))

    hs = range(G_HEADS)
    qc, kc, v = ys[:G_HEADS], ys[G_HEADS:2 * G_HEADS], ys[2 * G_HEADS:]
    q = [qc[h] * lax.rsqrt(jnp.sum(qc[h] * qc[h], axis=1, keepdims=True) + EPS) * (G_DK ** -0.5) for h in hs]
    k = [kc[h] * lax.rsqrt(jnp.sum(kc[h] * kc[h], axis=1, keepdims=True) + EPS) for h in hs]
    eg = [eg8[:, GATE_A + h:GATE_A + h + 1] for h in hs]
    bt = [beta[:, GATE_B + h:GATE_B + h + 1] for h in hs]
    qb = [x.astype(BF16) for x in q]
    kb = [x.astype(BF16) for x in k]
    ks = [[_dot(kb[h], s_ref[s, h].astype(BF16)) for s in range(nb)] for h in hs]
    qs = [[_dot(qb[h], s_ref[s, h].astype(BF16)) for s in range(nb)] for h in hs]
    u = [bt[h] * (v[h] - eg[h] * _pick_row(ks[h])) for h in hs]
    o = [eg[h] * _pick_row(qs[h]) + jnp.sum(q[h] * k[h], axis=1, keepdims=True) * u[h] for h in hs]
    k_cols = [_columns([k[h]]) for h in hs]
    for h in hs:
        eg_rows = jnp.broadcast_to(eg[h], (nb, G_DV))
        for s in range(nb):
            so_ref[s, h] = eg_rows[s:s + 1, :] * s_ref[s, h] + k_cols[h][:, s:s + 1] * u[h][s:s + 1, :]
    outs = []
    for h in hs:
        gz = _token_rows(pr_ref, nb, ROW_GZ + h)
        outs.append(_rms(o[h], gn_ref[:, h * G_DV:(h + 1) * G_DV]) * (gz * _sigmoid(gz)))
    og_ref[...] = jnp.concatenate(outs, axis=1).astype(BF16)


def _gdn_kernel(*refs, every):
    p_in, s_in, p_out, s_out, scr = refs[:7], refs[7:15], refs[15:18], refs[18:21], refs[21:]
    _gdn_prompt_kernel(*p_in, *p_out, *scr)

    @pl.when(pl.program_id(1) % every == 0)
    def _():
        _gdn_sample_kernel(*s_in, *s_out)


def _gdn(proj3, proj_rows, gate_bias, alog, conv_w, conv_w_rows, g_norm, s_all, conv_t, s_acc, l):
    b, t, _ = proj3.shape
    bs = s_all.shape[1]
    Lb = G_BLOCK
    nb = SAMPLE_BLOCK
    dv = G_HEADS * G_DV
    blk, every = _sample_block_index(b, t // Lb, bs // nb)
    s_spec = _layer_spec((nb, G_HEADS, G_DK, G_DV), l, lambda i, c: (blk(i, c), 0, 0, 0))
    cv_shape = (CONV_W - 1, CONV_ROWS, nb, LANES)
    return pl.pallas_call(
        functools.partial(_gdn_kernel, every=every),
        grid=(b, t // Lb),
        in_specs=[
            pl.BlockSpec((None, Lb, G_CONV_CH), lambda i, c: (i, c, 1)),
            pl.BlockSpec((None, Lb, dv), lambda i, c: (i, c, 6)),
            pl.BlockSpec((None, Lb, LANES), lambda i, c: (i, c, GATE_ROW)),
            _layer_spec((1, LANES), l, lambda i, c: (0, 0)),
            _layer_spec((1, LANES), l, lambda i, c: (0, 0)),
            _layer_spec((CONV_W, G_CONV_CH), l, lambda i, c: (0, 0)),
            _layer_spec((1, dv), l, lambda i, c: (0, 0)),
            pl.BlockSpec((nb * ROWS_PER_TOKEN, LANES), lambda i, c: (blk(i, c), 0)),
            _layer_spec((1, LANES), l, lambda i, c: (0, 0)),
            _layer_spec((1, LANES), l, lambda i, c: (0, 0)),
            _layer_spec((CONV_W, CONV_ROWS, LANES), l, lambda i, c: (0, 0, 0)),
            _layer_spec((1, dv), l, lambda i, c: (0, 0)),
            s_spec,
            _layer_spec(cv_shape, l, lambda i, c: (0, 0, blk(i, c), 0)),
            pl.BlockSpec(memory_space=pl.ANY),
        ],
        out_specs=[
            pl.BlockSpec((None, Lb, dv), lambda i, c: (i, c, 0)),
            pl.BlockSpec((None, G_HEADS, G_DK, G_DV), lambda i, c: (i, 0, 0, 0)),
            pl.BlockSpec((None, CONV_W - 1, G_CONV_CH), lambda i, c: (i, 0, 0)),
            pl.BlockSpec((nb, dv), lambda i, c: (blk(i, c), 0)),
            s_spec,
            pl.BlockSpec(cv_shape, lambda i, c: (0, 0, blk(i, c), 0)),
        ],
        out_shape=[
            jax.ShapeDtypeStruct((b, t, dv), BF16),
            jax.ShapeDtypeStruct((b, G_HEADS, G_DK, G_DV), F32),
            jax.ShapeDtypeStruct((b, CONV_W - 1, G_CONV_CH), F32),
            jax.ShapeDtypeStruct((bs, dv), BF16),
            jax.ShapeDtypeStruct(s_all.shape, F32),
            jax.ShapeDtypeStruct((CONV_W - 1, CONV_ROWS, bs, LANES), F32),
        ],
        scratch_shapes=[
            pltpu.VMEM((G_HEADS, G_DK, G_DV), F32),
            pltpu.VMEM((SUBLANES, G_CONV_CH), F32),
        ],
        input_output_aliases={14: 4},
        compiler_params=_cparams("parallel", "arbitrary"),
        name="gdn",
    )(proj3, proj3, proj3, gate_bias, alog, conv_w, g_norm,
      proj_rows, gate_bias, alog, conv_w_rows, g_norm, s_all, conv_t, s_acc)


def _arrange_w_in_kernel(wt_ref, o_ref):
    n_m = 2 * M_HEADS * M_DK + 2 * M_HEADS * M_DV
    n_g = 4 * G_HEADS * G_DK
    g0 = n_m + 2 * M_HEADS
    tk = wt_ref.shape[1]
    step = 2 * LANES
    for dst, src, n in ((0, 0, n_m), (n_m, g0, n_g)):
        for c in range(0, n, step):
            o_ref[:, dst + c:dst + c + step] = wt_ref[src + c:src + c + step, :].T.astype(BF16)
    n_gate = wt_ref.shape[0] - N_WIDE
    gate = jnp.concatenate([wt_ref[n_m:g0, :], wt_ref[g0 + n_g:, :], jnp.zeros((LANES - n_gate, tk), F32)], axis=0)
    o_ref[:, N_WIDE:N_WIDE + LANES] = gate.T.astype(BF16)
    o_ref[:, N_WIDE + LANES:] = jnp.zeros((tk, N_PROJ - N_WIDE - LANES), BF16)


def _arrange_w_in(w_in, tk=256):
    depth, rows, n_in = w_in.shape
    return pl.pallas_call(
        _arrange_w_in_kernel,
        grid=(depth, rows // tk),
        in_specs=[pl.BlockSpec((None, n_in, tk), lambda l, i: (l, 0, i))],
        out_specs=pl.BlockSpec((None, tk, N_PROJ), lambda l, i: (l, i, 0)),
        out_shape=jax.ShapeDtypeStruct((depth, rows, N_PROJ), BF16),
        compiler_params=_cparams("parallel", "parallel"),
        name="arrange_w_in",
    )(jnp.swapaxes(w_in, 1, 2))


def _lane_row(parts, depth):
    row = jnp.zeros((depth, 1, LANES), F32)
    for off, arr in parts:
        row = row.at[:, 0, off:off + arr.shape[-1]].set(arr.astype(F32))
    return row


def kernel(x_prompt, x_sample, mem_prompt, state_mlstm_C, state_mlstm_n, state_mlstm_m, state_gdn_S,
           state_gdn_conv, cache_mem_k, cache_mem_v, norm_mix, w_in, mlstm_b_i, mlstm_b_f, mlstm_norm,
           gdn_conv_w, gdn_A_log, gdn_dt_bias, gdn_norm, w_out, norm_xattn, norm_mem, xattn_wq, xattn_wk,
           xattn_wv, xattn_wo, norm_ffn, ffn_w_gate, ffn_w_up, ffn_w_down, norm_final):
    bp, seq, _ = x_prompt.shape
    bs = x_sample.shape[0]
    depth = w_in.shape[0]
    dq = X_HEADS * X_DH
    tm_p, tm_wide = 512, 1024

    w_in_b = _arrange_w_in(w_in)
    gate_bias = _lane_row([(GATE_I, mlstm_b_i), (GATE_F, mlstm_b_f), (GATE_A, gdn_dt_bias)], depth)
    alog = _lane_row([(GATE_A, gdn_A_log)], depth)
    g_mix, g_x, g_mem, g_ffn = (g.reshape(depth, 1, D_MODEL) for g in (norm_mix, norm_xattn, norm_mem, norm_ffn))
    g_final = norm_final.reshape(1, D_MODEL)
    m_norm = mlstm_norm.reshape(depth, 1, M_HEADS * M_DV)
    g_norm = gdn_norm.reshape(depth, 1, G_HEADS * G_DV)
    conv_w_rows = gdn_conv_w.reshape(depth, CONV_W, CONV_ROWS, LANES)
    conv_t = jnp.transpose(state_gdn_conv.reshape(depth, bs, CONV_W - 1, CONV_ROWS, LANES), (0, 2, 3, 1, 4))
    n_all = state_mlstm_n.reshape(depth, bs, M_HEADS * M_DK)
    mem_k = cache_mem_k.reshape(depth, bs, N_MEM * X_HEADS, X_DH)
    mem_v = cache_mem_v.reshape(depth, bs, N_MEM * X_HEADS, X_DH)

    mk_all, mv_all = _mem_kv(mem_prompt.reshape(bp * N_MEM, D_MODEL), g_mem, xattn_wk, xattn_wv)

    xp = x_prompt.reshape(bp * seq, D_MODEL)
    xs = x_sample.reshape(bs, D_MODEL)
    c_acc = jnp.zeros(state_mlstm_C.shape, F32)
    s_acc = jnp.zeros(state_gdn_S.shape, F32)
    outs = [[] for _ in range(8)]
    for l in range(depth):
        proj_p, proj_s = _proj(xp, xs, g_mix, w_in_b, l, tm_wide)
        proj3 = proj_p.reshape(bp, seq, N_PROJ)
        proj_rows = proj_s.reshape(bs * ROWS_PER_TOKEN, LANES)
        hm_p, c_p, n_p, m_p, hm_s, c_acc, n_s, m_s = _mlstm(
            proj3, proj_rows, gate_bias, m_norm, state_mlstm_C, n_all, state_mlstm_m, c_acc, l)
        og_p, s_p, cv_p, og_s, s_acc, cv_s = _gdn(
            proj3, proj_rows, gate_bias, alog, gdn_conv_w, conv_w_rows, g_norm, state_gdn_S, conv_t, s_acc, l)
        for lst, val in zip(outs[:5], (c_p, n_p.reshape(bp, M_HEADS, M_DK), m_p[:, 0, :M_HEADS], s_p, cv_p)):
            lst.append(val)
        xp, xs = _mix_out(xp, xs, hm_p.reshape(bp * seq, -1), og_p.reshape(bp * seq, -1), hm_s, og_s, w_out, l, tm_p)
        xp, xs = _xattn(xp, xs, g_x, xattn_wq, mk_all, mv_all, mem_k, mem_v, xattn_wo, l, tm_p, seq)
        xp, xs = _ffn(xp, xs, g_ffn, ffn_w_gate, ffn_w_up, ffn_w_down, g_final, l, tm_wide, l == depth - 1)
        cv_s = jnp.transpose(cv_s, (2, 0, 1, 3)).reshape(bs, CONV_W - 1, G_CONV_CH)
        for lst, val in zip(outs[5:], (n_s.reshape(bs, M_HEADS, M_DK), m_s, cv_s)):
            lst.append(val)

    y_prompt = xp.reshape(bp, seq, D_MODEL)
    y_sample = xs.reshape(bs, 1, D_MODEL)
    pc, pn, pm, ps, pconv, sn, sm, sconv = (jnp.stack(o) for o in outs)
    mem_shape = (depth, bp, N_MEM, X_HEADS, X_DH)
    return (y_prompt, y_sample, pc, pn, pm, ps, pconv, mk_all.reshape(mem_shape), mv_all.reshape(mem_shape),
            c_acc, sn, sm, s_acc, sconv)
```
